```python
import jax
import jax.numpy as jnp
from jax import lax
import numpy as np

D_MODEL = 1024
BATCH = 8
SEQ = 4096
DEPTH = 4

GRID_W = 64
CTX_LEN = 256
HEAD_DIM = 64
EPS = 1e-6

ATT_Q_HEADS = 8
ATT_KV_HEADS = 2
GQA_GROUP = ATT_Q_HEADS // ATT_KV_HEADS
ATT_WIDTH = ATT_Q_HEADS * HEAD_DIM
ATT_KV_WIDTH = ATT_KV_HEADS * HEAD_DIM
ATT_SCALE = HEAD_DIM ** -0.5
Q_BLOCK = 128
ROPE_BASE = 10000.0
ROPE_PAIRS = HEAD_DIM // 4

RWKV_HEADS = 8
RWKV_HEAD_DIM = HEAD_DIM
RWKV_WIDTH = RWKV_HEADS * RWKV_HEAD_DIM
DECAY_LORA = 64
ICLR_LORA = 64
GATE_LORA = 128
SHIFT_CONV = 3
LNX_EPS = 64e-5

CMLP_GROUPS = 4
CMLP_WIDTH = 512
CMLP_GROUP_CH = CMLP_WIDTH // CMLP_GROUPS
CHUNK = 128

N_BRANCH = 3
FFN_DENSE = 2816
N_EXPERTS = 8
TOP_K = 2
FFN_EXPERT = 3584
MOE_BLOCK = 128

IN_SIZES = (ATT_KV_WIDTH, ATT_KV_WIDTH, 3 * RWKV_WIDTH, 2 * DECAY_LORA, 2 * ICLR_LORA,
            ATT_WIDTH, GATE_LORA, 2 * CMLP_WIDTH, N_BRANCH * D_MODEL)
IN_OFFSETS = tuple(int(s) for s in np.cumsum(IN_SIZES)[:-1])
IN_WIDTH = int(sum(IN_SIZES))
CTX_STATE_WIDTH = IN_OFFSETS[4]

kernel_name = 'hybrid_flow_block'


def _rms(x, g):
    xf = x.astype(jnp.float32)
    y = xf * lax.rsqrt(jnp.mean(xf * xf, axis=-1, keepdims=True) + EPS)
    return (y * g.astype(jnp.float32)).astype(x.dtype)


def _layernorm(x, g, b, eps):
    xf = x.astype(jnp.float32)
    mu = jnp.mean(xf, axis=-1, keepdims=True)
    var = jnp.mean(jnp.square(xf - mu), axis=-1, keepdims=True)
    y = (xf - mu) * lax.rsqrt(var + eps)
    return (y * g.astype(jnp.float32) + b.astype(jnp.float32)).astype(x.dtype)


def _modulation(cvec, w, b):
    m = jax.nn.silu(cvec) @ w + b
    return [t[..., None, :] for t in jnp.split(m, 6, axis=-1)]


def _axial_rope_tables(n_tokens, dtype):
    rows = n_tokens // GRID_W
    row = jnp.repeat(jnp.arange(rows, dtype=jnp.float32), GRID_W)
    col = jnp.tile(jnp.arange(GRID_W, dtype=jnp.float32), rows)
    freqs = ROPE_BASE ** (-jnp.arange(ROPE_PAIRS, dtype=jnp.float32) / ROPE_PAIRS)
    ang_r = row[:, None] * freqs
    ang_c = col[:, None] * freqs
    return (jnp.cos(ang_r).astype(dtype), jnp.sin(ang_r).astype(dtype),
            jnp.cos(ang_c).astype(dtype), jnp.sin(ang_c).astype(dtype))


def _rot_half(x, cos, sin):
    x1, x2 = jnp.split(x, 2, axis=-1)
    cos = cos[:, None, :]
    sin = sin[:, None, :]
    return jnp.concatenate([x1 * cos - x2 * sin, x2 * cos + x1 * sin], axis=-1)


def _apply_rope(x, rope):
    cos_r, sin_r, cos_c, sin_c = rope
    xr, xc = jnp.split(x, 2, axis=-1)
    return jnp.concatenate([_rot_half(xr, cos_r, sin_r), _rot_half(xc, cos_c, sin_c)], axis=-1)


def _gqa_softmax(q, k, v):
    s = jnp.einsum('bqhgd,bkhd->bhgqk', q, k).astype(jnp.float32) * ATT_SCALE
    p = jax.nn.softmax(s, axis=-1).astype(v.dtype)
    return jnp.einsum('bhgqk,bkhd->bqhgd', p, v)


def _attention_branch(qc, kc, vc, ql, kl, vl, qk_gain, rope):
    B, T, _ = ql.shape
    C = kc.shape[1]
    ql = _apply_rope(_rms(ql.reshape(B, T, ATT_Q_HEADS, HEAD_DIM), qk_gain[0]), rope)
    kl = _apply_rope(_rms(kl.reshape(B, T, ATT_KV_HEADS, HEAD_DIM), qk_gain[1]), rope)
    kc = _rms(kc.reshape(B, C, ATT_KV_HEADS, HEAD_DIM), qk_gain[1])
    vl = vl.reshape(B, T, ATT_KV_HEADS, HEAD_DIM)
    vc = vc.reshape(B, C, ATT_KV_HEADS, HEAD_DIM)
    k_all = jnp.concatenate([kc, kl], axis=1)
    v_all = jnp.concatenate([vc, vl], axis=1)
    nb = T // Q_BLOCK
    qb = ql.reshape(B, nb, Q_BLOCK, ATT_KV_HEADS, GQA_GROUP, HEAD_DIM).transpose(1, 0, 2, 3, 4, 5)
    ol = lax.map(lambda q: _gqa_softmax(q, k_all, v_all), qb)
    ol = ol.transpose(1, 0, 2, 3, 4, 5).reshape(B, T, ATT_WIDTH)
    oc = None
    if qc is not None:
        qc = _rms(qc.reshape(B, C, ATT_Q_HEADS, HEAD_DIM), qk_gain[0])
        qc = qc.reshape(B, C, ATT_KV_HEADS, GQA_GROUP, HEAD_DIM)
        oc = _gqa_softmax(qc, kc, vc).reshape(B, C, ATT_WIDTH)
    return oc, ol


def _heads(t):
    return t.reshape(t.shape[:-1] + (RWKV_HEADS, RWKV_HEAD_DIM))


def _short_conv(x, w):
    L = x.shape[1]
    pad = SHIFT_CONV // 2
    xp = jnp.pad(x, ((0, 0), (pad, pad), (0, 0)))
    return sum(xp[:, j:j + L] * w[j] for j in range(SHIFT_CONV))


def _rwkv_prepare(rkv, w_low, a_low, conv_w, w0, w2, a0, a2, key_k):
    r, k, v = jnp.split(_short_conv(rkv, conv_w), 3, axis=-1)
    kk = _heads(k * key_k[0]).astype(jnp.float32)
    kk = kk * lax.rsqrt(jnp.sum(kk * kk, axis=-1, keepdims=True) + 1e-12)
    decays, keys, iclrs = [], [], []
    for d in range(2):
        wd = (w0[d] + jnp.tanh(w_low[..., d * DECAY_LORA:(d + 1) * DECAY_LORA]) @ w2[d]).astype(jnp.float32)
        decays.append(_heads(jnp.exp(-jnp.exp(-jax.nn.softplus(-wd) - 0.5))))
        ad = jax.nn.sigmoid(a0[d] + a_low[..., d * ICLR_LORA:(d + 1) * ICLR_LORA] @ a2[d])
        iclrs.append(_heads(ad))
        keys.append(_heads(k * (1 + (ad - 1) * key_k[1])))
    return _heads(r), _heads(k), _heads(v), kk, decays, keys, iclrs


def _rwkv_scan(S0, r, w, k, v, kk, a, reverse, emit):
    xs = tuple(jnp.moveaxis(t.astype(jnp.float32), 1, 0) for t in (r, w, k, v, kk, a))

    def step(S, inp):
        r_t, w_t, k_t, v_t, kk_t, a_t = inp
        sa = jnp.einsum('bhvk,bhk->bhv', S, -kk_t)
        S = (S * w_t[:, :, None, :] + sa[..., None] * (kk_t * a_t)[:, :, None, :]
             + v_t[..., None] * k_t[:, :, None, :])
        y = jnp.einsum('bhvk,bhk->bhv', S, r_t) if emit else None
        return S, y

    S, ys = lax.scan(step, S0, xs, reverse=reverse)
    return S, (jnp.moveaxis(ys, 0, 1) if emit else None)


def _rwkv_readout(y, r, k, v, g_low, r_k, g2, lnx_g, lnx_b):
    B, L = y.shape[:2]
    yn = _layernorm(y, lnx_g.reshape(RWKV_HEADS, RWKV_HEAD_DIM),
                    lnx_b.reshape(RWKV_HEADS, RWKV_HEAD_DIM), LNX_EPS).astype(v.dtype)
    bonus = jnp.sum(r * k * r_k, axis=-1, keepdims=True) * v
    g = jax.nn.sigmoid(g_low) @ g2
    return (yn + bonus).reshape(B, L, RWKV_WIDTH) * g


def _rwkv_branch(prep_c, prep_l, glow_c, glow_l, r_k, g2, lnx_g, lnx_b, need_ctx):
    rc, kc, vc, kkc, wc, kdc, ac = prep_c
    rl, kl, vl, kkl, wl, kdl, al = prep_l
    B = rl.shape[0]
    S0 = jnp.zeros((B, RWKV_HEADS, RWKV_HEAD_DIM, RWKV_HEAD_DIM), jnp.float32)
    yl = 0.0
    yc = 0.0
    for d in range(2):
        rev = d == 1
        Sc, ycd = _rwkv_scan(S0, rc, wc[d], kdc[d], vc, kkc, ac[d], rev, need_ctx)
        _, yld = _rwkv_scan(Sc, rl, wl[d], kdl[d], vl, kkl, al[d], rev, True)
        yl = yl + yld
        if need_ctx:
            yc = yc + ycd
    ol = _rwkv_readout(yl, rl, kl, vl, glow_l, r_k, g2, lnx_g, lnx_b)
    oc = _rwkv_readout(yc, rc, kc, vc, glow_c, r_k, g2, lnx_g, lnx_b) if need_ctx else None
    return oc, ol


def _chunk_mlp(uv, ln_g, ln_b, ws, bs):
    B, L, _ = uv.shape
    u, v = jnp.split(jax.nn.gelu(uv), 2, axis=-1)
    v = _layernorm(v, ln_g, ln_b, EPS)
    vb = v.reshape(B, L // CHUNK, CHUNK, CMLP_GROUPS, CMLP_GROUP_CH)
    s = jnp.einsum('gpq,bnqgc->bnpgc', ws, vb) + bs.T[:, :, None]
    return u * s.reshape(B, L, CMLP_WIDTH)


def _merge(oa, ob, oc, gates, w_branch, w_out):
    B, L, _ = gates.shape
    g = jax.nn.sigmoid(gates.reshape(B, L, N_BRANCH, D_MODEL))
    y = (g[:, :, 0] * (oa @ w_branch[0]) + g[:, :, 1] * (ob @ w_branch[1])
         + g[:, :, 2] * (oc @ w_branch[2]))
    return y @ w_out


def _token_mixer(hc, hl, rope, w_in, qk_gain, conv_w, w0, w2, a0, a2, key_k, r_k, g2,
                 lnx_g, lnx_b, ln_g, ln_b, ws, bs, w_branch, w_out, need_ctx):
    kl, vl, rkvl, wlow_l, alow_l, ql, glow_l, uvl, gates_l = jnp.split(hl @ w_in, IN_OFFSETS, axis=-1)
    if need_ctx:
        kc, vc, rkvc, wlow_c, alow_c, qc, glow_c, uvc, gates_c = jnp.split(hc @ w_in, IN_OFFSETS, axis=-1)
    else:
        kc, vc, rkvc, wlow_c, alow_c = jnp.split(hc @ w_in[:, :CTX_STATE_WIDTH], IN_OFFSETS[:4], axis=-1)
        qc, glow_c = None, None
    oa_c, oa_l = _attention_branch(qc, kc, vc, ql, kl, vl, qk_gain, rope)
    prep_l = _rwkv_prepare(rkvl, wlow_l, alow_l, conv_w, w0, w2, a0, a2, key_k)
    prep_c = _rwkv_prepare(rkvc, wlow_c, alow_c, conv_w, w0, w2, a0, a2, key_k)
    ob_c, ob_l = _rwkv_branch(prep_c, prep_l, glow_c, glow_l, r_k, g2, lnx_g, lnx_b, need_ctx)
    ol = _merge(oa_l, ob_l, _chunk_mlp(uvl, ln_g, ln_b, ws, bs), gates_l, w_branch, w_out)
    oc = None
    if need_ctx:
        oc = _merge(oa_c, ob_c, _chunk_mlp(uvc, ln_g, ln_b, ws, bs), gates_c, w_branch, w_out)
    return oc, ol


def _swiglu(h, w_gu, w_down):
    g, u = jnp.split(h @ w_gu, 2, axis=-1)
    return (jax.nn.silu(g) * u) @ w_down


def _moe_swiglu(h, w_router, w_gu, w_down):
    shp = h.shape
    tb = h.reshape(-1, MOE_BLOCK, shp[-1])

    def block(t):
        logits = (t @ w_router).astype(jnp.float32)
        top_v, top_i = lax.top_k(logits, TOP_K)
        wts = jax.nn.softmax(top_v, axis=-1)
        comb = jnp.sum(jax.nn.one_hot(top_i, N_EXPERTS, dtype=jnp.float32) * wts[..., None], axis=1)
        g, u = jnp.split(jnp.einsum('nd,edf->enf', t, w_gu), 2, axis=-1)
        y = jnp.einsum('enf,efd->end', jax.nn.silu(g) * u, w_down)
        return jnp.einsum('ne,end->nd', comb.astype(y.dtype), y)

    return lax.map(block, tb).reshape(shp)


def setup_inputs(seed: int = 0) -> dict:
    key = jax.random.key(seed)
    ks = iter(jax.random.split(key, 40))

    def nrm(shape, scale):
        return scale * jax.random.normal(next(ks), shape, jnp.float32)

    L = DEPTH
    nd = (DEPTH + 1) // 2
    nm = DEPTH // 2
    D = D_MODEL
    return {
        'x': nrm((BATCH, SEQ, D), 1.0),
        'c': nrm((BATCH, D), 1.0),
        'ctx': nrm((BATCH, CTX_LEN, D), 1.0),
        'c_ctx': nrm((D,), 1.0),
        'w_mod': nrm((L, D, 6 * D), 0.5 * D ** -0.5),
        'b_mod': nrm((L, 6 * D), 0.01),
        'norm_gain': 1.0 + nrm((L, 4, D), 0.05),
        'w_in': nrm((L, D, IN_WIDTH), D ** -0.5),
        'qk_gain': 1.0 + nrm((L, 2, HEAD_DIM), 0.05),
        'rwkv_conv': nrm((L, SHIFT_CONV, 3 * RWKV_WIDTH), SHIFT_CONV ** -0.5),
        'decay_w0': jax.random.uniform(next(ks), (L, 2, RWKV_WIDTH), jnp.float32, minval=-6.0, maxval=1.0),
        'decay_w2': nrm((L, 2, DECAY_LORA, RWKV_WIDTH), 0.5 * DECAY_LORA ** -0.5),
        'iclr_a0': nrm((L, 2, RWKV_WIDTH), 0.1),
        'iclr_a2': nrm((L, 2, ICLR_LORA, RWKV_WIDTH), 0.5 * ICLR_LORA ** -0.5),
        'key_k': jnp.array([0.85, 1.0], jnp.float32)[None, :, None] + nrm((L, 2, RWKV_WIDTH), 0.05),
        'bonus_rk': nrm((L, RWKV_HEADS, RWKV_HEAD_DIM), 0.1),
        'gate_g2': nrm((L, GATE_LORA, RWKV_WIDTH), GATE_LORA ** -0.5),
        'lnx_gain': 1.0 + nrm((L, RWKV_WIDTH), 0.05),
        'lnx_bias': nrm((L, RWKV_WIDTH), 0.01),
        'cmlp_ln_gain': 1.0 + nrm((L, CMLP_WIDTH), 0.05),
        'cmlp_ln_bias': nrm((L, CMLP_WIDTH), 0.01),
        'cmlp_ws': nrm((L, CMLP_GROUPS, CHUNK, CHUNK), CHUNK ** -0.5),
        'cmlp_bs': 1.0 + nrm((L, CMLP_GROUPS, CHUNK), 0.05),
        'w_branch': nrm((L, N_BRANCH, RWKV_WIDTH, D), RWKV_WIDTH ** -0.5),
        'w_out': nrm((L, D, D), D ** -0.5),
        'ffn_w_gu': nrm((nd, D, 2 * FFN_DENSE), D ** -0.5),
        'ffn_w_down': nrm((nd, FFN_DENSE, D), FFN_DENSE ** -0.5),
        'moe_router': nrm((nm, D, N_EXPERTS), D ** -0.5),
        'moe_w_gu': nrm((nm, N_EXPERTS, D, 2 * FFN_EXPERT), D ** -0.5),
        'moe_w_down': nrm((nm, N_EXPERTS, FFN_EXPERT, D), FFN_EXPERT ** -0.5),
    }


def reference(x, c, ctx, c_ctx, w_mod, b_mod, norm_gain, w_in, qk_gain, rwkv_conv, decay_w0,
              decay_w2, iclr_a0, iclr_a2, key_k, bonus_rk, gate_g2, lnx_gain, lnx_bias,
              cmlp_ln_gain, cmlp_ln_bias, cmlp_ws, cmlp_bs, w_branch, w_out, ffn_w_gu,
              ffn_w_down, moe_router, moe_w_gu, moe_w_down):
    xl, xc = x, ctx
    rope = _axial_rope_tables(xl.shape[1], xl.dtype)
    for l in range(DEPTH):
        last = l == DEPTH - 1
        sh1, sc1, ga1, sh2, sc2, ga2 = _modulation(c, w_mod[l], b_mod[l])
        csh1, csc1, cga1, csh2, csc2, cga2 = _modulation(c_ctx, w_mod[l], b_mod[l])
        ng = norm_gain[l]
        hl = _rms(xl, ng[0]) * (1 + sc1) + sh1
        hc = _rms(xc, ng[0]) * (1 + csc1) + csh1
        oc, ol = _token_mixer(hc, hl, rope, w_in[l], qk_gain[l], rwkv_conv[l], decay_w0[l],
                              decay_w2[l], iclr_a0[l], iclr_a2[l], key_k[l], bonus_rk[l],
                              gate_g2[l], lnx_gain[l], lnx_bias[l], cmlp_ln_gain[l],
                              cmlp_ln_bias[l], cmlp_ws[l], cmlp_bs[l], w_branch[l], w_out[l],
                              not last)
        xl = xl + ga1 * _rms(ol, ng[1])
        if not last:
            xc = xc + cga1 * _rms(oc, ng[1])
        if l % 2 == 0:
            def ffn(h, i=l // 2):
                return _swiglu(h, ffn_w_gu[i], ffn_w_down[i])
        else:
            def ffn(h, i=l // 2):
                return _moe_swiglu(h, moe_router[i], moe_w_gu[i], moe_w_down[i])
        hl = _rms(xl, ng[2]) * (1 + sc2) + sh2
        xl = xl + ga2 * _rms(ffn(hl), ng[3])
        if not last:
            hc = _rms(xc, ng[2]) * (1 + csc2) + csh2
            xc = xc + cga2 * _rms(ffn(hc), ng[3])
    return xl
```

```python
import functools

import jax
import jax.numpy as jnp
import numpy as np
from jax import lax
from jax.experimental import pallas as pl
from jax.experimental.pallas import tpu as pltpu

F32 = jnp.float32
BF16 = jnp.bfloat16
HIGHEST = lax.Precision.HIGHEST

EPS = 1e-6
LNX_EPS = 64e-5
HEAD_DIM = 64
ROPE_BASE = 10000.0
GRID_W = 64
LANES = 128
MOD_ROWS = 256
SCAN_CHUNK = 64
CMLP_CHUNK = 128
N_EXPERTS = 8
VMEM_LIMIT = 56 * 1024 * 1024

COL_RKV, COL_Q, COL_UV, COL_GATES = 0, 1536, 2048, 3072
COL_K, COL_V, COL_WLOW, COL_ALOW, COL_GLOW = 6144, 6272, 6400, 6528, 6656
IN_PAD = 7168
Q_HEAD_ORDER = (0, 4, 1, 5, 2, 6, 3, 7)


def _cparams(*sem):
    return pltpu.CompilerParams(dimension_semantics=sem, vmem_limit_bytes=VMEM_LIMIT)


def _dot(a, b):
    return jnp.dot(a, b, preferred_element_type=F32)


def _dot_nt(a, b):
    return lax.dot_general(a, b, (((1,), (1,)), ((), ())), preferred_element_type=F32)


def _bf(x):
    return x.astype(BF16)


def _dot_split(a, b_exact, terms):
    acc = None
    rem = a
    for _ in range(terms):
        piece = _bf(rem)
        rem = rem - piece.astype(F32)
        part = _dot(piece, b_exact)
        acc = part if acc is None else acc + part
    return acc


def _dot_split_left(a_exact, b, terms):
    acc = None
    rem = b
    for _ in range(terms):
        piece = _bf(rem)
        rem = rem - piece.astype(F32)
        part = _dot(a_exact, piece)
        acc = part if acc is None else acc + part
    return acc


def _group_matrix(scale):
    r = lax.broadcasted_iota(jnp.int32, (LANES, LANES), 0) // HEAD_DIM
    c = lax.broadcasted_iota(jnp.int32, (LANES, LANES), 1) // HEAD_DIM
    return jnp.where(r == c, scale, 0.0).astype(BF16)


def _pick_tile(n, candidates):
    for t in candidates:
        if n % t == 0:
            return t
    raise ValueError(f"no tile in {candidates} divides {n}")


def _mod_rows(mod_ref, nsub, idx, d):
    parts = [jnp.broadcast_to(mod_ref[s][:, idx * d:(idx + 1) * d], (MOD_ROWS, d)) for s in range(nsub)]
    return parts[0] if nsub == 1 else jnp.concatenate(parts, axis=0)


def _rms(x, g):
    return x * lax.rsqrt(jnp.mean(x * x, axis=-1, keepdims=True) + EPS) * g


def _mod_kernel(c_ref, w_ref, b_ref, o_ref):
    cv = c_ref[...]
    s = cv * jax.nn.sigmoid(cv)
    o_ref[0] = jnp.dot(s, w_ref[0], precision=HIGHEST, preferred_element_type=F32) + b_ref[0]


def _modulation(cvec, w_mod, b_mod):
    nl, d, d6 = w_mod.shape
    rows = cvec.shape[0]
    tn = 1024
    return pl.pallas_call(
        _mod_kernel,
        grid=(nl, d6 // tn),
        in_specs=[pl.BlockSpec((rows, d), lambda l, j: (0, 0)),
                  pl.BlockSpec((1, d, tn), lambda l, j: (l, 0, j)),
                  pl.BlockSpec((1, 1, tn), lambda l, j: (l, 0, j))],
        out_specs=pl.BlockSpec((1, rows, tn), lambda l, j: (l, 0, j)),
        out_shape=jax.ShapeDtypeStruct((nl, rows, d6), F32),
        compiler_params=_cparams("parallel", "parallel"),
        name="modulation",
    )(cvec, w_mod, b_mod.reshape(nl, 1, d6))


def _nmm_kernel(x_ref, g_ref, mod_ref, w_ref, o_ref, h_ref, *, nsub, d, shift_idx, scale_idx):
    @pl.when(pl.program_id(1) == 0)
    def _():
        y = _rms(x_ref[...], g_ref[...])
        sc = _mod_rows(mod_ref, nsub, scale_idx, d)
        sh = _mod_rows(mod_ref, nsub, shift_idx, d)
        h_ref[...] = _bf(y * (1.0 + sc) + sh)

    o_ref[...] = _bf(_dot(h_ref[...], w_ref[...]))


def _norm_mod_matmul(x, gain, modblk, w, shift_idx, scale_idx):
    n, d = x.shape
    nout = w.shape[1]
    tm = _pick_tile(n, (1024, 512, 256))
    tn = 1024
    nsub = tm // MOD_ROWS
    kern = functools.partial(_nmm_kernel, nsub=nsub, d=d, shift_idx=shift_idx, scale_idx=scale_idx)
    return pl.pallas_call(
        kern,
        grid=(n // tm, nout // tn),
        in_specs=[pl.BlockSpec((tm, d), lambda i, j: (i, 0)),
                  pl.BlockSpec((1, d), lambda i, j: (0, 0)),
                  pl.BlockSpec((nsub, 1, 6 * d), lambda i, j: (i, 0, 0)),
                  pl.BlockSpec((d, tn), lambda i, j: (0, j))],
        out_specs=pl.BlockSpec((tm, tn), lambda i, j: (i, j)),
        out_shape=jax.ShapeDtypeStruct((n, nout), BF16),
        scratch_shapes=[pltpu.VMEM((tm, d), BF16)],
        compiler_params=_cparams("parallel", "arbitrary"),
        name="norm_mod_in_proj",
    )(x, gain, modblk, w)


def _qkprep_kernel(q_ref, k_ref, v_ref, qg_ref, kg_ref, cos_ref, sin_ref, qo_ref, ko_ref, vo_ref):
    cos = cos_ref[...]
    sin = sin_ref[...]
    avg = _group_matrix(1.0 / HEAD_DIM)
    lane = lax.broadcasted_iota(jnp.int32, cos.shape, 1)
    first = (lane % 32) < 16
    left = lane < HEAD_DIM

    def norm_rope(x, g):
        ms = _dot_split(x * x, avg, 2)
        xn = x * lax.rsqrt(ms + EPS) * g
        partner = jnp.where(first, pltpu.roll(xn, LANES - 16, 1), pltpu.roll(xn, 16, 1))
        return xn * cos + partner * sin

    for j in range(q_ref.shape[1] // LANES):
        q = q_ref[:, j * LANES:(j + 1) * LANES].astype(F32)
        qo_ref[:, j * LANES:(j + 1) * LANES] = _bf(norm_rope(q, qg_ref[...]) * (HEAD_DIM ** -0.5))
    k = norm_rope(k_ref[...].astype(F32), kg_ref[...])
    zero = jnp.zeros_like(k)
    ko_ref[:, :LANES] = _bf(jnp.where(left, k, zero))
    ko_ref[:, LANES:] = _bf(jnp.where(left, zero, k))
    v = v_ref[...].astype(F32)
    vo_ref[:, :LANES] = _bf(jnp.where(left, v, zero))
    vo_ref[:, LANES:] = _bf(jnp.where(left, zero, v))


def _qk_prep(big, qg, kg, cos, sin, s_tot):
    n = big.shape[0]
    tm = MOD_ROWS
    npos = s_tot // tm
    qw = 512
    return pl.pallas_call(
        _qkprep_kernel,
        grid=(n // tm,),
        in_specs=[pl.BlockSpec((tm, qw), lambda i: (i, COL_Q // qw)),
                  pl.BlockSpec((tm, LANES), lambda i: (i, COL_K // LANES)),
                  pl.BlockSpec((tm, LANES), lambda i: (i, COL_V // LANES)),
                  pl.BlockSpec((1, LANES), lambda i: (0, 0)),
                  pl.BlockSpec((1, LANES), lambda i: (0, 0)),
                  pl.BlockSpec((tm, LANES), lambda i: (i % npos, 0)),
                  pl.BlockSpec((tm, LANES), lambda i: (i % npos, 0))],
        out_specs=[pl.BlockSpec((tm, qw), lambda i: (i, 0)),
                   pl.BlockSpec((tm, 2 * LANES), lambda i: (i, 0)),
                   pl.BlockSpec((tm, 2 * LANES), lambda i: (i, 0))],
        out_shape=[jax.ShapeDtypeStruct((n, qw), BF16),
                   jax.ShapeDtypeStruct((n, 2 * LANES), BF16),
                   jax.ShapeDtypeStruct((n, 2 * LANES), BF16)],
        compiler_params=_cparams("parallel"),
        name="qk_norm_rope",
    )(big, big, big, qg, kg, cos, sin)


def _attn_kernel(q_ref, k_ref, v_ref, o_ref, *, tq, tk, n_ctx_q, n_ctx_kv, n_kv):
    i = pl.program_id(1)
    nkv = jnp.where(i < n_ctx_q, n_ctx_kv, n_kv)
    left = lax.broadcasted_iota(jnp.int32, (tq, LANES), 1) < HEAD_DIM
    neg = jnp.full((tq, 1), -1e30, F32)
    zero = jnp.zeros((tq, 1), F32)
    for j in range(q_ref.shape[1] // LANES):
        q = q_ref[:, j * LANES:(j + 1) * LANES]

        def body(c, carry):
            m0, l0, m1, l1, acc = carry
            off = pl.multiple_of(c * tk, tk)
            kb = k_ref[pl.ds(off, tk), :]
            vb = v_ref[pl.ds(off, tk), :]
            s0 = _dot_nt(q, kb[:, :LANES])
            s1 = _dot_nt(q, kb[:, LANES:])
            n0 = jnp.maximum(m0, jnp.max(s0, axis=-1, keepdims=True))
            n1 = jnp.maximum(m1, jnp.max(s1, axis=-1, keepdims=True))
            a0 = jnp.exp(m0 - n0)
            a1 = jnp.exp(m1 - n1)
            p0 = jnp.exp(s0 - n0)
            p1 = jnp.exp(s1 - n1)
            l0 = a0 * l0 + jnp.sum(p0, axis=-1, keepdims=True)
            l1 = a1 * l1 + jnp.sum(p1, axis=-1, keepdims=True)
            pv = _dot(_bf(p0), vb[:, :LANES]) + _dot(_bf(p1), vb[:, LANES:])
            acc = acc * jnp.where(left, a0, a1) + pv
            return n0, l0, n1, l1, acc

        m0, l0, m1, l1, acc = lax.fori_loop(
            0, nkv, body, (neg, zero, neg, zero, jnp.zeros((tq, LANES), F32)))
        o_ref[:, j * LANES:(j + 1) * LANES] = _bf(acc * jnp.where(left, 1.0 / l0, 1.0 / l1))


def _attention(qh, kbd, vbd, batch, s_tot, c_len):
    n, qw = qh.shape
    tq = 256
    tk = 256
    nq = s_tot // tq
    kern = functools.partial(_attn_kernel, tq=tq, tk=tk, n_ctx_q=c_len // tq,
                             n_ctx_kv=c_len // tk, n_kv=s_tot // tk)
    return pl.pallas_call(
        kern,
        grid=(batch, nq),
        in_specs=[pl.BlockSpec((tq, qw), lambda b, i: (b * nq + i, 0)),
                  pl.BlockSpec((s_tot, 2 * LANES), lambda b, i: (b, 0)),
                  pl.BlockSpec((s_tot, 2 * LANES), lambda b, i: (b, 0))],
        out_specs=pl.BlockSpec((tq, qw), lambda b, i: (b * nq + i, 0)),
        out_shape=jax.ShapeDtypeStruct((n, qw), BF16),
        compiler_params=_cparams("parallel", "parallel"),
        name="gqa_attention",
    )(qh, kbd, vbd)


def _cmlp_kernel(uv_ref, lng_ref, lnb_ref, ws_ref, bs_ref, o_ref, *, nchunk, width):
    x = uv_ref[...].astype(F32)
    g = 0.5 * x * (1.0 + jnp.tanh(0.7978845608028654 * (x + 0.044715 * (x * x * x))))
    u = g[:, :width]
    v = g[:, width:]
    mu = jnp.mean(v, axis=-1, keepdims=True)
    dv = v - mu
    var = jnp.mean(dv * dv, axis=-1, keepdims=True)
    vn = _bf(dv * lax.rsqrt(var + EPS) * lng_ref[...] + lnb_ref[...])
    ngroups = width // CMLP_CHUNK
    for c in range(nchunk):
        r0 = c * CMLP_CHUNK
        for gi in range(ngroups):
            c0 = gi * CMLP_CHUNK
            s = _dot(ws_ref[gi], vn[r0:r0 + CMLP_CHUNK, c0:c0 + CMLP_CHUNK]) + bs_ref[gi]
            o_ref[r0:r0 + CMLP_CHUNK, c0:c0 + CMLP_CHUNK] = _bf(u[r0:r0 + CMLP_CHUNK, c0:c0 + CMLP_CHUNK] * s)


def _chunk_mlp(big, ln_g, ln_b, ws, bs_b):
    n = big.shape[0]
    width = ln_g.shape[1]
    tr = _pick_tile(n, (512, 256, 128))
    kern = functools.partial(_cmlp_kernel, nchunk=tr // CMLP_CHUNK, width=width)
    ng = ws.shape[0]
    return pl.pallas_call(
        kern,
        grid=(n // tr,),
        in_specs=[pl.BlockSpec((tr, 2 * width), lambda i: (i, COL_UV // (2 * width))),
                  pl.BlockSpec((1, width), lambda i: (0, 0)),
                  pl.BlockSpec((1, width), lambda i: (0, 0)),
                  pl.BlockSpec((ng, CMLP_CHUNK, CMLP_CHUNK), lambda i: (0, 0, 0)),
                  pl.BlockSpec((ng, CMLP_CHUNK, CMLP_CHUNK), lambda i: (0, 0, 0))],
        out_specs=pl.BlockSpec((tr, width), lambda i: (i, 0)),
        out_shape=jax.ShapeDtypeStruct((n, width), BF16),
        compiler_params=_cparams("parallel"),
        name="chunk_gmlp",
    )(big, ln_g, ln_b, ws, bs_b)


def _rwkv_prep_kernel(x_ref, xp_ref, xn_ref, lo_ref, conv_ref, w0_ref, w2_ref, a0_ref, a2_ref,
                      kk0_ref, kk1_ref, rk_ref,
                      v_o, bonus_o, at_f, rt_f, bt_f, kt_f, bb_f, kb_f, pl_f,
                      at_b, rt_b, bt_b, kt_b, bb_b, kb_b, pl_b, *, tm, width, blocks_per_seq, ctx_blocks):
    i = pl.program_id(0)
    j = i % blocks_per_seq
    is_first = jnp.logical_or(j == 0, j == ctx_blocks)
    is_last = jnp.logical_or(j == ctx_blocks - 1, j == blocks_per_seq - 1)
    row = lax.broadcasted_iota(jnp.int32, (tm, width), 0)
    gsum = _group_matrix(1.0)
    halo = xp_ref.shape[0]

    def conv(c):
        cs = slice(c * width, (c + 1) * width)
        x = x_ref[:, cs].astype(F32)
        prev_row = jnp.where(is_first, 0.0, xp_ref[halo - 1:halo, cs].astype(F32))
        next_row = jnp.where(is_last, 0.0, xn_ref[0:1, cs].astype(F32))
        xprev = jnp.where(row == 0, prev_row, pltpu.roll(x, 1, 0))
        xnext = jnp.where(row == tm - 1, next_row, pltpu.roll(x, tm - 1, 0))
        return xprev * conv_ref[0:1, cs] + x * conv_ref[1:2, cs] + xnext * conv_ref[2:3, cs]

    r = conv(0)
    k = conv(1)
    v = conv(2)
    v_o[...] = _bf(v)

    def group_sum(x):
        parts = [_dot_split(x[:, c * LANES:(c + 1) * LANES], gsum, 2) for c in range(width // LANES)]
        return jnp.concatenate(parts, axis=1)

    kk = k * kk0_ref[...]
    kk = kk * lax.rsqrt(group_sum(kk * kk) + 1e-12)
    bonus_o[...] = _bf(group_sum(r * k * rk_ref[...]) * v)

    lo = lo_ref[...].astype(F32)
    wd = w0_ref[...] + _dot(_bf(jnp.tanh(lo[:, :LANES])), w2_ref[...])
    ad = jax.nn.sigmoid(a0_ref[...] + _dot(_bf(lo[:, LANES:]), a2_ref[...]))
    lw = -float(np.exp(-0.5)) * jax.nn.sigmoid(wd)

    r2 = lax.broadcasted_iota(jnp.int32, (tm, tm), 0)
    c2 = lax.broadcasted_iota(jnp.int32, (tm, tm), 1)
    same = (r2 // SCAN_CHUNK) == (c2 // SCAN_CHUNK)
    tri_pre = jnp.where(jnp.logical_and(same, c2 <= r2), 1.0, 0.0).astype(BF16)
    tri_suf = jnp.where(jnp.logical_and(same, c2 >= r2), 1.0, 0.0).astype(BF16)
    nchunk = tm // SCAN_CHUNK

    outs = ((at_f, rt_f, bt_f, kt_f, bb_f, kb_f, pl_f), (at_b, rt_b, bt_b, kt_b, bb_b, kb_b, pl_b))
    for d in range(2):
        ds_ = slice(d * width, (d + 1) * width)
        lwd = lw[:, ds_]
        pre = _dot_split_left(tri_pre, lwd, 3)
        suf = _dot_split_left(tri_suf, lwd, 3)
        cin, rem = (pre, suf - lwd) if d == 0 else (suf, pre - lwd)
        cex = cin - lwd
        a_d = ad[:, ds_]
        b = kk * a_d
        kd = k * (1.0 + (a_d - 1.0) * kk1_ref[...])
        at_o, rt_o, bt_o, kt_o, bb_o, kb_o, pl_o = outs[d]
        at_o[...] = _bf(-kk * jnp.exp(cex))
        rt_o[...] = _bf(r * jnp.exp(cin))
        pinv = jnp.exp(-cin)
        bt_o[...] = _bf(b * pinv)
        kt_o[...] = _bf(kd * pinv)
        pend = jnp.exp(rem)
        bb_o[...] = _bf(b * pend)
        kb_o[...] = _bf(kd * pend)
        for c in range(nchunk):
            last = (c + 1) * SCAN_CHUNK - 1
            pl_o[c] = jnp.exp(pre[last:last + 1, :])


def _rwkv_prep(big, conv_w, w0, w2s, a0, a2s, kk0, kk1, rk, s_tot, c_len):
    n = big.shape[0]
    width = rk.shape[1]
    tm = MOD_ROWS
    halo = 16
    hb = tm // halo
    nhalo = n // halo
    nchunk = tm // SCAN_CHUNK
    kern = functools.partial(_rwkv_prep_kernel, tm=tm, width=width, blocks_per_seq=s_tot // tm,
                             ctx_blocks=c_len // tm)
    tok = pl.BlockSpec((tm, width), lambda i: (i, 0))
    plspec = pl.BlockSpec((nchunk, 1, width), lambda i: (i, 0, 0))
    tok_shape = jax.ShapeDtypeStruct((n, width), BF16)
    pl_shape = jax.ShapeDtypeStruct((n // SCAN_CHUNK, 1, width), F32)
    full = lambda a: pl.BlockSpec(a.shape, lambda i: (0,) * a.ndim)
    return pl.pallas_call(
        kern,
        grid=(n // tm,),
        in_specs=[pl.BlockSpec((tm, 3 * width), lambda i: (i, 0)),
                  pl.BlockSpec((halo, 3 * width), lambda i: (jnp.maximum(i * hb - 1, 0), 0)),
                  pl.BlockSpec((halo, 3 * width), lambda i: (jnp.minimum((i + 1) * hb, nhalo - 1), 0)),
                  pl.BlockSpec((tm, 2 * LANES), lambda i: (i, COL_WLOW // (2 * LANES))),
                  full(conv_w), full(w0), full(w2s), full(a0), full(a2s), full(kk0), full(kk1), full(rk)],
        out_specs=[tok, tok] + [tok] * 6 + [plspec] + [tok] * 6 + [plspec],
        out_shape=[tok_shape, tok_shape] + [tok_shape] * 6 + [pl_shape] + [tok_shape] * 6 + [pl_shape],
        compiler_params=_cparams("parallel"),
        name="rwkv_prepare",
    )(big, big, big, big, conv_w, w0, w2s, a0, a2s, kk0, kk1, rk)


def _scan_chunk_pair(at, rt, bt, kt, bb, kb, v, plast, z, forward):
    L = SCAN_CHUNK
    lane = lax.broadcasted_iota(jnp.int32, (L, LANES), 1)
    m0 = _bf(jnp.where(lane < HEAD_DIM, 1.0, 0.0))
    m1 = _bf(jnp.where(lane < HEAD_DIM, 0.0, 1.0))

    def stack(x):
        return jnp.concatenate([x * m0, x * m1], axis=0)

    r2 = lax.broadcasted_iota(jnp.int32, (2 * L, 2 * L), 0)
    c2 = lax.broadcasted_iota(jnp.int32, (2 * L, 2 * L), 1)
    same = (r2 // L) == (c2 // L)
    tr, tc = r2 % L, c2 % L
    strict = jnp.logical_and(same, tc < tr if forward else tc > tr)
    incl = jnp.logical_and(same, tc <= tr if forward else tc >= tr)

    a_s, r_s, b_s, k_s, v_s = stack(at), stack(rt), stack(bt), stack(kt), stack(v)
    big1 = _dot_nt(jnp.concatenate([a_s, r_s], axis=0), jnp.concatenate([b_s, k_s], axis=0))
    mab = jnp.where(strict, big1[:2 * L, :2 * L], 0.0)
    mak = jnp.where(strict, big1[:2 * L, 2 * L:], 0.0)
    nrb = jnp.where(incl, big1[2 * L:, :2 * L], 0.0)
    nrk = jnp.where(incl, big1[2 * L:, 2 * L:], 0.0)

    mv = _dot(_bf(mak), v_s)
    x = jnp.concatenate([a_s.astype(F32), mv], axis=1)
    mp = mab
    steps = int(np.log2(L))
    for it in range(steps):
        if it < steps - 1:
            res = _dot(_bf(mp), _bf(jnp.concatenate([mp, x], axis=1)))
            x = x + res[:, 2 * L:]
            mp = res[:, :2 * L]
        else:
            x = x + _dot(_bf(mp), _bf(x))
    xb = _bf(x)
    rhs2 = jnp.concatenate([xb, jnp.concatenate([jnp.zeros_like(v_s), v_s], axis=1)], axis=0)
    lhs_top = _bf(jnp.concatenate([nrb, nrk], axis=1))
    lhs_bot = _bf(jnp.concatenate([stack(bb), stack(kb)], axis=0).astype(F32).T)
    res2 = _dot(jnp.concatenate([lhs_top, lhs_bot], axis=0), rhs2)
    rh = r_s.astype(F32) + res2[:2 * L, :LANES]
    y0 = res2[:2 * L, LANES:]
    rk_ = lax.broadcasted_iota(jnp.int32, (LANES, LANES), 0)
    ck_ = lax.broadcasted_iota(jnp.int32, (LANES, LANES), 1)
    g = res2[2 * L:, :LANES] + jnp.where(rk_ == ck_, jnp.broadcast_to(plast, (LANES, LANES)), 0.0)
    h = res2[2 * L:, LANES:]
    res3 = _dot(_bf(jnp.concatenate([rh, g], axis=0)), _bf(z))
    ys = res3[:2 * L] + y0
    return ys[:L] + ys[L:], res3[2 * L:] + h


def _rwkv_scan_kernel(v_f, at_f, rt_f, bt_f, kt_f, bb_f, kb_f, pl_f,
                      v_b, at_b, rt_b, bt_b, kt_b, bb_b, kb_b, pl_b,
                      yf_ref, yb_ref, z_ref, *, npairs):
    @pl.when(pl.program_id(1) == 0)
    def _():
        z_ref[...] = jnp.zeros_like(z_ref)

    dirs = ((v_f, at_f, rt_f, bt_f, kt_f, bb_f, kb_f, pl_f, yf_ref, True),
            (v_b, at_b, rt_b, bt_b, kt_b, bb_b, kb_b, pl_b, yb_ref, False))
    for d, (v, at, rt, bt, kt, bb, kb, plr, y_ref, fwd) in enumerate(dirs):
        for p in range(npairs):
            cs = slice(p * LANES, (p + 1) * LANES)
            y, znew = _scan_chunk_pair(at[:, cs], rt[:, cs], bt[:, cs], kt[:, cs], bb[:, cs], kb[:, cs],
                                       v[:, cs], plr[0][:, cs], z_ref[d, p], fwd)
            y_ref[:, cs] = y
            z_ref[d, p] = znew


def _rwkv_scan(prep, batch, s_tot, c_len):
    (v, _bonus, at_f, rt_f, bt_f, kt_f, bb_f, kb_f, pl_f, at_b, rt_b, bt_b, kt_b, bb_b, kb_b, pl_b) = prep
    n, width = v.shape
    L = SCAN_CHUNK
    nch = s_tot // L
    ncc = c_len // L
    npairs = width // LANES

    def fmap(b, c):
        return (b * nch + c, 0)

    def bmap(b, c):
        return (b * nch + jnp.where(c < ncc, ncc - 1 - c, nch - 1 - (c - ncc)), 0)

    def tok(m):
        return pl.BlockSpec((L, width), m)

    def pls(m):
        return pl.BlockSpec((1, 1, width), lambda b, c: m(b, c) + (0,))

    kern = functools.partial(_rwkv_scan_kernel, npairs=npairs)
    return pl.pallas_call(
        kern,
        grid=(batch, nch),
        in_specs=[tok(fmap)] * 7 + [pls(fmap)] + [tok(bmap)] * 7 + [pls(bmap)],
        out_specs=[tok(fmap), tok(bmap)],
        out_shape=[jax.ShapeDtypeStruct((n, width), F32)] * 2,
        scratch_shapes=[pltpu.VMEM((2, npairs, LANES, LANES), F32)],
        compiler_params=_cparams("parallel", "arbitrary"),
        name="rwkv_scan",
    )(v, at_f, rt_f, bt_f, kt_f, bb_f, kb_f, pl_f, v, at_b, rt_b, bt_b, kt_b, bb_b, kb_b, pl_b)


def _rwkv_readout_kernel(yf_ref, yb_ref, bonus_ref, gl_ref, g2_ref, lg_ref, lb_ref, o_ref):
    avg = _group_matrix(1.0 / HEAD_DIM)
    gate = _dot(_bf(jax.nn.sigmoid(gl_ref[...].astype(F32))), g2_ref[...])
    for c in range(o_ref.shape[1] // LANES):
        cs = slice(c * LANES, (c + 1) * LANES)
        y = yf_ref[:, cs] + yb_ref[:, cs]
        mu = _dot_split(y, avg, 2)
        dy = y - mu
        var = _dot_split(dy * dy, avg, 2)
        yn = dy * lax.rsqrt(var + LNX_EPS) * lg_ref[:, cs] + lb_ref[:, cs]
        o_ref[:, cs] = _bf((yn + bonus_ref[:, cs].astype(F32)) * gate[:, cs])


def _rwkv_readout(yf, yb, bonus, big, g2, lnx_g, lnx_b):
    n, width = yf.shape
    tm = _pick_tile(n, (512, 256))
    tok = pl.BlockSpec((tm, width), lambda i: (i, 0))
    full = lambda a: pl.BlockSpec(a.shape, lambda i: (0,) * a.ndim)
    return pl.pallas_call(
        _rwkv_readout_kernel,
        grid=(n // tm,),
        in_specs=[tok, tok, tok, pl.BlockSpec((tm, LANES), lambda i: (i, COL_GLOW // LANES)),
                  full(g2), full(lnx_g), full(lnx_b)],
        out_specs=tok,
        out_shape=jax.ShapeDtypeStruct((n, width), BF16),
        compiler_params=_cparams("parallel"),
        name="rwkv_readout",
    )(yf, yb, bonus, big, g2, lnx_g, lnx_b)


def _merge_kernel(oa_ref, ob_ref, oc_ref, gt_ref, x_ref, wb_ref, wo_ref, ng_ref, mod_ref, o_ref, *, nsub, d):
    y = None
    for br, ref in enumerate((oa_ref, ob_ref, oc_ref)):
        g = jax.nn.sigmoid(gt_ref[:, br * d:(br + 1) * d].astype(F32))
        t = g * _dot(ref[...], wb_ref[br])
        y = t if y is None else y + t
    o = _dot(_bf(y), wo_ref[...])
    o_ref[...] = x_ref[...] + _mod_rows(mod_ref, nsub, 2, d) * _rms(o, ng_ref[...])


def _merge(oa, ob, oc, big, x, wb, wo, ng, modblk):
    n, d = x.shape
    width = oa.shape[1]
    tm = _pick_tile(n, (512, 256))
    nsub = tm // MOD_ROWS
    kern = functools.partial(_merge_kernel, nsub=nsub, d=d)
    br = pl.BlockSpec((tm, width), lambda i: (i, 0))
    return pl.pallas_call(
        kern,
        grid=(n // tm,),
        in_specs=[br, br, br,
                  pl.BlockSpec((tm, 3 * d), lambda i: (i, COL_GATES // (3 * d))),
                  pl.BlockSpec((tm, d), lambda i: (i, 0)),
                  pl.BlockSpec(wb.shape, lambda i: (0, 0, 0)),
                  pl.BlockSpec(wo.shape, lambda i: (0, 0)),
                  pl.BlockSpec((1, d), lambda i: (0, 0)),
                  pl.BlockSpec((nsub, 1, 6 * d), lambda i: (i, 0, 0))],
        out_specs=pl.BlockSpec((tm, d), lambda i: (i, 0)),
        out_shape=jax.ShapeDtypeStruct((n, d), F32),
        compiler_params=_cparams("parallel"),
        name="merge_out_proj",
    )(oa, ob, oc, big, x, wb, wo, ng, modblk)


def _ffn_kernel(x_ref, g_ref, mod_ref, wg_ref, wu_ref, wd_ref, ng_ref, o_ref, h_ref, acc_ref, *, nsub, d):
    j = pl.program_id(1)

    @pl.when(j == 0)
    def _():
        y = _rms(x_ref[...], g_ref[...])
        h_ref[...] = _bf(y * (1.0 + _mod_rows(mod_ref, nsub, 4, d)) + _mod_rows(mod_ref, nsub, 3, d))
        acc_ref[...] = jnp.zeros_like(acc_ref)

    h = h_ref[...]
    g = _dot(h, wg_ref[...])
    u = _dot(h, wu_ref[...])
    acc_ref[...] += _dot(_bf(g * jax.nn.sigmoid(g) * u), wd_ref[...])

    @pl.when(j == pl.num_programs(1) - 1)
    def _():
        o_ref[...] = x_ref[...] + _mod_rows(mod_ref, nsub, 5, d) * _rms(acc_ref[...], ng_ref[...])


def _dense_ffn(x, gain_in, gain_out, modblk, w_gu, w_down):
    n, d = x.shape
    f = w_down.shape[0]
    tm = _pick_tile(n, (1024, 512, 256))
    tf = _pick_tile(f, (1408, 1024, 512, 256, 128))
    nf = f // tf
    nsub = tm // MOD_ROWS
    kern = functools.partial(_ffn_kernel, nsub=nsub, d=d)
    return pl.pallas_call(
        kern,
        grid=(n // tm, nf),
        in_specs=[pl.BlockSpec((tm, d), lambda i, j: (i, 0)),
                  pl.BlockSpec((1, d), lambda i, j: (0, 0)),
                  pl.BlockSpec((nsub, 1, 6 * d), lambda i, j: (i, 0, 0)),
                  pl.BlockSpec((d, tf), lambda i, j: (0, j)),
                  pl.BlockSpec((d, tf), lambda i, j: (0, j + nf)),
                  pl.BlockSpec((tf, d), lambda i, j: (j, 0)),
                  pl.BlockSpec((1, d), lambda i, j: (0, 0))],
        out_specs=pl.BlockSpec((tm, d), lambda i, j: (i, 0)),
        out_shape=jax.ShapeDtypeStruct((n, d), F32),
        scratch_shapes=[pltpu.VMEM((tm, d), BF16), pltpu.VMEM((tm, d), F32)],
        compiler_params=_cparams("parallel", "arbitrary"),
        name="dense_swiglu_ffn",
    )(x, gain_in, modblk, w_gu, w_gu, w_down, gain_out)


def _router_kernel(x_ref, g_ref, mod_ref, wr_ref, h_ref, comb_ref, *, nsub, d):
    y = _rms(x_ref[...], g_ref[...])
    h = y * (1.0 + _mod_rows(mod_ref, nsub, 4, d)) + _mod_rows(mod_ref, nsub, 3, d)
    h_ref[...] = _bf(h)
    logits = jnp.dot(h, wr_ref[...], precision=HIGHEST, preferred_element_type=F32)
    lane = lax.broadcasted_iota(jnp.int32, logits.shape, 1)
    ninf = jnp.float32(-jnp.inf)
    logits = jnp.where(lane < N_EXPERTS, logits, ninf)
    m1 = jnp.max(logits, axis=-1, keepdims=True)
    i1 = jnp.min(jnp.where(logits == m1, lane, LANES), axis=-1, keepdims=True)
    rest = jnp.where(lane == i1, ninf, logits)
    m2 = jnp.max(rest, axis=-1, keepdims=True)
    i2 = jnp.min(jnp.where(rest == m2, lane, LANES), axis=-1, keepdims=True)
    e2 = jnp.exp(m2 - m1)
    w1 = 1.0 / (1.0 + e2)
    comb_ref[...] = jnp.where(lane == i1, w1, 0.0) + jnp.where(lane == i2, e2 * w1, 0.0)


def _router(x, gain_in, modblk, w_router_pad):
    n, d = x.shape
    tm = _pick_tile(n, (512, 256))
    nsub = tm // MOD_ROWS
    kern = functools.partial(_router_kernel, nsub=nsub, d=d)
    return pl.pallas_call(
        kern,
        grid=(n // tm,),
        in_specs=[pl.BlockSpec((tm, d), lambda i: (i, 0)),
                  pl.BlockSpec((1, d), lambda i: (0, 0)),
                  pl.BlockSpec((nsub, 1, 6 * d), lambda i: (i, 0, 0)),
                  pl.BlockSpec((d, LANES), lambda i: (0, 0))],
        out_specs=[pl.BlockSpec((tm, d), lambda i: (i, 0)), pl.BlockSpec((tm, LANES), lambda i: (i, 0))],
        out_shape=[jax.ShapeDtypeStruct((n, d), BF16), jax.ShapeDtypeStruct((n, LANES), F32)],
        compiler_params=_cparams("parallel"),
        name="moe_router",
    )(x, gain_in, modblk, w_router_pad)


def _moe_kernel(h_ref, comb_ref, x_ref, wg_ref, wu_ref, wd_ref, ng_ref, mod_ref, o_ref, acc_ref, *, nsub, d):
    e = pl.program_id(1)
    j = pl.program_id(2)

    @pl.when(jnp.logical_and(e == 0, j == 0))
    def _():
        acc_ref[...] = jnp.zeros_like(acc_ref)

    h = h_ref[...]
    g = _dot(h, wg_ref[0])
    u = _dot(h, wu_ref[0])
    comb = comb_ref[...]
    lane = lax.broadcasted_iota(jnp.int32, comb.shape, 1)
    ce = jnp.sum(jnp.where(lane == e, comb, 0.0), axis=-1, keepdims=True)
    acc_ref[...] += _dot(_bf(g * jax.nn.sigmoid(g) * u * ce), wd_ref[0])

    @pl.when(jnp.logical_and(e == pl.num_programs(1) - 1, j == pl.num_programs(2) - 1))
    def _():
        o_ref[...] = x_ref[...] + _mod_rows(mod_ref, nsub, 5, d) * _rms(acc_ref[...], ng_ref[...])


def _moe_ffn(x, hb, comb, gain_out, modblk, w_gu, w_down):
    n, d = x.shape
    ne, f, _ = w_down.shape
    tm = _pick_tile(n, (1024, 512, 256))
    tf = _pick_tile(f, (896, 512, 256, 128))
    nf = f // tf
    nsub = tm // MOD_ROWS
    kern = functools.partial(_moe_kernel, nsub=nsub, d=d)
    return pl.pallas_call(
        kern,
        grid=(n // tm, ne, nf),
        in_specs=[pl.BlockSpec((tm, d), lambda i, e, j: (i, 0)),
                  pl.BlockSpec((tm, LANES), lambda i, e, j: (i, 0)),
                  pl.BlockSpec((tm, d), lambda i, e, j: (i, 0)),
                  pl.BlockSpec((1, d, tf), lambda i, e, j: (e, 0, j)),
                  pl.BlockSpec((1, d, tf), lambda i, e, j: (e, 0, j + nf)),
                  pl.BlockSpec((1, tf, d), lambda i, e, j: (e, j, 0)),
                  pl.BlockSpec((1, d), lambda i, e, j: (0, 0)),
                  pl.BlockSpec((nsub, 1, 6 * d), lambda i, e, j: (i, 0, 0))],
        out_specs=pl.BlockSpec((tm, d), lambda i, e, j: (i, 0)),
        out_shape=jax.ShapeDtypeStruct((n, d), F32),
        scratch_shapes=[pltpu.VMEM((tm, d), F32)],
        compiler_params=_cparams("parallel", "arbitrary", "arbitrary"),
        name="moe_swiglu_ffn",
    )(hb, comb, x, w_gu, w_gu, w_down, gain_out, modblk)


def _rope_tables(t_len, c_len):
    pairs = HEAD_DIM // 4
    rows = t_len // GRID_W
    row = jnp.repeat(jnp.arange(rows, dtype=F32), GRID_W)
    col = jnp.tile(jnp.arange(GRID_W, dtype=F32), rows)
    freqs = ROPE_BASE ** (-jnp.arange(pairs, dtype=F32) / pairs)
    ar = row[:, None] * freqs
    ac = col[:, None] * freqs
    cos = jnp.concatenate([jnp.cos(ar), jnp.cos(ar), jnp.cos(ac), jnp.cos(ac)], axis=1)
    sin = jnp.concatenate([-jnp.sin(ar), jnp.sin(ar), -jnp.sin(ac), jnp.sin(ac)], axis=1)
    cos = jnp.concatenate([jnp.ones((c_len, HEAD_DIM), F32), cos], axis=0)
    sin = jnp.concatenate([jnp.zeros((c_len, HEAD_DIM), F32), sin], axis=0)
    return jnp.tile(cos, (1, 2)), jnp.tile(sin, (1, 2))


def _block_diag2(w):
    z = jnp.zeros_like(w[0])
    return jnp.concatenate([jnp.concatenate([w[0], z], axis=1), jnp.concatenate([z, w[1]], axis=1)], axis=0)


def kernel(x, c, ctx, c_ctx, w_mod, b_mod, norm_gain, w_in, qk_gain, rwkv_conv, decay_w0, decay_w2, iclr_a0, iclr_a2, key_k, bonus_rk, gate_g2, lnx_gain, lnx_bias, cmlp_ln_gain, cmlp_ln_bias, cmlp_ws, cmlp_bs, w_branch, w_out, ffn_w_gu, ffn_w_down, moe_router, moe_w_gu, moe_w_down):
    batch, t_len, d = x.shape
    c_len = ctx.shape[1]
    depth = w_mod.shape[0]
    s_tot = c_len + t_len
    n = batch * s_tot
    assert c_len % MOD_ROWS == 0 and t_len % MOD_ROWS == 0 and d % LANES == 0
    width = bonus_rk.shape[1] * bonus_rk.shape[2]
    nheads = width // HEAD_DIM

    xs = jnp.concatenate([ctx, x], axis=1).reshape(n, d)

    mod_rows = 8 * ((batch + 1 + 7) // 8)
    cvec = jnp.zeros((mod_rows, d), F32).at[0].set(c_ctx).at[1:batch + 1].set(c)
    mods = _modulation(cvec, w_mod, b_mod)
    mod_ctx = jnp.broadcast_to(mods[:, 0:1, None, :], (depth, batch, c_len // MOD_ROWS, 6 * d))
    mod_lat = jnp.broadcast_to(mods[:, 1:batch + 1, None, :], (depth, batch, t_len // MOD_ROWS, 6 * d))
    modblk_all = jnp.concatenate([mod_ctx, mod_lat], axis=2).reshape(depth, n // MOD_ROWS, 1, 6 * d)

    order = np.array(Q_HEAD_ORDER)
    nl = depth
    wq = w_in[:, :, 2048:2560].reshape(nl, d, nheads, HEAD_DIM)[:, :, order].reshape(nl, d, width)
    w_in_p = jnp.concatenate([
        w_in[:, :, 256:1792], wq, w_in[:, :, 2688:3712], w_in[:, :, 3712:6784],
        w_in[:, :, 0:128], w_in[:, :, 128:256], w_in[:, :, 1792:1920], w_in[:, :, 1920:2048],
        w_in[:, :, 2560:2688], jnp.zeros((nl, d, IN_PAD - 6784), F32)], axis=2).astype(BF16)
    wb = w_branch.astype(BF16)
    wb0 = wb[:, 0].reshape(nl, nheads, HEAD_DIM, d)[:, order].reshape(nl, width, d)
    wb = jnp.concatenate([wb0[:, None], wb[:, 1:]], axis=1)
    wo = w_out.astype(BF16)
    cos, sin = _rope_tables(t_len, c_len)
    qg = jnp.tile(qk_gain[:, 0], (1, 2))[:, None, :]
    kg = jnp.tile(qk_gain[:, 1], (1, 2))[:, None, :]
    ws_b = cmlp_ws.astype(BF16)
    bs_b = jnp.broadcast_to(cmlp_bs[..., None], cmlp_bs.shape + (CMLP_CHUNK,))
    w2s = jnp.stack([_block_diag2(decay_w2[l]) for l in range(nl)]).astype(BF16)
    a2s = jnp.stack([_block_diag2(iclr_a2[l]) for l in range(nl)]).astype(BF16)
    w0 = decay_w0.reshape(nl, 1, 2 * width)
    a0 = iclr_a0.reshape(nl, 1, 2 * width)
    rk = bonus_rk.reshape(nl, 1, width)
    g2 = gate_g2.astype(BF16)
    ffn_gu = ffn_w_gu.astype(BF16)
    ffn_dn = ffn_w_down.astype(BF16)
    moe_gu = moe_w_gu.astype(BF16)
    moe_dn = moe_w_down.astype(BF16)
    router_pad = jnp.pad(moe_router, ((0, 0), (0, 0), (0, LANES - moe_router.shape[2])))

    for l in range(depth):
        modblk = modblk_all[l]
        ng = norm_gain[l]
        big = _norm_mod_matmul(xs, ng[0:1], modblk, w_in_p[l], 0, 1)
        qh, kbd, vbd = _qk_prep(big, qg[l], kg[l], cos, sin, s_tot)
        oa = _attention(qh, kbd, vbd, batch, s_tot, c_len)
        prep = _rwkv_prep(big, rwkv_conv[l], w0[l], w2s[l], a0[l], a2s[l],
                          key_k[l, 0:1], key_k[l, 1:2], rk[l], s_tot, c_len)
        yf, yb = _rwkv_scan(prep, batch, s_tot, c_len)
        ob = _rwkv_readout(yf, yb, prep[1], big, g2[l], lnx_gain[l][None], lnx_bias[l][None])
        oc = _chunk_mlp(big, cmlp_ln_gain[l][None], cmlp_ln_bias[l][None], ws_b[l], bs_b[l])
        xs = _merge(oa, ob, oc, big, xs, wb[l], wo[l], ng[1:2], modblk)
        if l % 2 == 0:
            xs = _dense_ffn(xs, ng[2:3], ng[3:4], modblk, ffn_gu[l // 2], ffn_dn[l // 2])
        else:
            hb, comb = _router(xs, ng[2:3], modblk, router_pad[l // 2])
            xs = _moe_ffn(xs, hb, comb, ng[3:4], modblk, moe_gu[l // 2], moe_dn[l // 2])
    return xs.reshape(batch, s_tot, d)[:, c_len:, :]
```

```python
import functools

import jax
import jax.numpy as jnp
import numpy as np
from jax import lax
from jax.experimental import pallas as pl
from jax.experimental.pallas import tpu as pltpu

F32 = jnp.float32
BF16 = jnp.bfloat16
HIGHEST = lax.Precision.HIGHEST

EPS = 1e-6
LNX_EPS = 64e-5
HEAD_DIM = 64
ROPE_BASE = 10000.0
GRID_W = 64
LANES = 128
MOD_ROWS = 256
SCAN_CHUNK = 64
CMLP_CHUNK = 128
ATT_TK = 128
N_EXPERTS = 8
VMEM_LIMIT = 56 * 1024 * 1024

COL_RKV, COL_Q, COL_UV, COL_GATES = 0, 1536, 2048, 3072
COL_K, COL_V, COL_WLOW, COL_ALOW, COL_GLOW = 6144, 6272, 6400, 6528, 6656
IN_PAD = 7168
Q_HEAD_ORDER = (0, 4, 1, 5, 2, 6, 3, 7)


def _cparams(*sem):
    return pltpu.CompilerParams(dimension_semantics=sem, vmem_limit_bytes=VMEM_LIMIT)


def _dot(a, b):
    return jnp.dot(a, b, preferred_element_type=F32)


def _dot_nt(a, b):
    return lax.dot_general(a, b, (((1,), (1,)), ((), ())), preferred_element_type=F32)


def _bf(x):
    return x.astype(BF16)


def _dot_split(a, b_exact, terms):
    acc = None
    rem = a
    for _ in range(terms):
        piece = _bf(rem)
        rem = rem - piece.astype(F32)
        part = _dot(piece, b_exact)
        acc = part if acc is None else acc + part
    return acc


def _dot_split_left(a_exact, b, terms):
    acc = None
    rem = b
    for _ in range(terms):
        piece = _bf(rem)
        rem = rem - piece.astype(F32)
        part = _dot(a_exact, piece)
        acc = part if acc is None else acc + part
    return acc


def _group_matrix(scale):
    r = lax.broadcasted_iota(jnp.int32, (LANES, LANES), 0) // HEAD_DIM
    c = lax.broadcasted_iota(jnp.int32, (LANES, LANES), 1) // HEAD_DIM
    return jnp.where(r == c, scale, 0.0).astype(BF16)


def _pick_tile(n, candidates):
    for t in candidates:
        if n % t == 0:
            return t
    raise ValueError(f"no tile in {candidates} divides {n}")


def _mod_rows(mod_ref, nsub, idx, d):
    parts = [jnp.broadcast_to(mod_ref[s][:, idx * d:(idx + 1) * d], (MOD_ROWS, d)) for s in range(nsub)]
    return parts[0] if nsub == 1 else jnp.concatenate(parts, axis=0)


def _rms(x, g):
    return x * lax.rsqrt(jnp.mean(x * x, axis=-1, keepdims=True) + EPS) * g


def _mod_kernel(c_ref, w_ref, b_ref, o_ref):
    cv = c_ref[...]
    s = cv * jax.nn.sigmoid(cv)
    o_ref[0] = jnp.dot(s, w_ref[0], precision=HIGHEST, preferred_element_type=F32) + b_ref[0]


def _modulation(cvec, w_mod, b_mod):
    nl, d, d6 = w_mod.shape
    rows = cvec.shape[0]
    tn = 1024
    return pl.pallas_call(
        _mod_kernel,
        grid=(nl, d6 // tn),
        in_specs=[pl.BlockSpec((rows, d), lambda l, j: (0, 0)),
                  pl.BlockSpec((1, d, tn), lambda l, j: (l, 0, j)),
                  pl.BlockSpec((1, 1, tn), lambda l, j: (l, 0, j))],
        out_specs=pl.BlockSpec((1, rows, tn), lambda l, j: (l, 0, j)),
        out_shape=jax.ShapeDtypeStruct((nl, rows, d6), F32),
        compiler_params=_cparams("parallel", "parallel"),
        name="modulation",
    )(cvec, w_mod, b_mod.reshape(nl, 1, d6))


def _nmm_kernel(x_ref, g_ref, mod_ref, w_ref, o_ref, h_ref, *, nsub, d, shift_idx, scale_idx):
    @pl.when(pl.program_id(1) == 0)
    def _():
        y = _rms(x_ref[...], g_ref[...])
        sc = _mod_rows(mod_ref, nsub, scale_idx, d)
        sh = _mod_rows(mod_ref, nsub, shift_idx, d)
        h_ref[...] = _bf(y * (1.0 + sc) + sh)

    o_ref[...] = _bf(_dot(h_ref[...], w_ref[...]))


def _norm_mod_matmul(x, gain, modblk, w, shift_idx, scale_idx):
    n, d = x.shape
    nout = w.shape[1]
    tm = _pick_tile(n, (1024, 512, 256))
    tn = 1024
    nsub = tm // MOD_ROWS
    kern = functools.partial(_nmm_kernel, nsub=nsub, d=d, shift_idx=shift_idx, scale_idx=scale_idx)
    return pl.pallas_call(
        kern,
        grid=(n // tm, nout // tn),
        in_specs=[pl.BlockSpec((tm, d), lambda i, j: (i, 0)),
                  pl.BlockSpec((1, d), lambda i, j: (0, 0)),
                  pl.BlockSpec((nsub, 1, 6 * d), lambda i, j: (i, 0, 0)),
                  pl.BlockSpec((d, tn), lambda i, j: (0, j))],
        out_specs=pl.BlockSpec((tm, tn), lambda i, j: (i, j)),
        out_shape=jax.ShapeDtypeStruct((n, nout), BF16),
        scratch_shapes=[pltpu.VMEM((tm, d), BF16)],
        compiler_params=_cparams("parallel", "arbitrary"),
        name="norm_mod_in_proj",
    )(x, gain, modblk, w)


def _qkprep_kernel(q_ref, k_ref, v_ref, qg_ref, kg_ref, cos_ref, sin_ref, qo_ref, ko_ref, vo_ref):
    cos = cos_ref[...]
    sin = sin_ref[...]
    avg = _group_matrix(1.0 / HEAD_DIM)
    lane = lax.broadcasted_iota(jnp.int32, cos.shape, 1)
    first = (lane % 32) < 16
    left = lane < HEAD_DIM

    def norm_rope(x, g):
        ms = _dot_split(x * x, avg, 2)
        xn = x * lax.rsqrt(ms + EPS) * g
        partner = jnp.where(first, pltpu.roll(xn, LANES - 16, 1), pltpu.roll(xn, 16, 1))
        return xn * cos + partner * sin

    qscale = (HEAD_DIM ** -0.5) * float(np.log2(np.e))
    for j in range(q_ref.shape[1] // LANES):
        q = q_ref[:, j * LANES:(j + 1) * LANES].astype(F32)
        qo_ref[:, j * LANES:(j + 1) * LANES] = _bf(norm_rope(q, qg_ref[...]) * qscale)
    k = norm_rope(k_ref[...].astype(F32), kg_ref[...])
    zero = jnp.zeros_like(k)
    k0 = _bf(jnp.where(left, k, zero))
    k1 = _bf(jnp.where(left, zero, k))
    v = v_ref[...].astype(F32)
    for c in range(vo_ref.shape[0]):
        rows = slice(c * ATT_TK, (c + 1) * ATT_TK)
        ko_ref[c, :ATT_TK, :] = k0[rows]
        ko_ref[c, ATT_TK:, :] = k1[rows]
        vo_ref[c] = _bf(v[rows, :].T)


def _qk_prep(big, qg, kg, cos, sin, s_tot):
    n = big.shape[0]
    tm = MOD_ROWS
    npos = s_tot // tm
    qw = 512
    vchunks = tm // ATT_TK
    return pl.pallas_call(
        _qkprep_kernel,
        grid=(n // tm,),
        in_specs=[pl.BlockSpec((tm, qw), lambda i: (i, COL_Q // qw)),
                  pl.BlockSpec((tm, LANES), lambda i: (i, COL_K // LANES)),
                  pl.BlockSpec((tm, LANES), lambda i: (i, COL_V // LANES)),
                  pl.BlockSpec((1, LANES), lambda i: (0, 0)),
                  pl.BlockSpec((1, LANES), lambda i: (0, 0)),
                  pl.BlockSpec((tm, LANES), lambda i: (i % npos, 0)),
                  pl.BlockSpec((tm, LANES), lambda i: (i % npos, 0))],
        out_specs=[pl.BlockSpec((tm, qw), lambda i: (i, 0)),
                   pl.BlockSpec((vchunks, 2 * ATT_TK, LANES), lambda i: (i, 0, 0)),
                   pl.BlockSpec((vchunks, LANES, ATT_TK), lambda i: (i, 0, 0))],
        out_shape=[jax.ShapeDtypeStruct((n, qw), BF16),
                   jax.ShapeDtypeStruct((n // ATT_TK, 2 * ATT_TK, LANES), BF16),
                   jax.ShapeDtypeStruct((n // ATT_TK, LANES, ATT_TK), BF16)],
        compiler_params=_cparams("parallel"),
        name="qk_norm_rope",
    )(big, big, big, qg, kg, cos, sin)


def _attn_kernel(q_ref, k_ref, vt_ref, o_ref, acc_ref, *, tq, tk, n_ctx_q, n_ctx_kv, n_kv):
    i = pl.program_id(1)
    nkv = jnp.where(i < n_ctx_q, n_ctx_kv, n_kv)
    hd = HEAD_DIM
    npair = q_ref.shape[1] // LANES
    nh = 2 * npair
    qs = [q_ref[:, j * LANES:(j + 1) * LANES] for j in range(npair)]
    acc_ref[...] = jnp.zeros_like(acc_ref)

    def body(c, carry):
        m, l = carry[:nh], carry[nh:]
        kb = k_ref[c]
        vt = vt_ref[c]
        s = [_dot_nt(kb, qs[j]) for j in range(npair)]
        sh = [s[h // 2][(h % 2) * tk:(h % 2 + 1) * tk] for h in range(nh)]
        n = [jnp.maximum(m[h], jnp.max(sh[h], axis=0, keepdims=True)) for h in range(nh)]
        a = [jnp.exp2(m[h] - n[h]) for h in range(nh)]
        p = [jnp.exp2(sh[h] - n[h]) for h in range(nh)]
        ln = [a[h] * l[h] + jnp.sum(p[h], axis=0, keepdims=True) for h in range(nh)]
        pv = [_dot(vt[(h % 2) * hd:(h % 2 + 1) * hd, :], _bf(p[h])) for h in range(nh)]
        for h in range(nh):
            rows = slice(h * hd, (h + 1) * hd)
            acc_ref[rows, :] = acc_ref[rows, :] * a[h] + pv[h]
        return tuple(n) + tuple(ln)

    init = (jnp.full((1, tq), -1e30, F32),) * nh + (jnp.zeros((1, tq), F32),) * nh
    fin = lax.fori_loop(0, nkv, body, init)
    for j in range(npair):
        inv = jnp.concatenate([jnp.broadcast_to(1.0 / fin[nh + 2 * j + t], (hd, tq)) for t in range(2)], axis=0)
        rows = slice(j * LANES, (j + 1) * LANES)
        o_ref[:, j * LANES:(j + 1) * LANES] = _bf((acc_ref[rows, :] * inv).T)


def _attention(qh, kbd, vt, batch, s_tot, c_len):
    n, qw = qh.shape
    tq = 256
    tk = ATT_TK
    nq = s_tot // tq
    kern = functools.partial(_attn_kernel, tq=tq, tk=tk, n_ctx_q=c_len // tq,
                             n_ctx_kv=c_len // tk, n_kv=s_tot // tk)
    return pl.pallas_call(
        kern,
        grid=(batch, nq),
        in_specs=[pl.BlockSpec((tq, qw), lambda b, i: (b * nq + i, 0)),
                  pl.BlockSpec((s_tot // tk, 2 * tk, LANES), lambda b, i: (b, 0, 0)),
                  pl.BlockSpec((s_tot // tk, LANES, tk), lambda b, i: (b, 0, 0))],
        out_specs=pl.BlockSpec((tq, qw), lambda b, i: (b * nq + i, 0)),
        out_shape=jax.ShapeDtypeStruct((n, qw), BF16),
        scratch_shapes=[pltpu.VMEM((qw, tq), F32)],
        compiler_params=_cparams("parallel", "parallel"),
        name="gqa_attention",
    )(qh, kbd, vt)


def _cmlp_kernel(uv_ref, lng_ref, lnb_ref, ws_ref, bs_ref, o_ref, *, nchunk, width):
    x = uv_ref[...].astype(F32)
    g = 0.5 * x * (1.0 + jnp.tanh(0.7978845608028654 * (x + 0.044715 * (x * x * x))))
    u = g[:, :width]
    v = g[:, width:]
    mu = jnp.mean(v, axis=-1, keepdims=True)
    dv = v - mu
    var = jnp.mean(dv * dv, axis=-1, keepdims=True)
    vn = _bf(dv * lax.rsqrt(var + EPS) * lng_ref[...] + lnb_ref[...])
    ngroups = width // CMLP_CHUNK
    for c in range(nchunk):
        r0 = c * CMLP_CHUNK
        for gi in range(ngroups):
            c0 = gi * CMLP_CHUNK
            s = _dot(ws_ref[gi], vn[r0:r0 + CMLP_CHUNK, c0:c0 + CMLP_CHUNK]) + bs_ref[gi]
            o_ref[r0:r0 + CMLP_CHUNK, c0:c0 + CMLP_CHUNK] = _bf(u[r0:r0 + CMLP_CHUNK, c0:c0 + CMLP_CHUNK] * s)


def _chunk_mlp(big, ln_g, ln_b, ws, bs_b):
    n = big.shape[0]
    width = ln_g.shape[1]
    tr = _pick_tile(n, (512, 256, 128))
    kern = functools.partial(_cmlp_kernel, nchunk=tr // CMLP_CHUNK, width=width)
    ng = ws.shape[0]
    return pl.pallas_call(
        kern,
        grid=(n // tr,),
        in_specs=[pl.BlockSpec((tr, 2 * width), lambda i: (i, COL_UV // (2 * width))),
                  pl.BlockSpec((1, width), lambda i: (0, 0)),
                  pl.BlockSpec((1, width), lambda i: (0, 0)),
                  pl.BlockSpec((ng, CMLP_CHUNK, CMLP_CHUNK), lambda i: (0, 0, 0)),
                  pl.BlockSpec((ng, CMLP_CHUNK, CMLP_CHUNK), lambda i: (0, 0, 0))],
        out_specs=pl.BlockSpec((tr, width), lambda i: (i, 0)),
        out_shape=jax.ShapeDtypeStruct((n, width), BF16),
        compiler_params=_cparams("parallel"),
        name="chunk_gmlp",
    )(big, ln_g, ln_b, ws, bs_b)


def _rwkv_prep_kernel(x_ref, xp_ref, xn_ref, lo_ref, conv_ref, w0_ref, w2_ref, a0_ref, a2_ref,
                      kk0_ref, kk1_ref, rk_ref,
                      v_o, bonus_o, at_f, rt_f, bt_f, kt_f, bb_f, kb_f, pl_f,
                      at_b, rt_b, bt_b, kt_b, bb_b, kb_b, pl_b, *, tm, width, blocks_per_seq, ctx_blocks):
    i = pl.program_id(0)
    j = i % blocks_per_seq
    is_first = jnp.logical_or(j == 0, j == ctx_blocks)
    is_last = jnp.logical_or(j == ctx_blocks - 1, j == blocks_per_seq - 1)
    row = lax.broadcasted_iota(jnp.int32, (tm, width), 0)
    gsum = _group_matrix(1.0)
    halo = xp_ref.shape[0]

    def conv(c):
        cs = slice(c * width, (c + 1) * width)
        x = x_ref[:, cs].astype(F32)
        prev_row = jnp.where(is_first, 0.0, xp_ref[halo - 1:halo, cs].astype(F32))
        next_row = jnp.where(is_last, 0.0, xn_ref[0:1, cs].astype(F32))
        xprev = jnp.where(row == 0, prev_row, pltpu.roll(x, 1, 0))
        xnext = jnp.where(row == tm - 1, next_row, pltpu.roll(x, tm - 1, 0))
        return xprev * conv_ref[0:1, cs] + x * conv_ref[1:2, cs] + xnext * conv_ref[2:3, cs]

    r = conv(0)
    k = conv(1)
    v = conv(2)
    v_o[...] = _bf(v)

    def group_sum(x):
        parts = [_dot_split(x[:, c * LANES:(c + 1) * LANES], gsum, 2) for c in range(width // LANES)]
        return jnp.concatenate(parts, axis=1)

    kk = k * kk0_ref[...]
    kk = kk * lax.rsqrt(group_sum(kk * kk) + 1e-12)
    bonus_o[...] = _bf(group_sum(r * k * rk_ref[...]) * v)

    lo = lo_ref[...].astype(F32)
    wd = w0_ref[...] + _dot(_bf(jnp.tanh(lo[:, :LANES])), w2_ref[...])
    ad = jax.nn.sigmoid(a0_ref[...] + _dot(_bf(lo[:, LANES:]), a2_ref[...]))
    lw = -float(np.exp(-0.5)) * jax.nn.sigmoid(wd)

    r2 = lax.broadcasted_iota(jnp.int32, (tm, tm), 0)
    c2 = lax.broadcasted_iota(jnp.int32, (tm, tm), 1)
    same = (r2 // SCAN_CHUNK) == (c2 // SCAN_CHUNK)
    tri_pre = jnp.where(jnp.logical_and(same, c2 <= r2), 1.0, 0.0).astype(BF16)
    tri_suf = jnp.where(jnp.logical_and(same, c2 >= r2), 1.0, 0.0).astype(BF16)
    nchunk = tm // SCAN_CHUNK

    outs = ((at_f, rt_f, bt_f, kt_f, bb_f, kb_f, pl_f), (at_b, rt_b, bt_b, kt_b, bb_b, kb_b, pl_b))
    for d in range(2):
        ds_ = slice(d * width, (d + 1) * width)
        lwd = lw[:, ds_]
        pre = _dot_split_left(tri_pre, lwd, 3)
        suf = _dot_split_left(tri_suf, lwd, 3)
        cin, rem = (pre, suf - lwd) if d == 0 else (suf, pre - lwd)
        cex = cin - lwd
        a_d = ad[:, ds_]
        b = kk * a_d
        kd = k * (1.0 + (a_d - 1.0) * kk1_ref[...])
        at_o, rt_o, bt_o, kt_o, bb_o, kb_o, pl_o = outs[d]
        at_o[...] = _bf(-kk * jnp.exp(cex))
        rt_o[...] = _bf(r * jnp.exp(cin))
        pinv = jnp.exp(-cin)
        bt_o[...] = _bf(b * pinv)
        kt_o[...] = _bf(kd * pinv)
        pend = jnp.exp(rem)
        bb_o[...] = _bf(b * pend)
        kb_o[...] = _bf(kd * pend)
        for c in range(nchunk):
            last = (c + 1) * SCAN_CHUNK - 1
            pl_o[c] = jnp.exp(pre[last:last + 1, :])


def _rwkv_prep(big, conv_w, w0, w2s, a0, a2s, kk0, kk1, rk, s_tot, c_len):
    n = big.shape[0]
    width = rk.shape[1]
    tm = MOD_ROWS
    halo = 16
    hb = tm // halo
    nhalo = n // halo
    nchunk = tm // SCAN_CHUNK
    kern = functools.partial(_rwkv_prep_kernel, tm=tm, width=width, blocks_per_seq=s_tot // tm,
                             ctx_blocks=c_len // tm)
    tok = pl.BlockSpec((tm, width), lambda i: (i, 0))
    plspec = pl.BlockSpec((nchunk, 1, width), lambda i: (i, 0, 0))
    tok_shape = jax.ShapeDtypeStruct((n, width), BF16)
    pl_shape = jax.ShapeDtypeStruct((n // SCAN_CHUNK, 1, width), F32)
    full = lambda a: pl.BlockSpec(a.shape, lambda i: (0,) * a.ndim)
    return pl.pallas_call(
        kern,
        grid=(n // tm,),
        in_specs=[pl.BlockSpec((tm, 3 * width), lambda i: (i, 0)),
                  pl.BlockSpec((halo, 3 * width), lambda i: (jnp.maximum(i * hb - 1, 0), 0)),
                  pl.BlockSpec((halo, 3 * width), lambda i: (jnp.minimum((i + 1) * hb, nhalo - 1), 0)),
                  pl.BlockSpec((tm, 2 * LANES), lambda i: (i, COL_WLOW // (2 * LANES))),
                  full(conv_w), full(w0), full(w2s), full(a0), full(a2s), full(kk0), full(kk1), full(rk)],
        out_specs=[tok, tok] + [tok] * 6 + [plspec] + [tok] * 6 + [plspec],
        out_shape=[tok_shape, tok_shape] + [tok_shape] * 6 + [pl_shape] + [tok_shape] * 6 + [pl_shape],
        compiler_params=_cparams("parallel"),
        name="rwkv_prepare",
    )(big, big, big, big, conv_w, w0, w2s, a0, a2s, kk0, kk1, rk)


def _scan_chunks(chains):
    L = SCAN_CHUNK
    lane = lax.broadcasted_iota(jnp.int32, (L, LANES), 1)
    m0 = _bf(jnp.where(lane < HEAD_DIM, 1.0, 0.0))
    m1 = _bf(jnp.where(lane < HEAD_DIM, 0.0, 1.0))

    def stack(x):
        return jnp.concatenate([x * m0, x * m1], axis=0)

    r2 = lax.broadcasted_iota(jnp.int32, (2 * L, 2 * L), 0)
    c2 = lax.broadcasted_iota(jnp.int32, (2 * L, 2 * L), 1)
    same = (r2 // L) == (c2 // L)
    tr, tc = r2 % L, c2 % L
    masks = {True: (jnp.logical_and(same, tc < tr), jnp.logical_and(same, tc <= tr)),
             False: (jnp.logical_and(same, tc > tr), jnp.logical_and(same, tc >= tr))}
    eye = r2 == c2
    fwd = [ch[9] for ch in chains]
    nc = range(len(chains))

    a_s = [stack(ch[0]) for ch in chains]
    r_s = [stack(ch[1]) for ch in chains]
    v_s = [stack(ch[6]) for ch in chains]
    big1 = [_dot_nt(jnp.concatenate([a_s[i], r_s[i]], axis=0),
                    jnp.concatenate([stack(chains[i][2]), stack(chains[i][3])], axis=0)) for i in nc]
    mab = [jnp.where(masks[fwd[i]][0], big1[i][:2 * L, :2 * L], 0.0) for i in nc]
    mak = [_bf(jnp.where(masks[fwd[i]][0], big1[i][:2 * L, 2 * L:], 0.0)) for i in nc]
    lhs_top = [_bf(jnp.where(jnp.concatenate([masks[fwd[i]][1]] * 2, axis=1), big1[i][2 * L:], 0.0)) for i in nc]
    mv = [_dot(mak[i], v_s[i]) for i in nc]
    x = [jnp.concatenate([a_s[i].astype(F32), mv[i]], axis=1) for i in nc]
    mp = mab
    steps = int(np.log2(L))
    for it in range(steps):
        if it < steps - 1:
            res = [_dot(_bf(mp[i]), _bf(jnp.concatenate([mp[i], x[i]], axis=1))) for i in nc]
            x = [x[i] + res[i][:, 2 * L:] for i in nc]
            mp = [res[i][:, :2 * L] for i in nc]
        else:
            res = [_dot(_bf(mp[i]), _bf(x[i])) for i in nc]
            x = [x[i] + res[i] for i in nc]
    rhs2 = [jnp.concatenate([_bf(x[i]), jnp.concatenate([jnp.zeros_like(v_s[i]), v_s[i]], axis=1)], axis=0)
            for i in nc]
    lhs_bot = [_bf(jnp.concatenate([stack(chains[i][4]), stack(chains[i][5])], axis=0).astype(F32).T)
               for i in nc]
    res2 = [_dot(jnp.concatenate([lhs_top[i], lhs_bot[i]], axis=0), rhs2[i]) for i in nc]
    lhs3 = [_bf(jnp.concatenate(
        [r_s[i].astype(F32) + res2[i][:2 * L, :LANES],
         res2[i][2 * L:, :LANES] + jnp.where(eye, jnp.broadcast_to(chains[i][7], (LANES, LANES)), 0.0)], axis=0))
        for i in nc]
    res3 = [_dot(lhs3[i], _bf(chains[i][8])) for i in nc]
    out = []
    for i in nc:
        ys = res3[i][:2 * L] + res2[i][:2 * L, LANES:]
        out.append((ys[:L] + ys[L:], res3[i][2 * L:] + res2[i][2 * L:, LANES:]))
    return out


def _rwkv_scan_kernel(v_f, at_f, rt_f, bt_f, kt_f, bb_f, kb_f, pl_f,
                      v_b, at_b, rt_b, bt_b, kt_b, bb_b, kb_b, pl_b,
                      yf_ref, yb_ref, z_ref, *, npairs):
    @pl.when(pl.program_id(1) == 0)
    def _():
        z_ref[...] = jnp.zeros_like(z_ref)

    dirs = ((v_f, at_f, rt_f, bt_f, kt_f, bb_f, kb_f, pl_f, yf_ref, True),
            (v_b, at_b, rt_b, bt_b, kt_b, bb_b, kb_b, pl_b, yb_ref, False))
    chains, dest = [], []
    for d, (v, at, rt, bt, kt, bb, kb, plr, y_ref, fwd) in enumerate(dirs):
        for p in range(npairs):
            cs = slice(p * LANES, (p + 1) * LANES)
            chains.append((at[:, cs], rt[:, cs], bt[:, cs], kt[:, cs], bb[:, cs], kb[:, cs],
                           v[:, cs], plr[0][:, cs], z_ref[d, p], fwd))
            dest.append((y_ref, cs, d, p))
    for (y_ref, cs, d, p), (y, znew) in zip(dest, _scan_chunks(chains)):
        y_ref[:, cs] = y
        z_ref[d, p] = znew


def _rwkv_scan(prep, batch, s_tot, c_len):
    (v, _bonus, at_f, rt_f, bt_f, kt_f, bb_f, kb_f, pl_f, at_b, rt_b, bt_b, kt_b, bb_b, kb_b, pl_b) = prep
    n, width = v.shape
    L = SCAN_CHUNK
    nch = s_tot // L
    ncc = c_len // L
    npairs = width // LANES

    def fmap(b, c):
        return (b * nch + c, 0)

    def bmap(b, c):
        return (b * nch + jnp.where(c < ncc, ncc - 1 - c, nch - 1 - (c - ncc)), 0)

    def tok(m):
        return pl.BlockSpec((L, width), m)

    def pls(m):
        return pl.BlockSpec((1, 1, width), lambda b, c: m(b, c) + (0,))

    kern = functools.partial(_rwkv_scan_kernel, npairs=npairs)
    return pl.pallas_call(
        kern,
        grid=(batch, nch),
        in_specs=[tok(fmap)] * 7 + [pls(fmap)] + [tok(bmap)] * 7 + [pls(bmap)],
        out_specs=[tok(fmap), tok(bmap)],
        out_shape=[jax.ShapeDtypeStruct((n, width), F32)] * 2,
        scratch_shapes=[pltpu.VMEM((2, npairs, LANES, LANES), F32)],
        compiler_params=_cparams("parallel", "arbitrary"),
        name="rwkv_scan",
    )(v, at_f, rt_f, bt_f, kt_f, bb_f, kb_f, pl_f, v, at_b, rt_b, bt_b, kt_b, bb_b, kb_b, pl_b)


def _rwkv_readout_kernel(yf_ref, yb_ref, bonus_ref, gl_ref, g2_ref, lg_ref, lb_ref, o_ref):
    avg = _group_matrix(1.0 / HEAD_DIM)
    gate = _dot(_bf(jax.nn.sigmoid(gl_ref[...].astype(F32))), g2_ref[...])
    for c in range(o_ref.shape[1] // LANES):
        cs = slice(c * LANES, (c + 1) * LANES)
        y = yf_ref[:, cs] + yb_ref[:, cs]
        mu = _dot_split(y, avg, 2)
        dy = y - mu
        var = _dot_split(dy * dy, avg, 2)
        yn = dy * lax.rsqrt(var + LNX_EPS) * lg_ref[:, cs] + lb_ref[:, cs]
        o_ref[:, cs] = _bf((yn + bonus_ref[:, cs].astype(F32)) * gate[:, cs])


def _rwkv_readout(yf, yb, bonus, big, g2, lnx_g, lnx_b):
    n, width = yf.shape
    tm = _pick_tile(n, (512, 256))
    tok = pl.BlockSpec((tm, width), lambda i: (i, 0))
    full = lambda a: pl.BlockSpec(a.shape, lambda i: (0,) * a.ndim)
    return pl.pallas_call(
        _rwkv_readout_kernel,
        grid=(n // tm,),
        in_specs=[tok, tok, tok, pl.BlockSpec((tm, LANES), lambda i: (i, COL_GLOW // LANES)),
                  full(g2), full(lnx_g), full(lnx_b)],
        out_specs=tok,
        out_shape=jax.ShapeDtypeStruct((n, width), BF16),
        compiler_params=_cparams("parallel"),
        name="rwkv_readout",
    )(yf, yb, bonus, big, g2, lnx_g, lnx_b)


def _merge_kernel(oa_ref, ob_ref, oc_ref, gt_ref, x_ref, wb_ref, wo_ref, ng_ref, mod_ref, o_ref, *, nsub, d):
    y = None
    for br, ref in enumerate((oa_ref, ob_ref, oc_ref)):
        g = jax.nn.sigmoid(gt_ref[:, br * d:(br + 1) * d].astype(F32))
        t = g * _dot(ref[...], wb_ref[br])
        y = t if y is None else y + t
    o = _dot(_bf(y), wo_ref[...])
    o_ref[...] = x_ref[...] + _mod_rows(mod_ref, nsub, 2, d) * _rms(o, ng_ref[...])


def _merge(oa, ob, oc, big, x, wb, wo, ng, modblk):
    n, d = x.shape
    width = oa.shape[1]
    tm = _pick_tile(n, (512, 256))
    nsub = tm // MOD_ROWS
    kern = functools.partial(_merge_kernel, nsub=nsub, d=d)
    br = pl.BlockSpec((tm, width), lambda i: (i, 0))
    return pl.pallas_call(
        kern,
        grid=(n // tm,),
        in_specs=[br, br, br,
                  pl.BlockSpec((tm, 3 * d), lambda i: (i, COL_GATES // (3 * d))),
                  pl.BlockSpec((tm, d), lambda i: (i, 0)),
                  pl.BlockSpec(wb.shape, lambda i: (0, 0, 0)),
                  pl.BlockSpec(wo.shape, lambda i: (0, 0)),
                  pl.BlockSpec((1, d), lambda i: (0, 0)),
                  pl.BlockSpec((nsub, 1, 6 * d), lambda i: (i, 0, 0))],
        out_specs=pl.BlockSpec((tm, d), lambda i: (i, 0)),
        out_shape=jax.ShapeDtypeStruct((n, d), F32),
        compiler_params=_cparams("parallel"),
        name="merge_out_proj",
    )(oa, ob, oc, big, x, wb, wo, ng, modblk)


def _ffn_kernel(x_ref, g_ref, mod_ref, wg_ref, wu_ref, wd_ref, ng_ref, o_ref, h_ref, acc_ref, *, nsub, d):
    j = pl.program_id(1)

    @pl.when(j == 0)
    def _():
        y = _rms(x_ref[...], g_ref[...])
        h_ref[...] = _bf(y * (1.0 + _mod_rows(mod_ref, nsub, 4, d)) + _mod_rows(mod_ref, nsub, 3, d))
        acc_ref[...] = jnp.zeros_like(acc_ref)

    h = h_ref[...]
    g = _dot(h, wg_ref[...])
    u = _dot(h, wu_ref[...])
    acc_ref[...] += _dot(_bf(g * jax.nn.sigmoid(g) * u), wd_ref[...])

    @pl.when(j == pl.num_programs(1) - 1)
    def _():
        o_ref[...] = x_ref[...] + _mod_rows(mod_ref, nsub, 5, d) * _rms(acc_ref[...], ng_ref[...])


def _dense_ffn(x, gain_in, gain_out, modblk, w_gu, w_down):
    n, d = x.shape
    f = w_down.shape[0]
    tm = _pick_tile(n, (1024, 512, 256))
    tf = _pick_tile(f, (1408, 1024, 512, 256, 128))
    nf = f // tf
    nsub = tm // MOD_ROWS
    kern = functools.partial(_ffn_kernel, nsub=nsub, d=d)
    return pl.pallas_call(
        kern,
        grid=(n // tm, nf),
        in_specs=[pl.BlockSpec((tm, d), lambda i, j: (i, 0)),
                  pl.BlockSpec((1, d), lambda i, j: (0, 0)),
                  pl.BlockSpec((nsub, 1, 6 * d), lambda i, j: (i, 0, 0)),
                  pl.BlockSpec((d, tf), lambda i, j: (0, j)),
                  pl.BlockSpec((d, tf), lambda i, j: (0, j + nf)),
                  pl.BlockSpec((tf, d), lambda i, j: (j, 0)),
                  pl.BlockSpec((1, d), lambda i, j: (0, 0))],
        out_specs=pl.BlockSpec((tm, d), lambda i, j: (i, 0)),
        out_shape=jax.ShapeDtypeStruct((n, d), F32),
        scratch_shapes=[pltpu.VMEM((tm, d), BF16), pltpu.VMEM((tm, d), F32)],
        compiler_params=_cparams("parallel", "arbitrary"),
        name="dense_swiglu_ffn",
    )(x, gain_in, modblk, w_gu, w_gu, w_down, gain_out)


def _router_kernel(x_ref, g_ref, mod_ref, wr_ref, h_ref, comb_ref, *, nsub, d):
    y = _rms(x_ref[...], g_ref[...])
    h = y * (1.0 + _mod_rows(mod_ref, nsub, 4, d)) + _mod_rows(mod_ref, nsub, 3, d)
    h_ref[...] = _bf(h)
    logits = jnp.dot(h, wr_ref[...], precision=HIGHEST, preferred_element_type=F32)
    lane = lax.broadcasted_iota(jnp.int32, logits.shape, 1)
    ninf = jnp.float32(-jnp.inf)
    logits = jnp.where(lane < N_EXPERTS, logits, ninf)
    m1 = jnp.max(logits, axis=-1, keepdims=True)
    i1 = jnp.min(jnp.where(logits == m1, lane, LANES), axis=-1, keepdims=True)
    rest = jnp.where(lane == i1, ninf, logits)
    m2 = jnp.max(rest, axis=-1, keepdims=True)
    i2 = jnp.min(jnp.where(rest == m2, lane, LANES), axis=-1, keepdims=True)
    e2 = jnp.exp(m2 - m1)
    w1 = 1.0 / (1.0 + e2)
    comb_ref[...] = jnp.where(lane == i1, w1, 0.0) + jnp.where(lane == i2, e2 * w1, 0.0)


def _router(x, gain_in, modblk, w_router_pad):
    n, d = x.shape
    tm = _pick_tile(n, (512, 256))
    nsub = tm // MOD_ROWS
    kern = functools.partial(_router_kernel, nsub=nsub, d=d)
    return pl.pallas_call(
        kern,
        grid=(n // tm,),
        in_specs=[pl.BlockSpec((tm, d), lambda i: (i, 0)),
                  pl.BlockSpec((1, d), lambda i: (0, 0)),
                  pl.BlockSpec((nsub, 1, 6 * d), lambda i: (i, 0, 0)),
                  pl.BlockSpec((d, LANES), lambda i: (0, 0))],
        out_specs=[pl.BlockSpec((tm, d), lambda i: (i, 0)), pl.BlockSpec((tm, LANES), lambda i: (i, 0))],
        out_shape=[jax.ShapeDtypeStruct((n, d), BF16), jax.ShapeDtypeStruct((n, LANES), F32)],
        compiler_params=_cparams("parallel"),
        name="moe_router",
    )(x, gain_in, modblk, w_router_pad)


def _moe_kernel(h_ref, comb_ref, x_ref, wg_ref, wu_ref, wd_ref, ng_ref, mod_ref, o_ref, acc_ref, *, nsub, d):
    e = pl.program_id(1)
    j = pl.program_id(2)

    @pl.when(jnp.logical_and(e == 0, j == 0))
    def _():
        acc_ref[...] = jnp.zeros_like(acc_ref)

    h = h_ref[...]
    g = _dot(h, wg_ref[0])
    u = _dot(h, wu_ref[0])
    comb = comb_ref[...]
    lane = lax.broadcasted_iota(jnp.int32, comb.shape, 1)
    ce = jnp.sum(jnp.where(lane == e, comb, 0.0), axis=-1, keepdims=True)
    acc_ref[...] += _dot(_bf(g * jax.nn.sigmoid(g) * u * ce), wd_ref[0])

    @pl.when(jnp.logical_and(e == pl.num_programs(1) - 1, j == pl.num_programs(2) - 1))
    def _():
        o_ref[...] = x_ref[...] + _mod_rows(mod_ref, nsub, 5, d) * _rms(acc_ref[...], ng_ref[...])


def _moe_ffn(x, hb, comb, gain_out, modblk, w_gu, w_down):
    n, d = x.shape
    ne, f, _ = w_down.shape
    tm = _pick_tile(n, (1024, 512, 256))
    tf = _pick_tile(f, (896, 512, 256, 128))
    nf = f // tf
    nsub = tm // MOD_ROWS
    kern = functools.partial(_moe_kernel, nsub=nsub, d=d)
    return pl.pallas_call(
        kern,
        grid=(n // tm, ne, nf),
        in_specs=[pl.BlockSpec((tm, d), lambda i, e, j: (i, 0)),
                  pl.BlockSpec((tm, LANES), lambda i, e, j: (i, 0)),
                  pl.BlockSpec((tm, d), lambda i, e, j: (i, 0)),
                  pl.BlockSpec((1, d, tf), lambda i, e, j: (e, 0, j)),
                  pl.BlockSpec((1, d, tf), lambda i, e, j: (e, 0, j + nf)),
                  pl.BlockSpec((1, tf, d), lambda i, e, j: (e, j, 0)),
                  pl.BlockSpec((1, d), lambda i, e, j: (0, 0)),
                  pl.BlockSpec((nsub, 1, 6 * d), lambda i, e, j: (i, 0, 0))],
        out_specs=pl.BlockSpec((tm, d), lambda i, e, j: (i, 0)),
        out_shape=jax.ShapeDtypeStruct((n, d), F32),
        scratch_shapes=[pltpu.VMEM((tm, d), F32)],
        compiler_params=_cparams("parallel", "arbitrary", "arbitrary"),
        name="moe_swiglu_ffn",
    )(hb, comb, x, w_gu, w_gu, w_down, gain_out, modblk)


def _rope_tables(t_len, c_len):
    pairs = HEAD_DIM // 4
    rows = t_len // GRID_W
    row = jnp.repeat(jnp.arange(rows, dtype=F32), GRID_W)
    col = jnp.tile(jnp.arange(GRID_W, dtype=F32), rows)
    freqs = ROPE_BASE ** (-jnp.arange(pairs, dtype=F32) / pairs)
    ar = row[:, None] * freqs
    ac = col[:, None] * freqs
    cos = jnp.concatenate([jnp.cos(ar), jnp.cos(ar), jnp.cos(ac), jnp.cos(ac)], axis=1)
    sin = jnp.concatenate([-jnp.sin(ar), jnp.sin(ar), -jnp.sin(ac), jnp.sin(ac)], axis=1)
    cos = jnp.concatenate([jnp.ones((c_len, HEAD_DIM), F32), cos], axis=0)
    sin = jnp.concatenate([jnp.zeros((c_len, HEAD_DIM), F32), sin], axis=0)
    return jnp.tile(cos, (1, 2)), jnp.tile(sin, (1, 2))


def _block_diag2(w):
    z = jnp.zeros_like(w[0])
    return jnp.concatenate([jnp.concatenate([w[0], z], axis=1), jnp.concatenate([z, w[1]], axis=1)], axis=0)


def kernel(x, c, ctx, c_ctx, w_mod, b_mod, norm_gain, w_in, qk_gain, rwkv_conv, decay_w0, decay_w2, iclr_a0, iclr_a2, key_k, bonus_rk, gate_g2, lnx_gain, lnx_bias, cmlp_ln_gain, cmlp_ln_bias, cmlp_ws, cmlp_bs, w_branch, w_out, ffn_w_gu, ffn_w_down, moe_router, moe_w_gu, moe_w_down):
    batch, t_len, d = x.shape
    c_len = ctx.shape[1]
    depth = w_mod.shape[0]
    s_tot = c_len + t_len
    n = batch * s_tot
    assert c_len % MOD_ROWS == 0 and t_len % MOD_ROWS == 0 and d % LANES == 0
    width = bonus_rk.shape[1] * bonus_rk.shape[2]
    nheads = width // HEAD_DIM

    xs = jnp.concatenate([ctx, x], axis=1).reshape(n, d)

    mod_rows = 8 * ((batch + 1 + 7) // 8)
    cvec = jnp.zeros((mod_rows, d), F32).at[0].set(c_ctx).at[1:batch + 1].set(c)
    mods = _modulation(cvec, w_mod, b_mod)
    mod_ctx = jnp.broadcast_to(mods[:, 0:1, None, :], (depth, batch, c_len // MOD_ROWS, 6 * d))
    mod_lat = jnp.broadcast_to(mods[:, 1:batch + 1, None, :], (depth, batch, t_len // MOD_ROWS, 6 * d))
    modblk_all = jnp.concatenate([mod_ctx, mod_lat], axis=2).reshape(depth, n // MOD_ROWS, 1, 6 * d)

    order = np.array(Q_HEAD_ORDER)
    nl = depth
    wq = w_in[:, :, 2048:2560].reshape(nl, d, nheads, HEAD_DIM)[:, :, order].reshape(nl, d, width)
    w_in_p = jnp.concatenate([
        w_in[:, :, 256:1792], wq, w_in[:, :, 2688:3712], w_in[:, :, 3712:6784],
        w_in[:, :, 0:128], w_in[:, :, 128:256], w_in[:, :, 1792:1920], w_in[:, :, 1920:2048],
        w_in[:, :, 2560:2688], jnp.zeros((nl, d, IN_PAD - 6784), F32)], axis=2).astype(BF16)
    wb = w_branch.astype(BF16)
    wb0 = wb[:, 0].reshape(nl, nheads, HEAD_DIM, d)[:, order].reshape(nl, width, d)
    wb = jnp.concatenate([wb0[:, None], wb[:, 1:]], axis=1)
    wo = w_out.astype(BF16)
    cos, sin = _rope_tables(t_len, c_len)
    qg = jnp.tile(qk_gain[:, 0], (1, 2))[:, None, :]
    kg = jnp.tile(qk_gain[:, 1], (1, 2))[:, None, :]
    ws_b = cmlp_ws.astype(BF16)
    bs_b = jnp.broadcast_to(cmlp_bs[..., None], cmlp_bs.shape + (CMLP_CHUNK,))
    w2s = jnp.stack([_block_diag2(decay_w2[l]) for l in range(nl)]).astype(BF16)
    a2s = jnp.stack([_block_diag2(iclr_a2[l]) for l in range(nl)]).astype(BF16)
    w0 = decay_w0.reshape(nl, 1, 2 * width)
    a0 = iclr_a0.reshape(nl, 1, 2 * width)
    rk = bonus_rk.reshape(nl, 1, width)
    g2 = gate_g2.astype(BF16)
    ffn_gu = ffn_w_gu.astype(BF16)
    ffn_dn = ffn_w_down.astype(BF16)
    moe_gu = moe_w_gu.astype(BF16)
    moe_dn = moe_w_down.astype(BF16)
    router_pad = jnp.pad(moe_router, ((0, 0), (0, 0), (0, LANES - moe_router.shape[2])))

    for l in range(depth):
        modblk = modblk_all[l]
        ng = norm_gain[l]
        big = _norm_mod_matmul(xs, ng[0:1], modblk, w_in_p[l], 0, 1)
        qh, kbd, vt = _qk_prep(big, qg[l], kg[l], cos, sin, s_tot)
        oa = _attention(qh, kbd, vt, batch, s_tot, c_len)
        prep = _rwkv_prep(big, rwkv_conv[l], w0[l], w2s[l], a0[l], a2s[l],
                          key_k[l, 0:1], key_k[l, 1:2], rk[l], s_tot, c_len)
        yf, yb = _rwkv_scan(prep, batch, s_tot, c_len)
        ob = _rwkv_readout(yf, yb, prep[1], big, g2[l], lnx_gain[l][None], lnx_bias[l][None])
        oc = _chunk_mlp(big, cmlp_ln_gain[l][None], cmlp_ln_bias[l][None], ws_b[l], bs_b[l])
        xs = _merge(oa, ob, oc, big, xs, wb[l], wo[l], ng[1:2], modblk)
        if l % 2 == 0:
            xs = _dense_ffn(xs, ng[2:3], ng[3:4], modblk, ffn_gu[l // 2], ffn_dn[l // 2])
        else:
            hb, comb = _router(xs, ng[2:3], modblk, router_pad[l // 2])
            xs = _moe_ffn(xs, hb, comb, ng[3:4], modblk, moe_gu[l // 2], moe_dn[l // 2])
    return xs.reshape(batch, s_tot, d)[:, c_len:, :]
```

```python
import functools

import jax
import jax.numpy as jnp
import numpy as np
from jax import lax
from jax.experimental import pallas as pl
from jax.experimental.pallas import tpu as pltpu

F32 = jnp.float32
BF16 = jnp.bfloat16
HIGHEST = lax.Precision.HIGHEST

EPS = 1e-6
LNX_EPS = 64e-5
HEAD_DIM = 64
ROPE_BASE = 10000.0
GRID_W = 64
LANES = 128
MOD_ROWS = 256
SCAN_CHUNK = 64
CMLP_CHUNK = 128
ATT_TK = 128
N_EXPERTS = 8
MOE_ROW_BLOCK = 256
VMEM_LIMIT = 56 * 1024 * 1024

COL_RKV, COL_Q, COL_UV, COL_GATES = 0, 1536, 2048, 3072
COL_K, COL_V, COL_WLOW, COL_ALOW, COL_GLOW = 6144, 6272, 6400, 6528, 6656
IN_PAD = 7168
Q_HEAD_ORDER = (0, 4, 1, 5, 2, 6, 3, 7)


def _cparams(*sem):
    return pltpu.CompilerParams(dimension_semantics=sem, vmem_limit_bytes=VMEM_LIMIT)


def _dot(a, b):
    return jnp.dot(a, b, preferred_element_type=F32)


def _dot_nt(a, b):
    return lax.dot_general(a, b, (((1,), (1,)), ((), ())), preferred_element_type=F32)


def _bf(x):
    return x.astype(BF16)


def _dot_split(a, b_exact, terms):
    acc = None
    rem = a
    for _ in range(terms):
        piece = _bf(rem)
        rem = rem - piece.astype(F32)
        part = _dot(piece, b_exact)
        acc = part if acc is None else acc + part
    return acc


def _dot_split_left(a_exact, b, terms):
    acc = None
    rem = b
    for _ in range(terms):
        piece = _bf(rem)
        rem = rem - piece.astype(F32)
        part = _dot(a_exact, piece)
        acc = part if acc is None else acc + part
    return acc


def _group_matrix(scale):
    r = lax.broadcasted_iota(jnp.int32, (LANES, LANES), 0) // HEAD_DIM
    c = lax.broadcasted_iota(jnp.int32, (LANES, LANES), 1) // HEAD_DIM
    return jnp.where(r == c, scale, 0.0).astype(BF16)


def _pick_tile(n, candidates):
    for t in candidates:
        if n % t == 0:
            return t
    raise ValueError(f"no tile in {candidates} divides {n}")


def _mod_rows(mod_ref, nsub, idx, d):
    parts = [jnp.broadcast_to(mod_ref[s][:, idx * d:(idx + 1) * d], (MOD_ROWS, d)) for s in range(nsub)]
    return parts[0] if nsub == 1 else jnp.concatenate(parts, axis=0)


def _rms(x, g):
    return x * lax.rsqrt(jnp.mean(x * x, axis=-1, keepdims=True) + EPS) * g


def _mod_kernel(c_ref, w_ref, b_ref, o_ref):
    cv = c_ref[...]
    s = cv * jax.nn.sigmoid(cv)
    o_ref[0] = jnp.dot(s, w_ref[0], precision=HIGHEST, preferred_element_type=F32) + b_ref[0]


def _modulation(cvec, w_mod, b_mod):
    nl, d, d6 = w_mod.shape
    rows = cvec.shape[0]
    tn = 1024
    return pl.pallas_call(
        _mod_kernel,
        grid=(nl, d6 // tn),
        in_specs=[pl.BlockSpec((rows, d), lambda l, j: (0, 0)),
                  pl.BlockSpec((1, d, tn), lambda l, j: (l, 0, j)),
                  pl.BlockSpec((1, 1, tn), lambda l, j: (l, 0, j))],
        out_specs=pl.BlockSpec((1, rows, tn), lambda l, j: (l, 0, j)),
        out_shape=jax.ShapeDtypeStruct((nl, rows, d6), F32),
        compiler_params=_cparams("parallel", "parallel"),
        name="modulation",
    )(cvec, w_mod, b_mod.reshape(nl, 1, d6))


def _nmm_kernel(x_ref, g_ref, mod_ref, w_ref, o_ref, h_ref, *, nsub, d, shift_idx, scale_idx):
    @pl.when(pl.program_id(1) == 0)
    def _():
        y = _rms(x_ref[...], g_ref[...])
        sc = _mod_rows(mod_ref, nsub, scale_idx, d)
        sh = _mod_rows(mod_ref, nsub, shift_idx, d)
        h_ref[...] = _bf(y * (1.0 + sc) + sh)

    o_ref[...] = _bf(_dot(h_ref[...], w_ref[...]))


def _norm_mod_matmul(x, gain, modblk, w, shift_idx, scale_idx):
    n, d = x.shape
    nout = w.shape[1]
    tm = _pick_tile(n, (1024, 512, 256))
    tn = 1024
    nsub = tm // MOD_ROWS
    kern = functools.partial(_nmm_kernel, nsub=nsub, d=d, shift_idx=shift_idx, scale_idx=scale_idx)
    return pl.pallas_call(
        kern,
        grid=(n // tm, nout // tn),
        in_specs=[pl.BlockSpec((tm, d), lambda i, j: (i, 0)),
                  pl.BlockSpec((1, d), lambda i, j: (0, 0)),
                  pl.BlockSpec((nsub, 1, 6 * d), lambda i, j: (i, 0, 0)),
                  pl.BlockSpec((d, tn), lambda i, j: (0, j))],
        out_specs=pl.BlockSpec((tm, tn), lambda i, j: (i, j)),
        out_shape=jax.ShapeDtypeStruct((n, nout), BF16),
        scratch_shapes=[pltpu.VMEM((tm, d), BF16)],
        compiler_params=_cparams("parallel", "arbitrary"),
        name="norm_mod_in_proj",
    )(x, gain, modblk, w)


def _qkprep_kernel(q_ref, k_ref, v_ref, qg_ref, kg_ref, cos_ref, sin_ref, qo_ref, ko_ref, vo_ref):
    cos = cos_ref[...]
    sin = sin_ref[...]
    avg = _group_matrix(1.0 / HEAD_DIM)
    lane = lax.broadcasted_iota(jnp.int32, cos.shape, 1)
    first = (lane % 32) < 16
    left = lane < HEAD_DIM

    def norm_rope(x, g):
        ms = _dot_split(x * x, avg, 2)
        xn = x * lax.rsqrt(ms + EPS) * g
        partner = jnp.where(first, pltpu.roll(xn, LANES - 16, 1), pltpu.roll(xn, 16, 1))
        return xn * cos + partner * sin

    qscale = (HEAD_DIM ** -0.5) * float(np.log2(np.e))
    for j in range(q_ref.shape[1] // LANES):
        q = q_ref[:, j * LANES:(j + 1) * LANES].astype(F32)
        qo_ref[:, j * LANES:(j + 1) * LANES] = _bf(norm_rope(q, qg_ref[...]) * qscale)
    k = norm_rope(k_ref[...].astype(F32), kg_ref[...])
    zero = jnp.zeros_like(k)
    k0 = _bf(jnp.where(left, k, zero))
    k1 = _bf(jnp.where(left, zero, k))
    v = v_ref[...].astype(F32)
    for c in range(vo_ref.shape[0]):
        rows = slice(c * ATT_TK, (c + 1) * ATT_TK)
        ko_ref[c, :ATT_TK, :] = k0[rows]
        ko_ref[c, ATT_TK:, :] = k1[rows]
        vo_ref[c] = _bf(v[rows, :].T)


def _qk_prep(big, qg, kg, cos, sin, s_tot):
    n = big.shape[0]
    tm = MOD_ROWS
    npos = s_tot // tm
    qw = 512
    vchunks = tm // ATT_TK
    return pl.pallas_call(
        _qkprep_kernel,
        grid=(n // tm,),
        in_specs=[pl.BlockSpec((tm, qw), lambda i: (i, COL_Q // qw)),
                  pl.BlockSpec((tm, LANES), lambda i: (i, COL_K // LANES)),
                  pl.BlockSpec((tm, LANES), lambda i: (i, COL_V // LANES)),
                  pl.BlockSpec((1, LANES), lambda i: (0, 0)),
                  pl.BlockSpec((1, LANES), lambda i: (0, 0)),
                  pl.BlockSpec((tm, LANES), lambda i: (i % npos, 0)),
                  pl.BlockSpec((tm, LANES), lambda i: (i % npos, 0))],
        out_specs=[pl.BlockSpec((tm, qw), lambda i: (i, 0)),
                   pl.BlockSpec((vchunks, 2 * ATT_TK, LANES), lambda i: (i, 0, 0)),
                   pl.BlockSpec((vchunks, LANES, ATT_TK), lambda i: (i, 0, 0))],
        out_shape=[jax.ShapeDtypeStruct((n, qw), BF16),
                   jax.ShapeDtypeStruct((n // ATT_TK, 2 * ATT_TK, LANES), BF16),
                   jax.ShapeDtypeStruct((n // ATT_TK, LANES, ATT_TK), BF16)],
        compiler_params=_cparams("parallel"),
        name="qk_norm_rope",
    )(big, big, big, qg, kg, cos, sin)


def _attn_kernel(q_ref, k_ref, vt_ref, o_ref, acc_ref, *, tq, tk, n_ctx_q, n_ctx_kv, n_kv):
    i = pl.program_id(1)
    nkv = jnp.where(i < n_ctx_q, n_ctx_kv, n_kv)
    hd = HEAD_DIM
    npair = q_ref.shape[1] // LANES
    nh = 2 * npair
    qs = [q_ref[:, j * LANES:(j + 1) * LANES] for j in range(npair)]
    acc_ref[...] = jnp.zeros_like(acc_ref)

    def body(c, carry):
        m, l = carry[:nh], carry[nh:]
        kb = k_ref[c]
        vt = vt_ref[c]
        s = [_dot_nt(kb, qs[j]) for j in range(npair)]
        sh = [s[h // 2][(h % 2) * tk:(h % 2 + 1) * tk] for h in range(nh)]
        n = [jnp.maximum(m[h], jnp.max(sh[h], axis=0, keepdims=True)) for h in range(nh)]
        a = [jnp.exp2(m[h] - n[h]) for h in range(nh)]
        p = [jnp.exp2(sh[h] - n[h]) for h in range(nh)]
        ln = [a[h] * l[h] + jnp.sum(p[h], axis=0, keepdims=True) for h in range(nh)]
        pv = [_dot(vt[(h % 2) * hd:(h % 2 + 1) * hd, :], _bf(p[h])) for h in range(nh)]
        for h in range(nh):
            rows = slice(h * hd, (h + 1) * hd)
            acc_ref[rows, :] = acc_ref[rows, :] * a[h] + pv[h]
        return tuple(n) + tuple(ln)

    init = (jnp.full((1, tq), -1e30, F32),) * nh + (jnp.zeros((1, tq), F32),) * nh
    fin = lax.fori_loop(0, nkv, body, init)
    for j in range(npair):
        inv = jnp.concatenate([jnp.broadcast_to(1.0 / fin[nh + 2 * j + t], (hd, tq)) for t in range(2)], axis=0)
        rows = slice(j * LANES, (j + 1) * LANES)
        o_ref[:, j * LANES:(j + 1) * LANES] = _bf((acc_ref[rows, :] * inv).T)


def _attention(qh, kbd, vt, batch, s_tot, c_len):
    n, qw = qh.shape
    tq = 256
    tk = ATT_TK
    nq = s_tot // tq
    kern = functools.partial(_attn_kernel, tq=tq, tk=tk, n_ctx_q=c_len // tq,
                             n_ctx_kv=c_len // tk, n_kv=s_tot // tk)
    return pl.pallas_call(
        kern,
        grid=(batch, nq),
        in_specs=[pl.BlockSpec((tq, qw), lambda b, i: (b * nq + i, 0)),
                  pl.BlockSpec((s_tot // tk, 2 * tk, LANES), lambda b, i: (b, 0, 0)),
                  pl.BlockSpec((s_tot // tk, LANES, tk), lambda b, i: (b, 0, 0))],
        out_specs=pl.BlockSpec((tq, qw), lambda b, i: (b * nq + i, 0)),
        out_shape=jax.ShapeDtypeStruct((n, qw), BF16),
        scratch_shapes=[pltpu.VMEM((qw, tq), F32)],
        compiler_params=_cparams("parallel", "parallel"),
        name="gqa_attention",
    )(qh, kbd, vt)


def _cmlp_kernel(uv_ref, lng_ref, lnb_ref, ws_ref, bs_ref, o_ref, *, nchunk, width):
    x = uv_ref[...].astype(F32)
    g = 0.5 * x * (1.0 + jnp.tanh(0.7978845608028654 * (x + 0.044715 * (x * x * x))))
    u = g[:, :width]
    v = g[:, width:]
    mu = jnp.mean(v, axis=-1, keepdims=True)
    dv = v - mu
    var = jnp.mean(dv * dv, axis=-1, keepdims=True)
    vn = _bf(dv * lax.rsqrt(var + EPS) * lng_ref[...] + lnb_ref[...])
    ngroups = width // CMLP_CHUNK
    for c in range(nchunk):
        r0 = c * CMLP_CHUNK
        for gi in range(ngroups):
            c0 = gi * CMLP_CHUNK
            s = _dot(ws_ref[gi], vn[r0:r0 + CMLP_CHUNK, c0:c0 + CMLP_CHUNK]) + bs_ref[gi]
            o_ref[r0:r0 + CMLP_CHUNK, c0:c0 + CMLP_CHUNK] = _bf(u[r0:r0 + CMLP_CHUNK, c0:c0 + CMLP_CHUNK] * s)


def _chunk_mlp(big, ln_g, ln_b, ws, bs_b):
    n = big.shape[0]
    width = ln_g.shape[1]
    tr = _pick_tile(n, (512, 256, 128))
    kern = functools.partial(_cmlp_kernel, nchunk=tr // CMLP_CHUNK, width=width)
    ng = ws.shape[0]
    return pl.pallas_call(
        kern,
        grid=(n // tr,),
        in_specs=[pl.BlockSpec((tr, 2 * width), lambda i: (i, COL_UV // (2 * width))),
                  pl.BlockSpec((1, width), lambda i: (0, 0)),
                  pl.BlockSpec((1, width), lambda i: (0, 0)),
                  pl.BlockSpec((ng, CMLP_CHUNK, CMLP_CHUNK), lambda i: (0, 0, 0)),
                  pl.BlockSpec((ng, CMLP_CHUNK, CMLP_CHUNK), lambda i: (0, 0, 0))],
        out_specs=pl.BlockSpec((tr, width), lambda i: (i, 0)),
        out_shape=jax.ShapeDtypeStruct((n, width), BF16),
        compiler_params=_cparams("parallel"),
        name="chunk_gmlp",
    )(big, ln_g, ln_b, ws, bs_b)


def _rwkv_prep_kernel(x_ref, xp_ref, xn_ref, lo_ref, conv_ref, w0_ref, w2_ref, a0_ref, a2_ref,
                      kk0_ref, kk1_ref, rk_ref,
                      v_o, bonus_o, at_f, rt_f, bt_f, kt_f, bb_f, kb_f, pl_f,
                      at_b, rt_b, bt_b, kt_b, bb_b, kb_b, pl_b, *, tm, width, blocks_per_seq, ctx_blocks):
    i = pl.program_id(0)
    j = i % blocks_per_seq
    is_first = jnp.logical_or(j == 0, j == ctx_blocks)
    is_last = jnp.logical_or(j == ctx_blocks - 1, j == blocks_per_seq - 1)
    row = lax.broadcasted_iota(jnp.int32, (tm, width), 0)
    gsum = _group_matrix(1.0)
    halo = xp_ref.shape[0]

    def conv(c):
        cs = slice(c * width, (c + 1) * width)
        x = x_ref[:, cs].astype(F32)
        prev_row = jnp.where(is_first, 0.0, xp_ref[halo - 1:halo, cs].astype(F32))
        next_row = jnp.where(is_last, 0.0, xn_ref[0:1, cs].astype(F32))
        xprev = jnp.where(row == 0, prev_row, pltpu.roll(x, 1, 0))
        xnext = jnp.where(row == tm - 1, next_row, pltpu.roll(x, tm - 1, 0))
        return xprev * conv_ref[0:1, cs] + x * conv_ref[1:2, cs] + xnext * conv_ref[2:3, cs]

    r = conv(0)
    k = conv(1)
    v = conv(2)
    v_o[...] = _bf(v)

    def group_sum(x):
        parts = [_dot_split(x[:, c * LANES:(c + 1) * LANES], gsum, 2) for c in range(width // LANES)]
        return jnp.concatenate(parts, axis=1)

    kk = k * kk0_ref[...]
    kk = kk * lax.rsqrt(group_sum(kk * kk) + 1e-12)
    bonus_o[...] = _bf(group_sum(r * k * rk_ref[...]) * v)

    lo = lo_ref[...].astype(F32)
    wd = w0_ref[...] + _dot(_bf(jnp.tanh(lo[:, :LANES])), w2_ref[...])
    ad = jax.nn.sigmoid(a0_ref[...] + _dot(_bf(lo[:, LANES:]), a2_ref[...]))
    lw = -float(np.exp(-0.5)) * jax.nn.sigmoid(wd)

    r2 = lax.broadcasted_iota(jnp.int32, (tm, tm), 0)
    c2 = lax.broadcasted_iota(jnp.int32, (tm, tm), 1)
    same = (r2 // SCAN_CHUNK) == (c2 // SCAN_CHUNK)
    tri_pre = jnp.where(jnp.logical_and(same, c2 <= r2), 1.0, 0.0).astype(BF16)
    tri_suf = jnp.where(jnp.logical_and(same, c2 >= r2), 1.0, 0.0).astype(BF16)
    nchunk = tm // SCAN_CHUNK

    outs = ((at_f, rt_f, bt_f, kt_f, bb_f, kb_f, pl_f), (at_b, rt_b, bt_b, kt_b, bb_b, kb_b, pl_b))
    for d in range(2):
        ds_ = slice(d * width, (d + 1) * width)
        lwd = lw[:, ds_]
        pre = _dot_split_left(tri_pre, lwd, 3)
        suf = _dot_split_left(tri_suf, lwd, 3)
        cin, rem = (pre, suf - lwd) if d == 0 else (suf, pre - lwd)
        cex = cin - lwd
        a_d = ad[:, ds_]
        b = kk * a_d
        kd = k * (1.0 + (a_d - 1.0) * kk1_ref[...])
        at_o, rt_o, bt_o, kt_o, bb_o, kb_o, pl_o = outs[d]
        at_o[...] = _bf(-kk * jnp.exp(cex))
        rt_o[...] = _bf(r * jnp.exp(cin))
        pinv = jnp.exp(-cin)
        bt_o[...] = _bf(b * pinv)
        kt_o[...] = _bf(kd * pinv)
        pend = jnp.exp(rem)
        bb_o[...] = _bf(b * pend)
        kb_o[...] = _bf(kd * pend)
        for c in range(nchunk):
            last = (c + 1) * SCAN_CHUNK - 1
            pl_o[c] = jnp.exp(pre[last:last + 1, :])


def _rwkv_prep(big, conv_w, w0, w2s, a0, a2s, kk0, kk1, rk, s_tot, c_len):
    n = big.shape[0]
    width = rk.shape[1]
    tm = MOD_ROWS
    halo = 16
    hb = tm // halo
    nhalo = n // halo
    nchunk = tm // SCAN_CHUNK
    kern = functools.partial(_rwkv_prep_kernel, tm=tm, width=width, blocks_per_seq=s_tot // tm,
                             ctx_blocks=c_len // tm)
    tok = pl.BlockSpec((tm, width), lambda i: (i, 0))
    plspec = pl.BlockSpec((nchunk, 1, width), lambda i: (i, 0, 0))
    tok_shape = jax.ShapeDtypeStruct((n, width), BF16)
    pl_shape = jax.ShapeDtypeStruct((n // SCAN_CHUNK, 1, width), F32)
    full = lambda a: pl.BlockSpec(a.shape, lambda i: (0,) * a.ndim)
    return pl.pallas_call(
        kern,
        grid=(n // tm,),
        in_specs=[pl.BlockSpec((tm, 3 * width), lambda i: (i, 0)),
                  pl.BlockSpec((halo, 3 * width), lambda i: (jnp.maximum(i * hb - 1, 0), 0)),
                  pl.BlockSpec((halo, 3 * width), lambda i: (jnp.minimum((i + 1) * hb, nhalo - 1), 0)),
                  pl.BlockSpec((tm, 2 * LANES), lambda i: (i, COL_WLOW // (2 * LANES))),
                  full(conv_w), full(w0), full(w2s), full(a0), full(a2s), full(kk0), full(kk1), full(rk)],
        out_specs=[tok, tok] + [tok] * 6 + [plspec] + [tok] * 6 + [plspec],
        out_shape=[tok_shape, tok_shape] + [tok_shape] * 6 + [pl_shape] + [tok_shape] * 6 + [pl_shape],
        compiler_params=_cparams("parallel"),
        name="rwkv_prepare",
    )(big, big, big, big, conv_w, w0, w2s, a0, a2s, kk0, kk1, rk)


def _scan_chunks(chains):
    L = SCAN_CHUNK
    lane = lax.broadcasted_iota(jnp.int32, (L, LANES), 1)
    m0 = _bf(jnp.where(lane < HEAD_DIM, 1.0, 0.0))
    m1 = _bf(jnp.where(lane < HEAD_DIM, 0.0, 1.0))

    def stack(x):
        return jnp.concatenate([x * m0, x * m1], axis=0)

    r2 = lax.broadcasted_iota(jnp.int32, (2 * L, 2 * L), 0)
    c2 = lax.broadcasted_iota(jnp.int32, (2 * L, 2 * L), 1)
    same = (r2 // L) == (c2 // L)
    tr, tc = r2 % L, c2 % L
    masks = {True: (jnp.logical_and(same, tc < tr), jnp.logical_and(same, tc <= tr)),
             False: (jnp.logical_and(same, tc > tr), jnp.logical_and(same, tc >= tr))}
    eye = r2 == c2
    fwd = [ch[9] for ch in chains]
    nc = range(len(chains))

    a_s = [stack(ch[0]) for ch in chains]
    r_s = [stack(ch[1]) for ch in chains]
    v_s = [stack(ch[6]) for ch in chains]
    big1 = [_dot_nt(jnp.concatenate([a_s[i], r_s[i]], axis=0),
                    jnp.concatenate([stack(chains[i][2]), stack(chains[i][3])], axis=0)) for i in nc]
    mab = [jnp.where(masks[fwd[i]][0], big1[i][:2 * L, :2 * L], 0.0) for i in nc]
    mak = [_bf(jnp.where(masks[fwd[i]][0], big1[i][:2 * L, 2 * L:], 0.0)) for i in nc]
    lhs_top = [_bf(jnp.where(jnp.concatenate([masks[fwd[i]][1]] * 2, axis=1), big1[i][2 * L:], 0.0)) for i in nc]
    mv = [_dot(mak[i], v_s[i]) for i in nc]
    x = [jnp.concatenate([a_s[i].astype(F32), mv[i]], axis=1) for i in nc]
    mp = mab
    steps = int(np.log2(L))
    for it in range(steps):
        if it < steps - 1:
            res = [_dot(_bf(mp[i]), _bf(jnp.concatenate([mp[i], x[i]], axis=1))) for i in nc]
            x = [x[i] + res[i][:, 2 * L:] for i in nc]
            mp = [res[i][:, :2 * L] for i in nc]
        else:
            res = [_dot(_bf(mp[i]), _bf(x[i])) for i in nc]
            x = [x[i] + res[i] for i in nc]
    rhs2 = [jnp.concatenate([_bf(x[i]), jnp.concatenate([jnp.zeros_like(v_s[i]), v_s[i]], axis=1)], axis=0)
            for i in nc]
    lhs_bot = [_bf(jnp.concatenate([stack(chains[i][4]), stack(chains[i][5])], axis=0).astype(F32).T)
               for i in nc]
    res2 = [_dot(jnp.concatenate([lhs_top[i], lhs_bot[i]], axis=0), rhs2[i]) for i in nc]
    lhs3 = [_bf(jnp.concatenate(
        [r_s[i].astype(F32) + res2[i][:2 * L, :LANES],
         res2[i][2 * L:, :LANES] + jnp.where(eye, jnp.broadcast_to(chains[i][7], (LANES, LANES)), 0.0)], axis=0))
        for i in nc]
    res3 = [_dot(lhs3[i], _bf(chains[i][8])) for i in nc]
    out = []
    for i in nc:
        ys = res3[i][:2 * L] + res2[i][:2 * L, LANES:]
        out.append((ys[:L] + ys[L:], res3[i][2 * L:] + res2[i][2 * L:, LANES:]))
    return out


def _rwkv_scan_kernel(v_f, at_f, rt_f, bt_f, kt_f, bb_f, kb_f, pl_f,
                      v_b, at_b, rt_b, bt_b, kt_b, bb_b, kb_b, pl_b,
                      yf_ref, yb_ref, z_ref, *, npairs):
    @pl.when(pl.program_id(1) == 0)
    def _():
        z_ref[...] = jnp.zeros_like(z_ref)

    dirs = ((v_f, at_f, rt_f, bt_f, kt_f, bb_f, kb_f, pl_f, yf_ref, True),
            (v_b, at_b, rt_b, bt_b, kt_b, bb_b, kb_b, pl_b, yb_ref, False))
    chains, dest = [], []
    for d, (v, at, rt, bt, kt, bb, kb, plr, y_ref, fwd) in enumerate(dirs):
        for p in range(npairs):
            cs = slice(p * LANES, (p + 1) * LANES)
            chains.append((at[:, cs], rt[:, cs], bt[:, cs], kt[:, cs], bb[:, cs], kb[:, cs],
                           v[:, cs], plr[0][:, cs], z_ref[d, p], fwd))
            dest.append((y_ref, cs, d, p))
    for (y_ref, cs, d, p), (y, znew) in zip(dest, _scan_chunks(chains)):
        y_ref[:, cs] = y
        z_ref[d, p] = znew


def _rwkv_scan(prep, batch, s_tot, c_len):
    (v, _bonus, at_f, rt_f, bt_f, kt_f, bb_f, kb_f, pl_f, at_b, rt_b, bt_b, kt_b, bb_b, kb_b, pl_b) = prep
    n, width = v.shape
    L = SCAN_CHUNK
    nch = s_tot // L
    ncc = c_len // L
    npairs = width // LANES

    def fmap(b, c):
        return (b * nch + c, 0)

    def bmap(b, c):
        return (b * nch + jnp.where(c < ncc, ncc - 1 - c, nch - 1 - (c - ncc)), 0)

    def tok(m):
        return pl.BlockSpec((L, width), m)

    def pls(m):
        return pl.BlockSpec((1, 1, width), lambda b, c: m(b, c) + (0,))

    kern = functools.partial(_rwkv_scan_kernel, npairs=npairs)
    return pl.pallas_call(
        kern,
        grid=(batch, nch),
        in_specs=[tok(fmap)] * 7 + [pls(fmap)] + [tok(bmap)] * 7 + [pls(bmap)],
        out_specs=[tok(fmap), tok(bmap)],
        out_shape=[jax.ShapeDtypeStruct((n, width), F32)] * 2,
        scratch_shapes=[pltpu.VMEM((2, npairs, LANES, LANES), F32)],
        compiler_params=_cparams("parallel", "arbitrary"),
        name="rwkv_scan",
    )(v, at_f, rt_f, bt_f, kt_f, bb_f, kb_f, pl_f, v, at_b, rt_b, bt_b, kt_b, bb_b, kb_b, pl_b)


def _rwkv_readout_kernel(yf_ref, yb_ref, bonus_ref, gl_ref, g2_ref, lg_ref, lb_ref, o_ref):
    avg = _group_matrix(1.0 / HEAD_DIM)
    gate = _dot(_bf(jax.nn.sigmoid(gl_ref[...].astype(F32))), g2_ref[...])
    for c in range(o_ref.shape[1] // LANES):
        cs = slice(c * LANES, (c + 1) * LANES)
        y = yf_ref[:, cs] + yb_ref[:, cs]
        mu = _dot_split(y, avg, 2)
        dy = y - mu
        var = _dot_split(dy * dy, avg, 2)
        yn = dy * lax.rsqrt(var + LNX_EPS) * lg_ref[:, cs] + lb_ref[:, cs]
        o_ref[:, cs] = _bf((yn + bonus_ref[:, cs].astype(F32)) * gate[:, cs])


def _rwkv_readout(yf, yb, bonus, big, g2, lnx_g, lnx_b):
    n, width = yf.shape
    tm = _pick_tile(n, (512, 256))
    tok = pl.BlockSpec((tm, width), lambda i: (i, 0))
    full = lambda a: pl.BlockSpec(a.shape, lambda i: (0,) * a.ndim)
    return pl.pallas_call(
        _rwkv_readout_kernel,
        grid=(n // tm,),
        in_specs=[tok, tok, tok, pl.BlockSpec((tm, LANES), lambda i: (i, COL_GLOW // LANES)),
                  full(g2), full(lnx_g), full(lnx_b)],
        out_specs=tok,
        out_shape=jax.ShapeDtypeStruct((n, width), BF16),
        compiler_params=_cparams("parallel"),
        name="rwkv_readout",
    )(yf, yb, bonus, big, g2, lnx_g, lnx_b)


def _merge_kernel(oa_ref, ob_ref, oc_ref, gt_ref, x_ref, wb_ref, wo_ref, ng_ref, mod_ref, o_ref, *, nsub, d):
    y = None
    for br, ref in enumerate((oa_ref, ob_ref, oc_ref)):
        g = jax.nn.sigmoid(gt_ref[:, br * d:(br + 1) * d].astype(F32))
        t = g * _dot(ref[...], wb_ref[br])
        y = t if y is None else y + t
    o = _dot(_bf(y), wo_ref[...])
    o_ref[...] = x_ref[...] + _mod_rows(mod_ref, nsub, 2, d) * _rms(o, ng_ref[...])


def _merge(oa, ob, oc, big, x, wb, wo, ng, modblk):
    n, d = x.shape
    width = oa.shape[1]
    tm = _pick_tile(n, (512, 256))
    nsub = tm // MOD_ROWS
    kern = functools.partial(_merge_kernel, nsub=nsub, d=d)
    br = pl.BlockSpec((tm, width), lambda i: (i, 0))
    return pl.pallas_call(
        kern,
        grid=(n // tm,),
        in_specs=[br, br, br,
                  pl.BlockSpec((tm, 3 * d), lambda i: (i, COL_GATES // (3 * d))),
                  pl.BlockSpec((tm, d), lambda i: (i, 0)),
                  pl.BlockSpec(wb.shape, lambda i: (0, 0, 0)),
                  pl.BlockSpec(wo.shape, lambda i: (0, 0)),
                  pl.BlockSpec((1, d), lambda i: (0, 0)),
                  pl.BlockSpec((nsub, 1, 6 * d), lambda i: (i, 0, 0))],
        out_specs=pl.BlockSpec((tm, d), lambda i: (i, 0)),
        out_shape=jax.ShapeDtypeStruct((n, d), F32),
        compiler_params=_cparams("parallel"),
        name="merge_out_proj",
    )(oa, ob, oc, big, x, wb, wo, ng, modblk)


def _ffn_kernel(x_ref, g_ref, mod_ref, wg_ref, wu_ref, wd_ref, ng_ref, o_ref, h_ref, acc_ref, *, nsub, d):
    j = pl.program_id(1)

    @pl.when(j == 0)
    def _():
        y = _rms(x_ref[...], g_ref[...])
        h_ref[...] = _bf(y * (1.0 + _mod_rows(mod_ref, nsub, 4, d)) + _mod_rows(mod_ref, nsub, 3, d))
        acc_ref[...] = jnp.zeros_like(acc_ref)

    h = h_ref[...]
    g = _dot(h, wg_ref[...])
    u = _dot(h, wu_ref[...])
    acc_ref[...] += _dot(_bf(g * jax.nn.sigmoid(g) * u), wd_ref[...])

    @pl.when(j == pl.num_programs(1) - 1)
    def _():
        o_ref[...] = x_ref[...] + _mod_rows(mod_ref, nsub, 5, d) * _rms(acc_ref[...], ng_ref[...])


def _dense_ffn(x, gain_in, gain_out, modblk, w_gu, w_down):
    n, d = x.shape
    f = w_down.shape[0]
    tm = _pick_tile(n, (1024, 512, 256))
    tf = _pick_tile(f, (1408, 1024, 512, 256, 128))
    nf = f // tf
    nsub = tm // MOD_ROWS
    kern = functools.partial(_ffn_kernel, nsub=nsub, d=d)
    return pl.pallas_call(
        kern,
        grid=(n // tm, nf),
        in_specs=[pl.BlockSpec((tm, d), lambda i, j: (i, 0)),
                  pl.BlockSpec((1, d), lambda i, j: (0, 0)),
                  pl.BlockSpec((nsub, 1, 6 * d), lambda i, j: (i, 0, 0)),
                  pl.BlockSpec((d, tf), lambda i, j: (0, j)),
                  pl.BlockSpec((d, tf), lambda i, j: (0, j + nf)),
                  pl.BlockSpec((tf, d), lambda i, j: (j, 0)),
                  pl.BlockSpec((1, d), lambda i, j: (0, 0))],
        out_specs=pl.BlockSpec((tm, d), lambda i, j: (i, 0)),
        out_shape=jax.ShapeDtypeStruct((n, d), F32),
        scratch_shapes=[pltpu.VMEM((tm, d), BF16), pltpu.VMEM((tm, d), F32)],
        compiler_params=_cparams("parallel", "arbitrary"),
        name="dense_swiglu_ffn",
    )(x, gain_in, modblk, w_gu, w_gu, w_down, gain_out)


def _router_kernel(x_ref, g_ref, mod_ref, wr_ref, h_ref, comb_ref, rank_ref, combt_ref, rankt_ref, cnt_ref,
                   *, nsub, d):
    lane = lax.broadcasted_iota(jnp.int32, (MOD_ROWS, LANES), 1)
    r2 = lax.broadcasted_iota(jnp.int32, (MOD_ROWS, MOD_ROWS), 0)
    c2 = lax.broadcasted_iota(jnp.int32, (MOD_ROWS, MOD_ROWS), 1)
    tri = jnp.where(c2 < r2, 1.0, 0.0).astype(BF16)
    ninf = jnp.float32(-jnp.inf)
    running = jnp.zeros((1, LANES), F32)
    for s in range(nsub):
        rows = slice(s * MOD_ROWS, (s + 1) * MOD_ROWS)
        m = mod_ref[s]
        h = _rms(x_ref[rows, :], g_ref[...]) * (1.0 + m[:, 4 * d:5 * d]) + m[:, 3 * d:4 * d]
        h_ref[rows, :] = _bf(h)
        logits = jnp.dot(h, wr_ref[...], precision=HIGHEST, preferred_element_type=F32)
        logits = jnp.where(lane < N_EXPERTS, logits, ninf)
        m1 = jnp.max(logits, axis=-1, keepdims=True)
        i1 = jnp.min(jnp.where(logits == m1, lane, LANES), axis=-1, keepdims=True)
        rest = jnp.where(lane == i1, ninf, logits)
        m2 = jnp.max(rest, axis=-1, keepdims=True)
        i2 = jnp.min(jnp.where(rest == m2, lane, LANES), axis=-1, keepdims=True)
        e2 = jnp.exp(m2 - m1)
        w1 = 1.0 / (1.0 + e2)
        comb = jnp.where(lane == i1, w1, 0.0) + jnp.where(lane == i2, e2 * w1, 0.0)
        ind = jnp.where(comb > 0.0, 1.0, 0.0)
        rank = _dot(tri, _bf(ind)) + running
        running = running + jnp.sum(ind, axis=0, keepdims=True)
        comb_ref[rows, :] = comb
        rank_ref[rows, :] = rank
        combt_ref[:, rows] = comb.T[:N_EXPERTS, :]
        rankt_ref[:, rows] = rank.T[:N_EXPERTS, :]
    cnt_ref[0] = running


def _router(x, gain_in, modblk, w_router_pad, tm):
    n, d = x.shape
    nsub = tm // MOD_ROWS
    nt = n // tm
    kern = functools.partial(_router_kernel, nsub=nsub, d=d)
    tokm = pl.BlockSpec((tm, LANES), lambda i: (i, 0))
    expm = pl.BlockSpec((N_EXPERTS, tm), lambda i: (0, i))
    return pl.pallas_call(
        kern,
        grid=(nt,),
        in_specs=[pl.BlockSpec((tm, d), lambda i: (i, 0)),
                  pl.BlockSpec((1, d), lambda i: (0, 0)),
                  pl.BlockSpec((nsub, 1, 6 * d), lambda i: (i, 0, 0)),
                  pl.BlockSpec((d, LANES), lambda i: (0, 0))],
        out_specs=[pl.BlockSpec((tm, d), lambda i: (i, 0)), tokm, tokm, expm, expm,
                   pl.BlockSpec((1, 1, LANES), lambda i: (i, 0, 0))],
        out_shape=[jax.ShapeDtypeStruct((n, d), BF16),
                   jax.ShapeDtypeStruct((n, LANES), F32), jax.ShapeDtypeStruct((n, LANES), F32),
                   jax.ShapeDtypeStruct((N_EXPERTS, n), F32), jax.ShapeDtypeStruct((N_EXPERTS, n), F32),
                   jax.ShapeDtypeStruct((nt, 1, LANES), F32)],
        compiler_params=_cparams("parallel"),
        name="moe_router",
    )(x, gain_in, modblk, w_router_pad)


def _moe_kernel(cnt_ref, h_ref, comb_ref, rank_ref, combt_ref, rankt_ref, wg_ref, wu_ref, wd_ref, o_ref,
                xe_ref, y_ref, acc_ref, *, tm, rb):
    i = pl.program_id(0)
    e = pl.program_id(1)
    j = pl.program_id(2)
    nf = pl.num_programs(2)
    nblk = (cnt_ref[i * N_EXPERTS + e] + rb - 1) // rb

    @pl.when(jnp.logical_and(e == 0, j == 0))
    def _():
        acc_ref[...] = jnp.zeros_like(acc_ref)

    @pl.when(j == 0)
    def _():
        key = jnp.where(combt_ref[pl.ds(e, 1), :] > 0.0, rankt_ref[pl.ds(e, 1), :], -1.0)

        def gather(b, carry):
            r0 = pl.multiple_of(b * rb, rb)
            want = (r0 + lax.broadcasted_iota(jnp.int32, (rb, tm), 0)).astype(F32)
            sel = jnp.where(key == want, 1.0, 0.0).astype(BF16)
            xe_ref[pl.ds(r0, rb), :] = _bf(_dot(sel, h_ref[...]))
            return carry

        lax.fori_loop(0, nblk, gather, 0)

    def expert(b, carry):
        rows = pl.ds(pl.multiple_of(b * rb, rb), rb)
        xb = xe_ref[rows, :]
        g = _dot(xb, wg_ref[0])
        u = _dot(xb, wu_ref[0])
        part = _dot(_bf(g * jax.nn.sigmoid(g) * u), wd_ref[0])

        @pl.when(j == 0)
        def _():
            y_ref[rows, :] = part

        @pl.when(j > 0)
        def _():
            y_ref[rows, :] += part

        return carry

    lax.fori_loop(0, nblk, expert, 0)

    @pl.when(j == nf - 1)
    def _():
        lane = lax.broadcasted_iota(jnp.int32, (tm, LANES), 1)
        rank_col = jnp.sum(jnp.where(lane == e, rank_ref[...], 0.0), axis=1, keepdims=True)
        w_col = jnp.sum(jnp.where(lane == e, comb_ref[...], 0.0), axis=1, keepdims=True)

        def scatter(b, carry):
            r0 = pl.multiple_of(b * rb, rb)
            want = (r0 + lax.broadcasted_iota(jnp.int32, (tm, rb), 1)).astype(F32)
            selw = _bf(jnp.where(rank_col == want, w_col, 0.0))
            acc_ref[...] += _dot(selw, _bf(y_ref[pl.ds(r0, rb), :]))
            return carry

        lax.fori_loop(0, nblk, scatter, 0)

    @pl.when(jnp.logical_and(e == pl.num_programs(1) - 1, j == nf - 1))
    def _():
        o_ref[...] = _bf(acc_ref[...])


def _moe_ffn(hb, comb, rank, combt, rankt, counts, w_gu, w_down, tm):
    n, d = hb.shape
    ne, f, _ = w_down.shape
    tf = _pick_tile(f, (512, 256, 128))
    nf = f // tf
    rb = MOE_ROW_BLOCK
    kern = functools.partial(_moe_kernel, tm=tm, rb=rb)
    tokm = pl.BlockSpec((tm, LANES), lambda i, e, j, c: (i, 0))
    expm = pl.BlockSpec((N_EXPERTS, tm), lambda i, e, j, c: (0, i))
    grid_spec = pltpu.PrefetchScalarGridSpec(
        num_scalar_prefetch=1,
        grid=(n // tm, ne, nf),
        in_specs=[pl.BlockSpec((tm, d), lambda i, e, j, c: (i, 0)), tokm, tokm, expm, expm,
                  pl.BlockSpec((1, d, tf), lambda i, e, j, c: (e, 0, j)),
                  pl.BlockSpec((1, d, tf), lambda i, e, j, c: (e, 0, j + nf)),
                  pl.BlockSpec((1, tf, d), lambda i, e, j, c: (e, j, 0))],
        out_specs=pl.BlockSpec((tm, d), lambda i, e, j, c: (i, 0)),
        scratch_shapes=[pltpu.VMEM((tm, d), BF16), pltpu.VMEM((tm, d), F32), pltpu.VMEM((tm, d), F32)])
    return pl.pallas_call(
        kern,
        grid_spec=grid_spec,
        out_shape=jax.ShapeDtypeStruct((n, d), BF16),
        compiler_params=_cparams("parallel", "arbitrary", "arbitrary"),
        name="moe_swiglu_ffn",
    )(counts, hb, comb, rank, combt, rankt, w_gu, w_gu, w_down)


def _residual_kernel(x_ref, y_ref, ng_ref, mod_ref, o_ref, *, nsub, d):
    o_ref[...] = x_ref[...] + _mod_rows(mod_ref, nsub, 5, d) * _rms(y_ref[...].astype(F32), ng_ref[...])


def _gated_residual(x, y, gain_out, modblk):
    n, d = x.shape
    tm = _pick_tile(n, (512, 256))
    nsub = tm // MOD_ROWS
    tok = pl.BlockSpec((tm, d), lambda i: (i, 0))
    return pl.pallas_call(
        functools.partial(_residual_kernel, nsub=nsub, d=d),
        grid=(n // tm,),
        in_specs=[tok, tok, pl.BlockSpec((1, d), lambda i: (0, 0)),
                  pl.BlockSpec((nsub, 1, 6 * d), lambda i: (i, 0, 0))],
        out_specs=tok,
        out_shape=jax.ShapeDtypeStruct((n, d), F32),
        compiler_params=_cparams("parallel"),
        name="moe_gated_residual",
    )(x, y, gain_out, modblk)


def _rope_tables(t_len, c_len):
    pairs = HEAD_DIM // 4
    rows = t_len // GRID_W
    row = jnp.repeat(jnp.arange(rows, dtype=F32), GRID_W)
    col = jnp.tile(jnp.arange(GRID_W, dtype=F32), rows)
    freqs = ROPE_BASE ** (-jnp.arange(pairs, dtype=F32) / pairs)
    ar = row[:, None] * freqs
    ac = col[:, None] * freqs
    cos = jnp.concatenate([jnp.cos(ar), jnp.cos(ar), jnp.cos(ac), jnp.cos(ac)], axis=1)
    sin = jnp.concatenate([-jnp.sin(ar), jnp.sin(ar), -jnp.sin(ac), jnp.sin(ac)], axis=1)
    cos = jnp.concatenate([jnp.ones((c_len, HEAD_DIM), F32), cos], axis=0)
    sin = jnp.concatenate([jnp.zeros((c_len, HEAD_DIM), F32), sin], axis=0)
    return jnp.tile(cos, (1, 2)), jnp.tile(sin, (1, 2))


def _block_diag2(w):
    z = jnp.zeros_like(w[0])
    return jnp.concatenate([jnp.concatenate([w[0], z], axis=1), jnp.concatenate([z, w[1]], axis=1)], axis=0)


def kernel(x, c, ctx, c_ctx, w_mod, b_mod, norm_gain, w_in, qk_gain, rwkv_conv, decay_w0, decay_w2, iclr_a0, iclr_a2, key_k, bonus_rk, gate_g2, lnx_gain, lnx_bias, cmlp_ln_gain, cmlp_ln_bias, cmlp_ws, cmlp_bs, w_branch, w_out, ffn_w_gu, ffn_w_down, moe_router, moe_w_gu, moe_w_down):
    batch, t_len, d = x.shape
    c_len = ctx.shape[1]
    depth = w_mod.shape[0]
    s_tot = c_len + t_len
    n = batch * s_tot
    assert c_len % MOD_ROWS == 0 and t_len % MOD_ROWS == 0 and d % LANES == 0
    width = bonus_rk.shape[1] * bonus_rk.shape[2]
    nheads = width // HEAD_DIM

    xs = jnp.concatenate([ctx, x], axis=1).reshape(n, d)

    mod_rows = 8 * ((batch + 1 + 7) // 8)
    cvec = jnp.zeros((mod_rows, d), F32).at[0].set(c_ctx).at[1:batch + 1].set(c)
    mods = _modulation(cvec, w_mod, b_mod)
    mod_ctx = jnp.broadcast_to(mods[:, 0:1, None, :], (depth, batch, c_len // MOD_ROWS, 6 * d))
    mod_lat = jnp.broadcast_to(mods[:, 1:batch + 1, None, :], (depth, batch, t_len // MOD_ROWS, 6 * d))
    modblk_all = jnp.concatenate([mod_ctx, mod_lat], axis=2).reshape(depth, n // MOD_ROWS, 1, 6 * d)

    order = np.array(Q_HEAD_ORDER)
    nl = depth
    wq = w_in[:, :, 2048:2560].reshape(nl, d, nheads, HEAD_DIM)[:, :, order].reshape(nl, d, width)
    w_in_p = jnp.concatenate([
        w_in[:, :, 256:1792], wq, w_in[:, :, 2688:3712], w_in[:, :, 3712:6784],
        w_in[:, :, 0:128], w_in[:, :, 128:256], w_in[:, :, 1792:1920], w_in[:, :, 1920:2048],
        w_in[:, :, 2560:2688], jnp.zeros((nl, d, IN_PAD - 6784), F32)], axis=2).astype(BF16)
    wb = w_branch.astype(BF16)
    wb0 = wb[:, 0].reshape(nl, nheads, HEAD_DIM, d)[:, order].reshape(nl, width, d)
    wb = jnp.concatenate([wb0[:, None], wb[:, 1:]], axis=1)
    wo = w_out.astype(BF16)
    cos, sin = _rope_tables(t_len, c_len)
    qg = jnp.tile(qk_gain[:, 0], (1, 2))[:, None, :]
    kg = jnp.tile(qk_gain[:, 1], (1, 2))[:, None, :]
    ws_b = cmlp_ws.astype(BF16)
    bs_b = jnp.broadcast_to(cmlp_bs[..., None], cmlp_bs.shape + (CMLP_CHUNK,))
    w2s = jnp.stack([_block_diag2(decay_w2[l]) for l in range(nl)]).astype(BF16)
    a2s = jnp.stack([_block_diag2(iclr_a2[l]) for l in range(nl)]).astype(BF16)
    w0 = decay_w0.reshape(nl, 1, 2 * width)
    a0 = iclr_a0.reshape(nl, 1, 2 * width)
    rk = bonus_rk.reshape(nl, 1, width)
    g2 = gate_g2.astype(BF16)
    ffn_gu = ffn_w_gu.astype(BF16)
    ffn_dn = ffn_w_down.astype(BF16)
    moe_gu = moe_w_gu.astype(BF16)
    moe_dn = moe_w_down.astype(BF16)
    router_pad = jnp.pad(moe_router, ((0, 0), (0, 0), (0, LANES - moe_router.shape[2])))

    for l in range(depth):
        modblk = modblk_all[l]
        ng = norm_gain[l]
        big = _norm_mod_matmul(xs, ng[0:1], modblk, w_in_p[l], 0, 1)
        qh, kbd, vt = _qk_prep(big, qg[l], kg[l], cos, sin, s_tot)
        oa = _attention(qh, kbd, vt, batch, s_tot, c_len)
        prep = _rwkv_prep(big, rwkv_conv[l], w0[l], w2s[l], a0[l], a2s[l],
                          key_k[l, 0:1], key_k[l, 1:2], rk[l], s_tot, c_len)
        yf, yb = _rwkv_scan(prep, batch, s_tot, c_len)
        ob = _rwkv_readout(yf, yb, prep[1], big, g2[l], lnx_gain[l][None], lnx_bias[l][None])
        oc = _chunk_mlp(big, cmlp_ln_gain[l][None], cmlp_ln_bias[l][None], ws_b[l], bs_b[l])
        xs = _merge(oa, ob, oc, big, xs, wb[l], wo[l], ng[1:2], modblk)
        if l % 2 == 0:
            xs = _dense_ffn(xs, ng[2:3], ng[3:4], modblk, ffn_gu[l // 2], ffn_dn[l // 2])
        else:
            tmoe = _pick_tile(n, (2048, 1024, 512, 256))
            hb, comb, rank, combt, rankt, cnt = _router(xs, ng[2:3], modblk, router_pad[l // 2], tmoe)
            counts = cnt[:, 0, :N_EXPERTS].astype(jnp.int32).reshape(-1)
            y = _moe_ffn(hb, comb, rank, combt, rankt, counts, moe_gu[l // 2], moe_dn[l // 2], tmoe)
            xs = _gated_residual(xs, y, ng[3:4], modblk)
    return xs.reshape(batch, s_tot, d)[:, c_len:, :]
```

```python
import functools

import jax
import jax.numpy as jnp
import numpy as np
from jax import lax
from jax.experimental import pallas as pl
from jax.experimental.pallas import tpu as pltpu

F32 = jnp.float32
BF16 = jnp.bfloat16
HIGHEST = lax.Precision.HIGHEST

EPS = 1e-6
LNX_EPS = 64e-5
HEAD_DIM = 64
ROPE_BASE = 10000.0
GRID_W = 64
LANES = 128
MOD_ROWS = 256
SCAN_CHUNK = 64
CMLP_CHUNK = 128
ATT_TK = 256
ATT_VROWS = HEAD_DIM + 16
ATT_GROUP = 1
N_EXPERTS = 8
MOE_ROW_BLOCK = 256
VMEM_LIMIT = 56 * 1024 * 1024

COL_RKV, COL_Q, COL_UV, COL_GATES = 0, 1536, 2048, 3072
COL_K, COL_V, COL_WLOW, COL_ALOW, COL_GLOW = 6144, 6272, 6400, 6528, 6656
IN_PAD = 7168
Q_HEAD_ORDER = (0, 4, 1, 5, 2, 6, 3, 7)


def _cparams(*sem):
    return pltpu.CompilerParams(dimension_semantics=sem, vmem_limit_bytes=VMEM_LIMIT)


def _dot(a, b):
    return jnp.dot(a, b, preferred_element_type=F32)


def _dot_nt(a, b):
    return lax.dot_general(a, b, (((1,), (1,)), ((), ())), preferred_element_type=F32)


def _bf(x):
    return x.astype(BF16)


def _dot_split(a, b_exact, terms):
    acc = None
    rem = a
    for _ in range(terms):
        piece = _bf(rem)
        rem = rem - piece.astype(F32)
        part = _dot(piece, b_exact)
        acc = part if acc is None else acc + part
    return acc


def _dot_split_left(a_exact, b, terms):
    acc = None
    rem = b
    for _ in range(terms):
        piece = _bf(rem)
        rem = rem - piece.astype(F32)
        part = _dot(a_exact, piece)
        acc = part if acc is None else acc + part
    return acc


def _group_matrix(scale):
    r = lax.broadcasted_iota(jnp.int32, (LANES, LANES), 0) // HEAD_DIM
    c = lax.broadcasted_iota(jnp.int32, (LANES, LANES), 1) // HEAD_DIM
    return jnp.where(r == c, scale, 0.0).astype(BF16)


def _pick_tile(n, candidates):
    for t in candidates:
        if n % t == 0:
            return t
    raise ValueError(f"no tile in {candidates} divides {n}")


def _mod_rows(mod_ref, nsub, idx, d):
    parts = [jnp.broadcast_to(mod_ref[s][:, idx * d:(idx + 1) * d], (MOD_ROWS, d)) for s in range(nsub)]
    return parts[0] if nsub == 1 else jnp.concatenate(parts, axis=0)


def _rms(x, g):
    return x * lax.rsqrt(jnp.mean(x * x, axis=-1, keepdims=True) + EPS) * g


def _mod_kernel(c_ref, w_ref, b_ref, o_ref):
    cv = c_ref[...]
    s = cv * jax.nn.sigmoid(cv)
    o_ref[0] = jnp.dot(s, w_ref[0], precision=HIGHEST, preferred_element_type=F32) + b_ref[0]


def _modulation(cvec, w_mod, b_mod):
    nl, d, d6 = w_mod.shape
    rows = cvec.shape[0]
    tn = 1024
    return pl.pallas_call(
        _mod_kernel,
        grid=(nl, d6 // tn),
        in_specs=[pl.BlockSpec((rows, d), lambda l, j: (0, 0)),
                  pl.BlockSpec((1, d, tn), lambda l, j: (l, 0, j)),
                  pl.BlockSpec((1, 1, tn), lambda l, j: (l, 0, j))],
        out_specs=pl.BlockSpec((1, rows, tn), lambda l, j: (l, 0, j)),
        out_shape=jax.ShapeDtypeStruct((nl, rows, d6), F32),
        compiler_params=_cparams("parallel", "parallel"),
        name="modulation",
    )(cvec, w_mod, b_mod.reshape(nl, 1, d6))


def _nmm_kernel(x_ref, g_ref, mod_ref, w_ref, o_ref, h_ref, *, nsub, d, shift_idx, scale_idx):
    @pl.when(pl.program_id(1) == 0)
    def _():
        y = _rms(x_ref[...], g_ref[...])
        sc = _mod_rows(mod_ref, nsub, scale_idx, d)
        sh = _mod_rows(mod_ref, nsub, shift_idx, d)
        h_ref[...] = _bf(y * (1.0 + sc) + sh)

    o_ref[...] = _bf(_dot(h_ref[...], w_ref[...]))


def _norm_mod_matmul(x, gain, modblk, w, shift_idx, scale_idx):
    n, d = x.shape
    nout = w.shape[1]
    tm = _pick_tile(n, (1024, 512, 256))
    tn = 1024
    nsub = tm // MOD_ROWS
    kern = functools.partial(_nmm_kernel, nsub=nsub, d=d, shift_idx=shift_idx, scale_idx=scale_idx)
    return pl.pallas_call(
        kern,
        grid=(n // tm, nout // tn),
        in_specs=[pl.BlockSpec((tm, d), lambda i, j: (i, 0)),
                  pl.BlockSpec((1, d), lambda i, j: (0, 0)),
                  pl.BlockSpec((nsub, 1, 6 * d), lambda i, j: (i, 0, 0)),
                  pl.BlockSpec((d, tn), lambda i, j: (0, j))],
        out_specs=pl.BlockSpec((tm, tn), lambda i, j: (i, j)),
        out_shape=jax.ShapeDtypeStruct((n, nout), BF16),
        scratch_shapes=[pltpu.VMEM((tm, d), BF16)],
        compiler_params=_cparams("parallel", "arbitrary"),
        name="norm_mod_in_proj",
    )(x, gain, modblk, w)


def _qkprep_kernel(q_ref, k_ref, v_ref, qg_ref, kg_ref, cos_ref, sin_ref, qo_ref, ko_ref, vo_ref):
    cos = cos_ref[...]
    sin = sin_ref[...]
    avg = _group_matrix(1.0 / HEAD_DIM)
    lane = lax.broadcasted_iota(jnp.int32, cos.shape, 1)
    first = (lane % 32) < 16
    left = lane < HEAD_DIM

    def norm_rope(x, g):
        ms = _dot_split(x * x, avg, 2)
        xn = x * lax.rsqrt(ms + EPS) * g
        partner = jnp.where(first, pltpu.roll(xn, LANES - 16, 1), pltpu.roll(xn, 16, 1))
        return xn * cos + partner * sin

    qscale = (HEAD_DIM ** -0.5) * float(np.log2(np.e))
    for j in range(q_ref.shape[1] // LANES):
        q = q_ref[:, j * LANES:(j + 1) * LANES].astype(F32)
        qo_ref[:, j * LANES:(j + 1) * LANES] = _bf(norm_rope(q, qg_ref[...]) * qscale)
    k = norm_rope(k_ref[...].astype(F32), kg_ref[...])
    zero = jnp.zeros_like(k)
    k0 = _bf(jnp.where(left, k, zero))
    k1 = _bf(jnp.where(left, zero, k))
    v = v_ref[...].astype(F32)
    ones = jnp.ones((ATT_VROWS - HEAD_DIM, ATT_TK), BF16)
    for c in range(vo_ref.shape[0]):
        rows = slice(c * ATT_TK, (c + 1) * ATT_TK)
        ko_ref[c, :ATT_TK, :] = k0[rows]
        ko_ref[c, ATT_TK:, :] = k1[rows]
        vt = _bf(v[rows, :].T)
        for t in range(2):
            vo_ref[c, t, :HEAD_DIM, :] = vt[t * HEAD_DIM:(t + 1) * HEAD_DIM]
            vo_ref[c, t, HEAD_DIM:, :] = ones


def _qk_prep(big, qg, kg, cos, sin, s_tot):
    n = big.shape[0]
    tm = MOD_ROWS
    npos = s_tot // tm
    qw = 512
    vchunks = tm // ATT_TK
    return pl.pallas_call(
        _qkprep_kernel,
        grid=(n // tm,),
        in_specs=[pl.BlockSpec((tm, qw), lambda i: (i, COL_Q // qw)),
                  pl.BlockSpec((tm, LANES), lambda i: (i, COL_K // LANES)),
                  pl.BlockSpec((tm, LANES), lambda i: (i, COL_V // LANES)),
                  pl.BlockSpec((1, LANES), lambda i: (0, 0)),
                  pl.BlockSpec((1, LANES), lambda i: (0, 0)),
                  pl.BlockSpec((tm, LANES), lambda i: (i % npos, 0)),
                  pl.BlockSpec((tm, LANES), lambda i: (i % npos, 0))],
        out_specs=[pl.BlockSpec((tm, qw), lambda i: (i, 0)),
                   pl.BlockSpec((vchunks, 2 * ATT_TK, LANES), lambda i: (i, 0, 0)),
                   pl.BlockSpec((vchunks, 2, ATT_VROWS, ATT_TK), lambda i: (i, 0, 0, 0))],
        out_shape=[jax.ShapeDtypeStruct((n, qw), BF16),
                   jax.ShapeDtypeStruct((n // ATT_TK, 2 * ATT_TK, LANES), BF16),
                   jax.ShapeDtypeStruct((n // ATT_TK, 2, ATT_VROWS, ATT_TK), BF16)],
        compiler_params=_cparams("parallel"),
        name="qk_norm_rope",
    )(big, big, big, qg, kg, cos, sin)


def _attn_kernel(q_ref, k_ref, vt_ref, o_ref, acc_ref, sa_ref, sb_ref, *, tq, tk, n_ctx_q, n_ctx_kv, n_kv):
    i = pl.program_id(1)
    nkv = jnp.where(i < n_ctx_q, n_ctx_kv, n_kv)
    hd = HEAD_DIM
    npair = q_ref.shape[1] // LANES
    nh = 2 * npair
    qs = [q_ref[:, j * LANES:(j + 1) * LANES] for j in range(npair)]
    vr = ATT_VROWS
    acc_ref[...] = jnp.zeros_like(acc_ref)

    def scores_to(dst_ref, c):
        kb = k_ref[c]
        for j in range(npair):
            sj = _dot_nt(kb, qs[j])
            dst_ref[2 * j] = sj[:tk]
            dst_ref[2 * j + 1] = sj[tk:]

    def consume(src_ref, c, m):
        new_m = []
        for h in range(nh):
            s = src_ref[h]
            n = jnp.maximum(m[h], jnp.max(s, axis=0, keepdims=True))
            p = _bf(jnp.exp2(s - n))
            rows = slice(h * vr, (h + 1) * vr)
            acc_ref[rows, :] = acc_ref[rows, :] * jnp.exp2(m[h] - n) + _dot(vt_ref[c, h % 2], p)
            new_m.append(n)
        return tuple(new_m)

    def body(u, m):
        c = 2 * u
        scores_to(sb_ref, c + 1)
        m = consume(sa_ref, c, m)
        scores_to(sa_ref, c + 2)
        return consume(sb_ref, c + 1, m)

    scores_to(sa_ref, 0)
    m = lax.fori_loop(0, (nkv - 1) // 2, body, (jnp.full((1, tq), -1e30, F32),) * nh)
    consume(sa_ref, nkv - 1, m)
    for j in range(npair):
        o = [acc_ref[h * vr:h * vr + hd, :] * (1.0 / acc_ref[h * vr + hd:h * vr + hd + 1, :]) for h in (2 * j, 2 * j + 1)]
        o_ref[:, j * LANES:(j + 1) * LANES] = _bf(jnp.concatenate(o, axis=0).T)


def _attention(qh, kbd, vt, batch, s_tot, c_len):
    n, qw = qh.shape
    tq = 256
    tk = ATT_TK
    nq = s_tot // tq
    assert (c_len // tk) % 2 == 1 and (s_tot // tk) % 2 == 1, "the key-chunk loop is unrolled by two plus a tail"
    kern = functools.partial(_attn_kernel, tq=tq, tk=tk, n_ctx_q=c_len // tq,
                             n_ctx_kv=c_len // tk, n_kv=s_tot // tk)
    return pl.pallas_call(
        kern,
        grid=(batch, nq),
        in_specs=[pl.BlockSpec((tq, qw), lambda b, i: (b * nq + i, 0)),
                  pl.BlockSpec((s_tot // tk, 2 * tk, LANES), lambda b, i: (b, 0, 0)),
                  pl.BlockSpec((s_tot // tk, 2, ATT_VROWS, tk), lambda b, i: (b, 0, 0, 0))],
        out_specs=pl.BlockSpec((tq, qw), lambda b, i: (b * nq + i, 0)),
        out_shape=jax.ShapeDtypeStruct((n, qw), BF16),
        scratch_shapes=[pltpu.VMEM((2 * (qw // LANES) * ATT_VROWS, tq), F32),
                        pltpu.VMEM((2 * (qw // LANES), tk, tq), F32),
                        pltpu.VMEM((2 * (qw // LANES), tk, tq), F32)],
        compiler_params=_cparams("parallel", "parallel"),
        name="gqa_attention",
    )(qh, kbd, vt)


def _cmlp_kernel(uv_ref, lng_ref, lnb_ref, ws_ref, bs_ref, o_ref, *, nchunk, width):
    x = uv_ref[...].astype(F32)
    g = 0.5 * x * (1.0 + jnp.tanh(0.7978845608028654 * (x + 0.044715 * (x * x * x))))
    u = g[:, :width]
    v = g[:, width:]
    mu = jnp.mean(v, axis=-1, keepdims=True)
    dv = v - mu
    var = jnp.mean(dv * dv, axis=-1, keepdims=True)
    vn = _bf(dv * lax.rsqrt(var + EPS) * lng_ref[...] + lnb_ref[...])
    ngroups = width // CMLP_CHUNK
    for c in range(nchunk):
        r0 = c * CMLP_CHUNK
        for gi in range(ngroups):
            c0 = gi * CMLP_CHUNK
            s = _dot(ws_ref[gi], vn[r0:r0 + CMLP_CHUNK, c0:c0 + CMLP_CHUNK]) + bs_ref[gi]
            o_ref[r0:r0 + CMLP_CHUNK, c0:c0 + CMLP_CHUNK] = _bf(u[r0:r0 + CMLP_CHUNK, c0:c0 + CMLP_CHUNK] * s)


def _chunk_mlp(big, ln_g, ln_b, ws, bs_b):
    n = big.shape[0]
    width = ln_g.shape[1]
    tr = _pick_tile(n, (512, 256, 128))
    kern = functools.partial(_cmlp_kernel, nchunk=tr // CMLP_CHUNK, width=width)
    ng = ws.shape[0]
    return pl.pallas_call(
        kern,
        grid=(n // tr,),
        in_specs=[pl.BlockSpec((tr, 2 * width), lambda i: (i, COL_UV // (2 * width))),
                  pl.BlockSpec((1, width), lambda i: (0, 0)),
                  pl.BlockSpec((1, width), lambda i: (0, 0)),
                  pl.BlockSpec((ng, CMLP_CHUNK, CMLP_CHUNK), lambda i: (0, 0, 0)),
                  pl.BlockSpec((ng, CMLP_CHUNK, CMLP_CHUNK), lambda i: (0, 0, 0))],
        out_specs=pl.BlockSpec((tr, width), lambda i: (i, 0)),
        out_shape=jax.ShapeDtypeStruct((n, width), BF16),
        compiler_params=_cparams("parallel"),
        name="chunk_gmlp",
    )(big, ln_g, ln_b, ws, bs_b)


def _rwkv_prep_kernel(x_ref, xp_ref, xn_ref, lo_ref, conv_ref, w0_ref, w2_ref, a0_ref, a2_ref,
                      kk0_ref, kk1_ref, rk_ref,
                      v_o, bonus_o, at_f, rt_f, bt_f, kt_f, bb_f, kb_f, pl_f,
                      at_b, rt_b, bt_b, kt_b, bb_b, kb_b, pl_b, *, tm, width, blocks_per_seq, ctx_blocks):
    i = pl.program_id(0)
    j = i % blocks_per_seq
    is_first = jnp.logical_or(j == 0, j == ctx_blocks)
    is_last = jnp.logical_or(j == ctx_blocks - 1, j == blocks_per_seq - 1)
    row = lax.broadcasted_iota(jnp.int32, (tm, width), 0)
    gsum = _group_matrix(1.0)
    halo = xp_ref.shape[0]

    def conv(c):
        cs = slice(c * width, (c + 1) * width)
        x = x_ref[:, cs].astype(F32)
        prev_row = jnp.where(is_first, 0.0, xp_ref[halo - 1:halo, cs].astype(F32))
        next_row = jnp.where(is_last, 0.0, xn_ref[0:1, cs].astype(F32))
        xprev = jnp.where(row == 0, prev_row, pltpu.roll(x, 1, 0))
        xnext = jnp.where(row == tm - 1, next_row, pltpu.roll(x, tm - 1, 0))
        return xprev * conv_ref[0:1, cs] + x * conv_ref[1:2, cs] + xnext * conv_ref[2:3, cs]

    r = conv(0)
    k = conv(1)
    v = conv(2)
    v_o[...] = _bf(v)

    def group_sum(x):
        parts = [_dot_split(x[:, c * LANES:(c + 1) * LANES], gsum, 2) for c in range(width // LANES)]
        return jnp.concatenate(parts, axis=1)

    kk = k * kk0_ref[...]
    kk = kk * lax.rsqrt(group_sum(kk * kk) + 1e-12)
    bonus_o[...] = _bf(group_sum(r * k * rk_ref[...]) * v)

    lo = lo_ref[...].astype(F32)
    wd = w0_ref[...] + _dot(_bf(jnp.tanh(lo[:, :LANES])), w2_ref[...])
    ad = jax.nn.sigmoid(a0_ref[...] + _dot(_bf(lo[:, LANES:]), a2_ref[...]))
    lw = -float(np.exp(-0.5)) * jax.nn.sigmoid(wd)

    r2 = lax.broadcasted_iota(jnp.int32, (tm, tm), 0)
    c2 = lax.broadcasted_iota(jnp.int32, (tm, tm), 1)
    same = (r2 // SCAN_CHUNK) == (c2 // SCAN_CHUNK)
    tri_pre = jnp.where(jnp.logical_and(same, c2 <= r2), 1.0, 0.0).astype(BF16)
    tri_suf = jnp.where(jnp.logical_and(same, c2 >= r2), 1.0, 0.0).astype(BF16)
    nchunk = tm // SCAN_CHUNK

    outs = ((at_f, rt_f, bt_f, kt_f, bb_f, kb_f, pl_f), (at_b, rt_b, bt_b, kt_b, bb_b, kb_b, pl_b))
    for d in range(2):
        ds_ = slice(d * width, (d + 1) * width)
        lwd = lw[:, ds_]
        pre = _dot_split_left(tri_pre, lwd, 3)
        suf = _dot_split_left(tri_suf, lwd, 3)
        cin, rem = (pre, suf - lwd) if d == 0 else (suf, pre - lwd)
        cex = cin - lwd
        a_d = ad[:, ds_]
        b = kk * a_d
        kd = k * (1.0 + (a_d - 1.0) * kk1_ref[...])
        at_o, rt_o, bt_o, kt_o, bb_o, kb_o, pl_o = outs[d]
        at_o[...] = _bf(-kk * jnp.exp(cex))
        rt_o[...] = _bf(r * jnp.exp(cin))
        pinv = jnp.exp(-cin)
        bt_o[...] = _bf(b * pinv)
        kt_o[...] = _bf(kd * pinv)
        pend = jnp.exp(rem)
        bb_o[...] = _bf(b * pend)
        kb_o[...] = _bf(kd * pend)
        for c in range(nchunk):
            last = (c + 1) * SCAN_CHUNK - 1
            pl_o[c] = jnp.exp(pre[last:last + 1, :])


def _rwkv_prep(big, conv_w, w0, w2s, a0, a2s, kk0, kk1, rk, s_tot, c_len):
    n = big.shape[0]
    width = rk.shape[1]
    tm = MOD_ROWS
    halo = 16
    hb = tm // halo
    nhalo = n // halo
    nchunk = tm // SCAN_CHUNK
    kern = functools.partial(_rwkv_prep_kernel, tm=tm, width=width, blocks_per_seq=s_tot // tm,
                             ctx_blocks=c_len // tm)
    tok = pl.BlockSpec((tm, width), lambda i: (i, 0))
    plspec = pl.BlockSpec((nchunk, 1, width), lambda i: (i, 0, 0))
    tok_shape = jax.ShapeDtypeStruct((n, width), BF16)
    pl_shape = jax.ShapeDtypeStruct((n // SCAN_CHUNK, 1, width), F32)
    full = lambda a: pl.BlockSpec(a.shape, lambda i: (0,) * a.ndim)
    return pl.pallas_call(
        kern,
        grid=(n // tm,),
        in_specs=[pl.BlockSpec((tm, 3 * width), lambda i: (i, 0)),
                  pl.BlockSpec((halo, 3 * width), lambda i: (jnp.maximum(i * hb - 1, 0), 0)),
                  pl.BlockSpec((halo, 3 * width), lambda i: (jnp.minimum((i + 1) * hb, nhalo - 1), 0)),
                  pl.BlockSpec((tm, 2 * LANES), lambda i: (i, COL_WLOW // (2 * LANES))),
                  full(conv_w), full(w0), full(w2s), full(a0), full(a2s), full(kk0), full(kk1), full(rk)],
        out_specs=[tok, tok] + [tok] * 6 + [plspec] + [tok] * 6 + [plspec],
        out_shape=[tok_shape, tok_shape] + [tok_shape] * 6 + [pl_shape] + [tok_shape] * 6 + [pl_shape],
        compiler_params=_cparams("parallel"),
        name="rwkv_prepare",
    )(big, big, big, big, conv_w, w0, w2s, a0, a2s, kk0, kk1, rk)


def _scan_chunks(chains):
    L = SCAN_CHUNK
    lane = lax.broadcasted_iota(jnp.int32, (L, LANES), 1)
    m0 = _bf(jnp.where(lane < HEAD_DIM, 1.0, 0.0))
    m1 = _bf(jnp.where(lane < HEAD_DIM, 0.0, 1.0))

    def stack(x):
        return jnp.concatenate([x * m0, x * m1], axis=0)

    r2 = lax.broadcasted_iota(jnp.int32, (2 * L, 2 * L), 0)
    c2 = lax.broadcasted_iota(jnp.int32, (2 * L, 2 * L), 1)
    same = (r2 // L) == (c2 // L)
    tr, tc = r2 % L, c2 % L
    masks = {True: (jnp.logical_and(same, tc < tr), jnp.logical_and(same, tc <= tr)),
             False: (jnp.logical_and(same, tc > tr), jnp.logical_and(same, tc >= tr))}
    eye = r2 == c2
    fwd = [ch[9] for ch in chains]
    nc = range(len(chains))

    a_s = [stack(ch[0]) for ch in chains]
    r_s = [stack(ch[1]) for ch in chains]
    v_s = [stack(ch[6]) for ch in chains]
    big1 = [_dot_nt(jnp.concatenate([a_s[i], r_s[i]], axis=0),
                    jnp.concatenate([stack(chains[i][2]), stack(chains[i][3])], axis=0)) for i in nc]
    mab = [jnp.where(masks[fwd[i]][0], big1[i][:2 * L, :2 * L], 0.0) for i in nc]
    mak = [_bf(jnp.where(masks[fwd[i]][0], big1[i][:2 * L, 2 * L:], 0.0)) for i in nc]
    lhs_top = [_bf(jnp.where(jnp.concatenate([masks[fwd[i]][1]] * 2, axis=1), big1[i][2 * L:], 0.0)) for i in nc]
    mv = [_dot(mak[i], v_s[i]) for i in nc]
    x = [jnp.concatenate([a_s[i].astype(F32), mv[i]], axis=1) for i in nc]
    mp = mab
    steps = int(np.log2(L))
    for it in range(steps):
        if it < steps - 1:
            res = [_dot(_bf(mp[i]), _bf(jnp.concatenate([mp[i], x[i]], axis=1))) for i in nc]
            x = [x[i] + res[i][:, 2 * L:] for i in nc]
            mp = [res[i][:, :2 * L] for i in nc]
        else:
            res = [_dot(_bf(mp[i]), _bf(x[i])) for i in nc]
            x = [x[i] + res[i] for i in nc]
    rhs2 = [jnp.concatenate([_bf(x[i]), jnp.concatenate([jnp.zeros_like(v_s[i]), v_s[i]], axis=1)], axis=0)
            for i in nc]
    lhs_bot = [_bf(jnp.concatenate([stack(chains[i][4]), stack(chains[i][5])], axis=0).astype(F32).T)
               for i in nc]
    res2 = [_dot(jnp.concatenate([lhs_top[i], lhs_bot[i]], axis=0), rhs2[i]) for i in nc]
    lhs3 = [_bf(jnp.concatenate(
        [r_s[i].astype(F32) + res2[i][:2 * L, :LANES],
         res2[i][2 * L:, :LANES] + jnp.where(eye, jnp.broadcast_to(chains[i][7], (LANES, LANES)), 0.0)], axis=0))
        for i in nc]
    res3 = [_dot(lhs3[i], _bf(chains[i][8])) for i in nc]
    out = []
    for i in nc:
        ys = res3[i][:2 * L] + res2[i][:2 * L, LANES:]
        out.append((ys[:L] + ys[L:], res3[i][2 * L:] + res2[i][2 * L:, LANES:]))
    return out


def _rwkv_scan_kernel(v_f, at_f, rt_f, bt_f, kt_f, bb_f, kb_f, pl_f,
                      v_b, at_b, rt_b, bt_b, kt_b, bb_b, kb_b, pl_b,
                      yf_ref, yb_ref, z_ref, *, npairs):
    @pl.when(pl.program_id(1) == 0)
    def _():
        z_ref[...] = jnp.zeros_like(z_ref)

    dirs = ((v_f, at_f, rt_f, bt_f, kt_f, bb_f, kb_f, pl_f, yf_ref, True),
            (v_b, at_b, rt_b, bt_b, kt_b, bb_b, kb_b, pl_b, yb_ref, False))
    chains, dest = [], []
    for d, (v, at, rt, bt, kt, bb, kb, plr, y_ref, fwd) in enumerate(dirs):
        for p in range(npairs):
            cs = slice(p * LANES, (p + 1) * LANES)
            chains.append((at[:, cs], rt[:, cs], bt[:, cs], kt[:, cs], bb[:, cs], kb[:, cs],
                           v[:, cs], plr[0][:, cs], z_ref[d, p], fwd))
            dest.append((y_ref, cs, d, p))
    for (y_ref, cs, d, p), (y, znew) in zip(dest, _scan_chunks(chains)):
        y_ref[:, cs] = y
        z_ref[d, p] = znew


def _rwkv_scan(prep, batch, s_tot, c_len):
    (v, _bonus, at_f, rt_f, bt_f, kt_f, bb_f, kb_f, pl_f, at_b, rt_b, bt_b, kt_b, bb_b, kb_b, pl_b) = prep
    n, width = v.shape
    L = SCAN_CHUNK
    nch = s_tot // L
    ncc = c_len // L
    npairs = width // LANES

    def fmap(b, c):
        return (b * nch + c, 0)

    def bmap(b, c):
        return (b * nch + jnp.where(c < ncc, ncc - 1 - c, nch - 1 - (c - ncc)), 0)

    def tok(m):
        return pl.BlockSpec((L, width), m)

    def pls(m):
        return pl.BlockSpec((1, 1, width), lambda b, c: m(b, c) + (0,))

    kern = functools.partial(_rwkv_scan_kernel, npairs=npairs)
    return pl.pallas_call(
        kern,
        grid=(batch, nch),
        in_specs=[tok(fmap)] * 7 + [pls(fmap)] + [tok(bmap)] * 7 + [pls(bmap)],
        out_specs=[tok(fmap), tok(bmap)],
        out_shape=[jax.ShapeDtypeStruct((n, width), F32)] * 2,
        scratch_shapes=[pltpu.VMEM((2, npairs, LANES, LANES), F32)],
        compiler_params=_cparams("parallel", "arbitrary"),
        name="rwkv_scan",
    )(v, at_f, rt_f, bt_f, kt_f, bb_f, kb_f, pl_f, v, at_b, rt_b, bt_b, kt_b, bb_b, kb_b, pl_b)


def _rwkv_readout_kernel(yf_ref, yb_ref, bonus_ref, gl_ref, g2_ref, lg_ref, lb_ref, o_ref):
    avg = _group_matrix(1.0 / HEAD_DIM)
    gate = _dot(_bf(jax.nn.sigmoid(gl_ref[...].astype(F32))), g2_ref[...])
    for c in range(o_ref.shape[1] // LANES):
        cs = slice(c * LANES, (c + 1) * LANES)
        y = yf_ref[:, cs] + yb_ref[:, cs]
        mu = _dot_split(y, avg, 2)
        dy = y - mu
        var = _dot_split(dy * dy, avg, 2)
        yn = dy * lax.rsqrt(var + LNX_EPS) * lg_ref[:, cs] + lb_ref[:, cs]
        o_ref[:, cs] = _bf((yn + bonus_ref[:, cs].astype(F32)) * gate[:, cs])


def _rwkv_readout(yf, yb, bonus, big, g2, lnx_g, lnx_b):
    n, width = yf.shape
    tm = _pick_tile(n, (512, 256))
    tok = pl.BlockSpec((tm, width), lambda i: (i, 0))
    full = lambda a: pl.BlockSpec(a.shape, lambda i: (0,) * a.ndim)
    return pl.pallas_call(
        _rwkv_readout_kernel,
        grid=(n // tm,),
        in_specs=[tok, tok, tok, pl.BlockSpec((tm, LANES), lambda i: (i, COL_GLOW // LANES)),
                  full(g2), full(lnx_g), full(lnx_b)],
        out_specs=tok,
        out_shape=jax.ShapeDtypeStruct((n, width), BF16),
        compiler_params=_cparams("parallel"),
        name="rwkv_readout",
    )(yf, yb, bonus, big, g2, lnx_g, lnx_b)


def _merge_kernel(oa_ref, ob_ref, oc_ref, gt_ref, x_ref, wb_ref, wo_ref, ng_ref, mod_ref, o_ref, *, nsub, d):
    y = None
    for br, ref in enumerate((oa_ref, ob_ref, oc_ref)):
        g = jax.nn.sigmoid(gt_ref[:, br * d:(br + 1) * d].astype(F32))
        t = g * _dot(ref[...], wb_ref[br])
        y = t if y is None else y + t
    o = _dot(_bf(y), wo_ref[...])
    o_ref[...] = x_ref[...] + _mod_rows(mod_ref, nsub, 2, d) * _rms(o, ng_ref[...])


def _merge(oa, ob, oc, big, x, wb, wo, ng, modblk):
    n, d = x.shape
    width = oa.shape[1]
    tm = _pick_tile(n, (512, 256))
    nsub = tm // MOD_ROWS
    kern = functools.partial(_merge_kernel, nsub=nsub, d=d)
    br = pl.BlockSpec((tm, width), lambda i: (i, 0))
    return pl.pallas_call(
        kern,
        grid=(n // tm,),
        in_specs=[br, br, br,
                  pl.BlockSpec((tm, 3 * d), lambda i: (i, COL_GATES // (3 * d))),
                  pl.BlockSpec((tm, d), lambda i: (i, 0)),
                  pl.BlockSpec(wb.shape, lambda i: (0, 0, 0)),
                  pl.BlockSpec(wo.shape, lambda i: (0, 0)),
                  pl.BlockSpec((1, d), lambda i: (0, 0)),
                  pl.BlockSpec((nsub, 1, 6 * d), lambda i: (i, 0, 0))],
        out_specs=pl.BlockSpec((tm, d), lambda i: (i, 0)),
        out_shape=jax.ShapeDtypeStruct((n, d), F32),
        compiler_params=_cparams("parallel"),
        name="merge_out_proj",
    )(oa, ob, oc, big, x, wb, wo, ng, modblk)


def _ffn_kernel(x_ref, g_ref, mod_ref, wg_ref, wu_ref, wd_ref, ng_ref, o_ref, h_ref, acc_ref, *, nsub, d):
    j = pl.program_id(1)

    @pl.when(j == 0)
    def _():
        y = _rms(x_ref[...], g_ref[...])
        h_ref[...] = _bf(y * (1.0 + _mod_rows(mod_ref, nsub, 4, d)) + _mod_rows(mod_ref, nsub, 3, d))
        acc_ref[...] = jnp.zeros_like(acc_ref)

    h = h_ref[...]
    g = _dot(h, wg_ref[...])
    u = _dot(h, wu_ref[...])
    acc_ref[...] += _dot(_bf(g * jax.nn.sigmoid(g) * u), wd_ref[...])

    @pl.when(j == pl.num_programs(1) - 1)
    def _():
        o_ref[...] = x_ref[...] + _mod_rows(mod_ref, nsub, 5, d) * _rms(acc_ref[...], ng_ref[...])


def _dense_ffn(x, gain_in, gain_out, modblk, w_gu, w_down):
    n, d = x.shape
    f = w_down.shape[0]
    tm = _pick_tile(n, (1024, 512, 256))
    tf = _pick_tile(f, (1408, 1024, 512, 256, 128))
    nf = f // tf
    nsub = tm // MOD_ROWS
    kern = functools.partial(_ffn_kernel, nsub=nsub, d=d)
    return pl.pallas_call(
        kern,
        grid=(n // tm, nf),
        in_specs=[pl.BlockSpec((tm, d), lambda i, j: (i, 0)),
                  pl.BlockSpec((1, d), lambda i, j: (0, 0)),
                  pl.BlockSpec((nsub, 1, 6 * d), lambda i, j: (i, 0, 0)),
                  pl.BlockSpec((d, tf), lambda i, j: (0, j)),
                  pl.BlockSpec((d, tf), lambda i, j: (0, j + nf)),
                  pl.BlockSpec((tf, d), lambda i, j: (j, 0)),
                  pl.BlockSpec((1, d), lambda i, j: (0, 0))],
        out_specs=pl.BlockSpec((tm, d), lambda i, j: (i, 0)),
        out_shape=jax.ShapeDtypeStruct((n, d), F32),
        scratch_shapes=[pltpu.VMEM((tm, d), BF16), pltpu.VMEM((tm, d), F32)],
        compiler_params=_cparams("parallel", "arbitrary"),
        name="dense_swiglu_ffn",
    )(x, gain_in, modblk, w_gu, w_gu, w_down, gain_out)


def _router_kernel(x_ref, g_ref, mod_ref, wr_ref, h_ref, comb_ref, rank_ref, combt_ref, rankt_ref, cnt_ref,
                   *, nsub, d):
    lane = lax.broadcasted_iota(jnp.int32, (MOD_ROWS, LANES), 1)
    r2 = lax.broadcasted_iota(jnp.int32, (MOD_ROWS, MOD_ROWS), 0)
    c2 = lax.broadcasted_iota(jnp.int32, (MOD_ROWS, MOD_ROWS), 1)
    tri = jnp.where(c2 < r2, 1.0, 0.0).astype(BF16)
    ninf = jnp.float32(-jnp.inf)
    running = jnp.zeros((1, LANES), F32)
    for s in range(nsub):
        rows = slice(s * MOD_ROWS, (s + 1) * MOD_ROWS)
        m = mod_ref[s]
        h = _rms(x_ref[rows, :], g_ref[...]) * (1.0 + m[:, 4 * d:5 * d]) + m[:, 3 * d:4 * d]
        h_ref[rows, :] = _bf(h)
        logits = jnp.dot(h, wr_ref[...], precision=HIGHEST, preferred_element_type=F32)
        logits = jnp.where(lane < N_EXPERTS, logits, ninf)
        m1 = jnp.max(logits, axis=-1, keepdims=True)
        i1 = jnp.min(jnp.where(logits == m1, lane, LANES), axis=-1, keepdims=True)
        rest = jnp.where(lane == i1, ninf, logits)
        m2 = jnp.max(rest, axis=-1, keepdims=True)
        i2 = jnp.min(jnp.where(rest == m2, lane, LANES), axis=-1, keepdims=True)
        e2 = jnp.exp(m2 - m1)
        w1 = 1.0 / (1.0 + e2)
        comb = jnp.where(lane == i1, w1, 0.0) + jnp.where(lane == i2, e2 * w1, 0.0)
        ind = jnp.where(comb > 0.0, 1.0, 0.0)
        rank = _dot(tri, _bf(ind)) + running
        running = running + jnp.sum(ind, axis=0, keepdims=True)
        comb_ref[rows, :] = comb
        rank_ref[rows, :] = rank
        combt_ref[:, rows] = comb.T[:N_EXPERTS, :]
        rankt_ref[:, rows] = rank.T[:N_EXPERTS, :]
    cnt_ref[0] = running


def _router(x, gain_in, modblk, w_router_pad, tm):
    n, d = x.shape
    nsub = tm // MOD_ROWS
    nt = n // tm
    kern = functools.partial(_router_kernel, nsub=nsub, d=d)
    tokm = pl.BlockSpec((tm, LANES), lambda i: (i, 0))
    expm = pl.BlockSpec((N_EXPERTS, tm), lambda i: (0, i))
    return pl.pallas_call(
        kern,
        grid=(nt,),
        in_specs=[pl.BlockSpec((tm, d), lambda i: (i, 0)),
                  pl.BlockSpec((1, d), lambda i: (0, 0)),
                  pl.BlockSpec((nsub, 1, 6 * d), lambda i: (i, 0, 0)),
                  pl.BlockSpec((d, LANES), lambda i: (0, 0))],
        out_specs=[pl.BlockSpec((tm, d), lambda i: (i, 0)), tokm, tokm, expm, expm,
                   pl.BlockSpec((1, 1, LANES), lambda i: (i, 0, 0))],
        out_shape=[jax.ShapeDtypeStruct((n, d), BF16),
                   jax.ShapeDtypeStruct((n, LANES), F32), jax.ShapeDtypeStruct((n, LANES), F32),
                   jax.ShapeDtypeStruct((N_EXPERTS, n), F32), jax.ShapeDtypeStruct((N_EXPERTS, n), F32),
                   jax.ShapeDtypeStruct((nt, 1, LANES), F32)],
        compiler_params=_cparams("parallel"),
        name="moe_router",
    )(x, gain_in, modblk, w_router_pad)


def _moe_kernel(cnt_ref, h_ref, comb_ref, rank_ref, combt_ref, rankt_ref, wg_ref, wu_ref, wd_ref, o_ref,
                xe_ref, y_ref, acc_ref, *, tm, rb):
    i = pl.program_id(0)
    e = pl.program_id(1)
    j = pl.program_id(2)
    nf = pl.num_programs(2)
    nblk = (cnt_ref[i * N_EXPERTS + e] + rb - 1) // rb

    @pl.when(jnp.logical_and(e == 0, j == 0))
    def _():
        acc_ref[...] = jnp.zeros_like(acc_ref)

    @pl.when(j == 0)
    def _():
        key = jnp.where(combt_ref[pl.ds(e, 1), :] > 0.0, rankt_ref[pl.ds(e, 1), :], -1.0)

        def gather(b, carry):
            r0 = pl.multiple_of(b * rb, rb)
            want = (r0 + lax.broadcasted_iota(jnp.int32, (rb, tm), 0)).astype(F32)
            sel = jnp.where(key == want, 1.0, 0.0).astype(BF16)
            xe_ref[pl.ds(r0, rb), :] = _bf(_dot(sel, h_ref[...]))
            return carry

        lax.fori_loop(0, nblk, gather, 0)

    def expert(b, carry):
        rows = pl.ds(pl.multiple_of(b * rb, rb), rb)
        xb = xe_ref[rows, :]
        g = _dot(xb, wg_ref[0])
        u = _dot(xb, wu_ref[0])
        part = _dot(_bf(g * jax.nn.sigmoid(g) * u), wd_ref[0])

        @pl.when(j == 0)
        def _():
            y_ref[rows, :] = part

        @pl.when(j > 0)
        def _():
            y_ref[rows, :] += part

        return carry

    lax.fori_loop(0, nblk, expert, 0)

    @pl.when(j == nf - 1)
    def _():
        lane = lax.broadcasted_iota(jnp.int32, (tm, LANES), 1)
        rank_col = jnp.sum(jnp.where(lane == e, rank_ref[...], 0.0), axis=1, keepdims=True)
        w_col = jnp.sum(jnp.where(lane == e, comb_ref[...], 0.0), axis=1, keepdims=True)

        def scatter(b, carry):
            r0 = pl.multiple_of(b * rb, rb)
            want = (r0 + lax.broadcasted_iota(jnp.int32, (tm, rb), 1)).astype(F32)
            selw = _bf(jnp.where(rank_col == want, w_col, 0.0))
            acc_ref[...] += _dot(selw, _bf(y_ref[pl.ds(r0, rb), :]))
            return carry

        lax.fori_loop(0, nblk, scatter, 0)

    @pl.when(jnp.logical_and(e == pl.num_programs(1) - 1, j == nf - 1))
    def _():
        o_ref[...] = _bf(acc_ref[...])


def _moe_ffn(hb, comb, rank, combt, rankt, counts, w_gu, w_down, tm):
    n, d = hb.shape
    ne, f, _ = w_down.shape
    tf = _pick_tile(f, (512, 256, 128))
    nf = f // tf
    rb = MOE_ROW_BLOCK
    kern = functools.partial(_moe_kernel, tm=tm, rb=rb)
    tokm = pl.BlockSpec((tm, LANES), lambda i, e, j, c: (i, 0))
    expm = pl.BlockSpec((N_EXPERTS, tm), lambda i, e, j, c: (0, i))
    grid_spec = pltpu.PrefetchScalarGridSpec(
        num_scalar_prefetch=1,
        grid=(n // tm, ne, nf),
        in_specs=[pl.BlockSpec((tm, d), lambda i, e, j, c: (i, 0)), tokm, tokm, expm, expm,
                  pl.BlockSpec((1, d, tf), lambda i, e, j, c: (e, 0, j)),
                  pl.BlockSpec((1, d, tf), lambda i, e, j, c: (e, 0, j + nf)),
                  pl.BlockSpec((1, tf, d), lambda i, e, j, c: (e, j, 0))],
        out_specs=pl.BlockSpec((tm, d), lambda i, e, j, c: (i, 0)),
        scratch_shapes=[pltpu.VMEM((tm, d), BF16), pltpu.VMEM((tm, d), F32), pltpu.VMEM((tm, d), F32)])
    return pl.pallas_call(
        kern,
        grid_spec=grid_spec,
        out_shape=jax.ShapeDtypeStruct((n, d), BF16),
        compiler_params=_cparams("parallel", "arbitrary", "arbitrary"),
        name="moe_swiglu_ffn",
    )(counts, hb, comb, rank, combt, rankt, w_gu, w_gu, w_down)


def _residual_kernel(x_ref, y_ref, ng_ref, mod_ref, o_ref, *, nsub, d):
    o_ref[...] = x_ref[...] + _mod_rows(mod_ref, nsub, 5, d) * _rms(y_ref[...].astype(F32), ng_ref[...])


def _gated_residual(x, y, gain_out, modblk):
    n, d = x.shape
    tm = _pick_tile(n, (512, 256))
    nsub = tm // MOD_ROWS
    tok = pl.BlockSpec((tm, d), lambda i: (i, 0))
    return pl.pallas_call(
        functools.partial(_residual_kernel, nsub=nsub, d=d),
        grid=(n // tm,),
        in_specs=[tok, tok, pl.BlockSpec((1, d), lambda i: (0, 0)),
                  pl.BlockSpec((nsub, 1, 6 * d), lambda i: (i, 0, 0))],
        out_specs=tok,
        out_shape=jax.ShapeDtypeStruct((n, d), F32),
        compiler_params=_cparams("parallel"),
        name="moe_gated_residual",
    )(x, y, gain_out, modblk)


def _rope_tables(t_len, c_len):
    pairs = HEAD_DIM // 4
    rows = t_len // GRID_W
    row = jnp.repeat(jnp.arange(rows, dtype=F32), GRID_W)
    col = jnp.tile(jnp.arange(GRID_W, dtype=F32), rows)
    freqs = ROPE_BASE ** (-jnp.arange(pairs, dtype=F32) / pairs)
    ar = row[:, None] * freqs
    ac = col[:, None] * freqs
    cos = jnp.concatenate([jnp.cos(ar), jnp.cos(ar), jnp.cos(ac), jnp.cos(ac)], axis=1)
    sin = jnp.concatenate([-jnp.sin(ar), jnp.sin(ar), -jnp.sin(ac), jnp.sin(ac)], axis=1)
    cos = jnp.concatenate([jnp.ones((c_len, HEAD_DIM), F32), cos], axis=0)
    sin = jnp.concatenate([jnp.zeros((c_len, HEAD_DIM), F32), sin], axis=0)
    return jnp.tile(cos, (1, 2)), jnp.tile(sin, (1, 2))


def _block_diag2(w):
    z = jnp.zeros_like(w[0])
    return jnp.concatenate([jnp.concatenate([w[0], z], axis=1), jnp.concatenate([z, w[1]], axis=1)], axis=0)


def kernel(x, c, ctx, c_ctx, w_mod, b_mod, norm_gain, w_in, qk_gain, rwkv_conv, decay_w0, decay_w2, iclr_a0, iclr_a2, key_k, bonus_rk, gate_g2, lnx_gain, lnx_bias, cmlp_ln_gain, cmlp_ln_bias, cmlp_ws, cmlp_bs, w_branch, w_out, ffn_w_gu, ffn_w_down, moe_router, moe_w_gu, moe_w_down):
    batch, t_len, d = x.shape
    c_len = ctx.shape[1]
    depth = w_mod.shape[0]
    s_tot = c_len + t_len
    n = batch * s_tot
    assert c_len % MOD_ROWS == 0 and t_len % MOD_ROWS == 0 and d % LANES == 0
    width = bonus_rk.shape[1] * bonus_rk.shape[2]
    nheads = width // HEAD_DIM

    xs = jnp.concatenate([ctx, x], axis=1).reshape(n, d)

    mod_rows = 8 * ((batch + 1 + 7) // 8)
    cvec = jnp.zeros((mod_rows, d), F32).at[0].set(c_ctx).at[1:batch + 1].set(c)
    mods = _modulation(cvec, w_mod, b_mod)
    mod_ctx = jnp.broadcast_to(mods[:, 0:1, None, :], (depth, batch, c_len // MOD_ROWS, 6 * d))
    mod_lat = jnp.broadcast_to(mods[:, 1:batch + 1, None, :], (depth, batch, t_len // MOD_ROWS, 6 * d))
    modblk_all = jnp.concatenate([mod_ctx, mod_lat], axis=2).reshape(depth, n // MOD_ROWS, 1, 6 * d)

    order = np.array(Q_HEAD_ORDER)
    nl = depth
    wq = w_in[:, :, 2048:2560].reshape(nl, d, nheads, HEAD_DIM)[:, :, order].reshape(nl, d, width)
    w_in_p = jnp.concatenate([
        w_in[:, :, 256:1792], wq, w_in[:, :, 2688:3712], w_in[:, :, 3712:6784],
        w_in[:, :, 0:128], w_in[:, :, 128:256], w_in[:, :, 1792:1920], w_in[:, :, 1920:2048],
        w_in[:, :, 2560:2688], jnp.zeros((nl, d, IN_PAD - 6784), F32)], axis=2).astype(BF16)
    wb = w_branch.astype(BF16)
    wb0 = wb[:, 0].reshape(nl, nheads, HEAD_DIM, d)[:, order].reshape(nl, width, d)
    wb = jnp.concatenate([wb0[:, None], wb[:, 1:]], axis=1)
    wo = w_out.astype(BF16)
    cos, sin = _rope_tables(t_len, c_len)
    qg = jnp.tile(qk_gain[:, 0], (1, 2))[:, None, :]
    kg = jnp.tile(qk_gain[:, 1], (1, 2))[:, None, :]
    ws_b = cmlp_ws.astype(BF16)
    bs_b = jnp.broadcast_to(cmlp_bs[..., None], cmlp_bs.shape + (CMLP_CHUNK,))
    w2s = jnp.stack([_block_diag2(decay_w2[l]) for l in range(nl)]).astype(BF16)
    a2s = jnp.stack([_block_diag2(iclr_a2[l]) for l in range(nl)]).astype(BF16)
    w0 = decay_w0.reshape(nl, 1, 2 * width)
    a0 = iclr_a0.reshape(nl, 1, 2 * width)
    rk = bonus_rk.reshape(nl, 1, width)
    g2 = gate_g2.astype(BF16)
    ffn_gu = ffn_w_gu.astype(BF16)
    ffn_dn = ffn_w_down.astype(BF16)
    moe_gu = moe_w_gu.astype(BF16)
    moe_dn = moe_w_down.astype(BF16)
    router_pad = jnp.pad(moe_router, ((0, 0), (0, 0), (0, LANES - moe_router.shape[2])))

    for l in range(depth):
        modblk = modblk_all[l]
        ng = norm_gain[l]
        big = _norm_mod_matmul(xs, ng[0:1], modblk, w_in_p[l], 0, 1)
        qh, kbd, vt = _qk_prep(big, qg[l], kg[l], cos, sin, s_tot)
        oa = _attention(qh, kbd, vt, batch, s_tot, c_len)
        prep = _rwkv_prep(big, rwkv_conv[l], w0[l], w2s[l], a0[l], a2s[l],
                          key_k[l, 0:1], key_k[l, 1:2], rk[l], s_tot, c_len)
        yf, yb = _rwkv_scan(prep, batch, s_tot, c_len)
        ob = _rwkv_readout(yf, yb, prep[1], big, g2[l], lnx_gain[l][None], lnx_bias[l][None])
        oc = _chunk_mlp(big, cmlp_ln_gain[l][None], cmlp_ln_bias[l][None], ws_b[l], bs_b[l])
        xs = _merge(oa, ob, oc, big, xs, wb[l], wo[l], ng[1:2], modblk)
        if l % 2 == 0:
            xs = _dense_ffn(xs, ng[2:3], ng[3:4], modblk, ffn_gu[l // 2], ffn_dn[l // 2])
        else:
            tmoe = _pick_tile(n, (2048, 1024, 512, 256))
            hb, comb, rank, combt, rankt, cnt = _router(xs, ng[2:3], modblk, router_pad[l // 2], tmoe)
            counts = cnt[:, 0, :N_EXPERTS].astype(jnp.int32).reshape(-1)
            y = _moe_ffn(hb, comb, rank, combt, rankt, counts, moe_gu[l // 2], moe_dn[l // 2], tmoe)
            xs = _gated_residual(xs, y, ng[3:4], modblk)
    return xs.reshape(batch, s_tot, d)[:, c_len:, :]
```

```python
import functools

import jax
import jax.numpy as jnp
import numpy as np
from jax import lax
from jax.experimental import pallas as pl
from jax.experimental.pallas import tpu as pltpu

F32 = jnp.float32
BF16 = jnp.bfloat16
HIGHEST = lax.Precision.HIGHEST

EPS = 1e-6
LNX_EPS = 64e-5
HEAD_DIM = 64
ROPE_BASE = 10000.0
GRID_W = 64
LANES = 128
MOD_ROWS = 256
SCAN_CHUNK = 64
CMLP_CHUNK = 128
ATT_TK = 256
ATT_VROWS = HEAD_DIM + 16
ATT_GROUP = 1
N_EXPERTS = 8
MOE_ROW_BLOCK = 512
VMEM_LIMIT = 56 * 1024 * 1024

COL_RKV, COL_Q, COL_UV, COL_GATES = 0, 1536, 2048, 3072
COL_K, COL_V, COL_WLOW, COL_ALOW, COL_GLOW = 6144, 6272, 6400, 6528, 6656
IN_PAD = 7168
Q_HEAD_ORDER = (0, 4, 1, 5, 2, 6, 3, 7)


def _cparams(*sem):
    return pltpu.CompilerParams(dimension_semantics=sem, vmem_limit_bytes=VMEM_LIMIT)


def _dot(a, b):
    return jnp.dot(a, b, preferred_element_type=F32)


def _dot_nt(a, b):
    return lax.dot_general(a, b, (((1,), (1,)), ((), ())), preferred_element_type=F32)


def _bf(x):
    return x.astype(BF16)


def _dot_split(a, b_exact, terms):
    acc = None
    rem = a
    for _ in range(terms):
        piece = _bf(rem)
        rem = rem - piece.astype(F32)
        part = _dot(piece, b_exact)
        acc = part if acc is None else acc + part
    return acc


def _dot_split_left(a_exact, b, terms):
    acc = None
    rem = b
    for _ in range(terms):
        piece = _bf(rem)
        rem = rem - piece.astype(F32)
        part = _dot(a_exact, piece)
        acc = part if acc is None else acc + part
    return acc


def _group_matrix(scale):
    r = lax.broadcasted_iota(jnp.int32, (LANES, LANES), 0) // HEAD_DIM
    c = lax.broadcasted_iota(jnp.int32, (LANES, LANES), 1) // HEAD_DIM
    return jnp.where(r == c, scale, 0.0).astype(BF16)


def _pick_tile(n, candidates):
    for t in candidates:
        if n % t == 0:
            return t
    raise ValueError(f"no tile in {candidates} divides {n}")


def _mod_rows(mod_ref, nsub, idx, d):
    parts = [jnp.broadcast_to(mod_ref[s][:, idx * d:(idx + 1) * d], (MOD_ROWS, d)) for s in range(nsub)]
    return parts[0] if nsub == 1 else jnp.concatenate(parts, axis=0)


def _rms(x, g):
    return x * lax.rsqrt(jnp.mean(x * x, axis=-1, keepdims=True) + EPS) * g


def _mod_kernel(c_ref, w_ref, b_ref, o_ref):
    cv = c_ref[...]
    s = cv * jax.nn.sigmoid(cv)
    o_ref[0] = jnp.dot(s, w_ref[0], precision=HIGHEST, preferred_element_type=F32) + b_ref[0]


def _modulation(cvec, w_mod, b_mod):
    nl, d, d6 = w_mod.shape
    rows = cvec.shape[0]
    tn = 1024
    return pl.pallas_call(
        _mod_kernel,
        grid=(nl, d6 // tn),
        in_specs=[pl.BlockSpec((rows, d), lambda l, j: (0, 0)),
                  pl.BlockSpec((1, d, tn), lambda l, j: (l, 0, j)),
                  pl.BlockSpec((1, 1, tn), lambda l, j: (l, 0, j))],
        out_specs=pl.BlockSpec((1, rows, tn), lambda l, j: (l, 0, j)),
        out_shape=jax.ShapeDtypeStruct((nl, rows, d6), F32),
        compiler_params=_cparams("parallel", "parallel"),
        name="modulation",
    )(cvec, w_mod, b_mod.reshape(nl, 1, d6))


def _nmm_kernel(x_ref, g_ref, mod_ref, w_ref, o_ref, h_ref, *, nsub, d, shift_idx, scale_idx):
    @pl.when(pl.program_id(1) == 0)
    def _():
        y = _rms(x_ref[...], g_ref[...])
        sc = _mod_rows(mod_ref, nsub, scale_idx, d)
        sh = _mod_rows(mod_ref, nsub, shift_idx, d)
        h_ref[...] = _bf(y * (1.0 + sc) + sh)

    o_ref[...] = _bf(_dot(h_ref[...], w_ref[...]))


def _norm_mod_matmul(x, gain, modblk, w, shift_idx, scale_idx):
    n, d = x.shape
    nout = w.shape[1]
    tm = _pick_tile(n, (1024, 512, 256))
    tn = 1024
    nsub = tm // MOD_ROWS
    kern = functools.partial(_nmm_kernel, nsub=nsub, d=d, shift_idx=shift_idx, scale_idx=scale_idx)
    return pl.pallas_call(
        kern,
        grid=(n // tm, nout // tn),
        in_specs=[pl.BlockSpec((tm, d), lambda i, j: (i, 0)),
                  pl.BlockSpec((1, d), lambda i, j: (0, 0)),
                  pl.BlockSpec((nsub, 1, 6 * d), lambda i, j: (i, 0, 0)),
                  pl.BlockSpec((d, tn), lambda i, j: (0, j))],
        out_specs=pl.BlockSpec((tm, tn), lambda i, j: (i, j)),
        out_shape=jax.ShapeDtypeStruct((n, nout), BF16),
        scratch_shapes=[pltpu.VMEM((tm, d), BF16)],
        compiler_params=_cparams("parallel", "arbitrary"),
        name="norm_mod_in_proj",
    )(x, gain, modblk, w)


def _qkprep_kernel(q_ref, k_ref, v_ref, qg_ref, kg_ref, cos_ref, sin_ref, qo_ref, ko_ref, vo_ref):
    cos = cos_ref[...]
    sin = sin_ref[...]
    avg = _group_matrix(1.0 / HEAD_DIM)
    lane = lax.broadcasted_iota(jnp.int32, cos.shape, 1)
    first = (lane % 32) < 16
    left = lane < HEAD_DIM

    def norm_rope(x, g):
        ms = _dot_split(x * x, avg, 2)
        xn = x * lax.rsqrt(ms + EPS) * g
        partner = jnp.where(first, pltpu.roll(xn, LANES - 16, 1), pltpu.roll(xn, 16, 1))
        return xn * cos + partner * sin

    qscale = (HEAD_DIM ** -0.5) * float(np.log2(np.e))
    for j in range(q_ref.shape[1] // LANES):
        q = q_ref[:, j * LANES:(j + 1) * LANES].astype(F32)
        qo_ref[:, j * LANES:(j + 1) * LANES] = _bf(norm_rope(q, qg_ref[...]) * qscale)
    k = norm_rope(k_ref[...].astype(F32), kg_ref[...])
    zero = jnp.zeros_like(k)
    k0 = _bf(jnp.where(left, k, zero))
    k1 = _bf(jnp.where(left, zero, k))
    v = v_ref[...].astype(F32)
    ones = jnp.ones((ATT_VROWS - HEAD_DIM, ATT_TK), BF16)
    for c in range(vo_ref.shape[0]):
        rows = slice(c * ATT_TK, (c + 1) * ATT_TK)
        ko_ref[c, :ATT_TK, :] = k0[rows]
        ko_ref[c, ATT_TK:, :] = k1[rows]
        vt = _bf(v[rows, :].T)
        for t in range(2):
            vo_ref[c, t, :HEAD_DIM, :] = vt[t * HEAD_DIM:(t + 1) * HEAD_DIM]
            vo_ref[c, t, HEAD_DIM:, :] = ones


def _qk_prep(big, qg, kg, cos, sin, s_tot):
    n = big.shape[0]
    tm = MOD_ROWS
    npos = s_tot // tm
    qw = 512
    vchunks = tm // ATT_TK
    return pl.pallas_call(
        _qkprep_kernel,
        grid=(n // tm,),
        in_specs=[pl.BlockSpec((tm, qw), lambda i: (i, COL_Q // qw)),
                  pl.BlockSpec((tm, LANES), lambda i: (i, COL_K // LANES)),
                  pl.BlockSpec((tm, LANES), lambda i: (i, COL_V // LANES)),
                  pl.BlockSpec((1, LANES), lambda i: (0, 0)),
                  pl.BlockSpec((1, LANES), lambda i: (0, 0)),
                  pl.BlockSpec((tm, LANES), lambda i: (i % npos, 0)),
                  pl.BlockSpec((tm, LANES), lambda i: (i % npos, 0))],
        out_specs=[pl.BlockSpec((tm, qw), lambda i: (i, 0)),
                   pl.BlockSpec((vchunks, 2 * ATT_TK, LANES), lambda i: (i, 0, 0)),
                   pl.BlockSpec((vchunks, 2, ATT_VROWS, ATT_TK), lambda i: (i, 0, 0, 0))],
        out_shape=[jax.ShapeDtypeStruct((n, qw), BF16),
                   jax.ShapeDtypeStruct((n // ATT_TK, 2 * ATT_TK, LANES), BF16),
                   jax.ShapeDtypeStruct((n // ATT_TK, 2, ATT_VROWS, ATT_TK), BF16)],
        compiler_params=_cparams("parallel"),
        name="qk_norm_rope",
    )(big, big, big, qg, kg, cos, sin)


def _attn_kernel(q_ref, k_ref, vt_ref, o_ref, acc_ref, sa_ref, sb_ref, *, tq, tk, n_ctx_q, n_ctx_kv, n_kv):
    i = pl.program_id(1)
    nkv = jnp.where(i < n_ctx_q, n_ctx_kv, n_kv)
    hd = HEAD_DIM
    npair = q_ref.shape[1] // LANES
    nh = 2 * npair
    qs = [q_ref[:, j * LANES:(j + 1) * LANES] for j in range(npair)]
    vr = ATT_VROWS
    acc_ref[...] = jnp.zeros_like(acc_ref)

    def scores_to(dst_ref, c):
        kb = k_ref[c]
        for j in range(npair):
            sj = _dot_nt(kb, qs[j])
            dst_ref[2 * j] = sj[:tk]
            dst_ref[2 * j + 1] = sj[tk:]

    def consume(src_ref, c, m):
        new_m = []
        for h in range(nh):
            s = src_ref[h]
            n = jnp.maximum(m[h], jnp.max(s, axis=0, keepdims=True))
            p = _bf(jnp.exp2(s - n))
            rows = slice(h * vr, (h + 1) * vr)
            acc_ref[rows, :] = acc_ref[rows, :] * jnp.exp2(m[h] - n) + _dot(vt_ref[c, h % 2], p)
            new_m.append(n)
        return tuple(new_m)

    def body(u, m):
        c = 2 * u
        scores_to(sb_ref, c + 1)
        m = consume(sa_ref, c, m)
        scores_to(sa_ref, c + 2)
        return consume(sb_ref, c + 1, m)

    scores_to(sa_ref, 0)
    m = lax.fori_loop(0, (nkv - 1) // 2, body, (jnp.full((1, tq), -1e30, F32),) * nh)
    consume(sa_ref, nkv - 1, m)
    for j in range(npair):
        o = [acc_ref[h * vr:h * vr + hd, :] * (1.0 / acc_ref[h * vr + hd:h * vr + hd + 1, :]) for h in (2 * j, 2 * j + 1)]
        o_ref[:, j * LANES:(j + 1) * LANES] = _bf(jnp.concatenate(o, axis=0).T)


def _attention(qh, kbd, vt, batch, s_tot, c_len):
    n, qw = qh.shape
    tq = 256
    tk = ATT_TK
    nq = s_tot // tq
    assert (c_len // tk) % 2 == 1 and (s_tot // tk) % 2 == 1, "the key-chunk loop is unrolled by two plus a tail"
    kern = functools.partial(_attn_kernel, tq=tq, tk=tk, n_ctx_q=c_len // tq,
                             n_ctx_kv=c_len // tk, n_kv=s_tot // tk)
    return pl.pallas_call(
        kern,
        grid=(batch, nq),
        in_specs=[pl.BlockSpec((tq, qw), lambda b, i: (b * nq + i, 0)),
                  pl.BlockSpec((s_tot // tk, 2 * tk, LANES), lambda b, i: (b, 0, 0)),
                  pl.BlockSpec((s_tot // tk, 2, ATT_VROWS, tk), lambda b, i: (b, 0, 0, 0))],
        out_specs=pl.BlockSpec((tq, qw), lambda b, i: (b * nq + i, 0)),
        out_shape=jax.ShapeDtypeStruct((n, qw), BF16),
        scratch_shapes=[pltpu.VMEM((2 * (qw // LANES) * ATT_VROWS, tq), F32),
                        pltpu.VMEM((2 * (qw // LANES), tk, tq), F32),
                        pltpu.VMEM((2 * (qw // LANES), tk, tq), F32)],
        compiler_params=_cparams("parallel", "parallel"),
        name="gqa_attention",
    )(qh, kbd, vt)


def _cmlp_kernel(uv_ref, lng_ref, lnb_ref, ws_ref, bs_ref, o_ref, *, nchunk, width):
    x = uv_ref[...].astype(F32)
    g = 0.5 * x * (1.0 + jnp.tanh(0.7978845608028654 * (x + 0.044715 * (x * x * x))))
    u = g[:, :width]
    v = g[:, width:]
    mu = jnp.mean(v, axis=-1, keepdims=True)
    dv = v - mu
    var = jnp.mean(dv * dv, axis=-1, keepdims=True)
    vn = _bf(dv * lax.rsqrt(var + EPS) * lng_ref[...] + lnb_ref[...])
    ngroups = width // CMLP_CHUNK
    for c in range(nchunk):
        r0 = c * CMLP_CHUNK
        for gi in range(ngroups):
            c0 = gi * CMLP_CHUNK
            s = _dot(ws_ref[gi], vn[r0:r0 + CMLP_CHUNK, c0:c0 + CMLP_CHUNK]) + bs_ref[gi]
            o_ref[r0:r0 + CMLP_CHUNK, c0:c0 + CMLP_CHUNK] = _bf(u[r0:r0 + CMLP_CHUNK, c0:c0 + CMLP_CHUNK] * s)


def _chunk_mlp(big, ln_g, ln_b, ws, bs_b):
    n = big.shape[0]
    width = ln_g.shape[1]
    tr = _pick_tile(n, (512, 256, 128))
    kern = functools.partial(_cmlp_kernel, nchunk=tr // CMLP_CHUNK, width=width)
    ng = ws.shape[0]
    return pl.pallas_call(
        kern,
        grid=(n // tr,),
        in_specs=[pl.BlockSpec((tr, 2 * width), lambda i: (i, COL_UV // (2 * width))),
                  pl.BlockSpec((1, width), lambda i: (0, 0)),
                  pl.BlockSpec((1, width), lambda i: (0, 0)),
                  pl.BlockSpec((ng, CMLP_CHUNK, CMLP_CHUNK), lambda i: (0, 0, 0)),
                  pl.BlockSpec((ng, CMLP_CHUNK, CMLP_CHUNK), lambda i: (0, 0, 0))],
        out_specs=pl.BlockSpec((tr, width), lambda i: (i, 0)),
        out_shape=jax.ShapeDtypeStruct((n, width), BF16),
        compiler_params=_cparams("parallel"),
        name="chunk_gmlp",
    )(big, ln_g, ln_b, ws, bs_b)


def _rwkv_prep_kernel(x_ref, xp_ref, xn_ref, lo_ref, conv_ref, w0_ref, w2_ref, a0_ref, a2_ref,
                      kk0_ref, kk1_ref, rk_ref,
                      v_o, bonus_o, at_f, rt_f, bt_f, kt_f, bb_f, kb_f, pl_f,
                      at_b, rt_b, bt_b, kt_b, bb_b, kb_b, pl_b, *, tm, width, blocks_per_seq, ctx_blocks):
    i = pl.program_id(0)
    j = i % blocks_per_seq
    is_first = jnp.logical_or(j == 0, j == ctx_blocks)
    is_last = jnp.logical_or(j == ctx_blocks - 1, j == blocks_per_seq - 1)
    row = lax.broadcasted_iota(jnp.int32, (tm, width), 0)
    gsum = _group_matrix(1.0)
    halo = xp_ref.shape[0]

    def conv(c):
        cs = slice(c * width, (c + 1) * width)
        x = x_ref[:, cs].astype(F32)
        prev_row = jnp.where(is_first, 0.0, xp_ref[halo - 1:halo, cs].astype(F32))
        next_row = jnp.where(is_last, 0.0, xn_ref[0:1, cs].astype(F32))
        xprev = jnp.where(row == 0, prev_row, pltpu.roll(x, 1, 0))
        xnext = jnp.where(row == tm - 1, next_row, pltpu.roll(x, tm - 1, 0))
        return xprev * conv_ref[0:1, cs] + x * conv_ref[1:2, cs] + xnext * conv_ref[2:3, cs]

    r = conv(0)
    k = conv(1)
    v = conv(2)
    v_o[...] = _bf(v)

    def group_sum(x):
        parts = [_dot_split(x[:, c * LANES:(c + 1) * LANES], gsum, 2) for c in range(width // LANES)]
        return jnp.concatenate(parts, axis=1)

    kk = k * kk0_ref[...]
    kk = kk * lax.rsqrt(group_sum(kk * kk) + 1e-12)
    bonus_o[...] = _bf(group_sum(r * k * rk_ref[...]) * v)

    lo = lo_ref[...].astype(F32)
    wd = w0_ref[...] + _dot(_bf(jnp.tanh(lo[:, :LANES])), w2_ref[...])
    ad = jax.nn.sigmoid(a0_ref[...] + _dot(_bf(lo[:, LANES:]), a2_ref[...]))
    lw = -float(np.exp(-0.5)) * jax.nn.sigmoid(wd)

    r2 = lax.broadcasted_iota(jnp.int32, (tm, tm), 0)
    c2 = lax.broadcasted_iota(jnp.int32, (tm, tm), 1)
    same = (r2 // SCAN_CHUNK) == (c2 // SCAN_CHUNK)
    tri_pre = jnp.where(jnp.logical_and(same, c2 <= r2), 1.0, 0.0).astype(BF16)
    tri_suf = jnp.where(jnp.logical_and(same, c2 >= r2), 1.0, 0.0).astype(BF16)
    nchunk = tm // SCAN_CHUNK

    outs = ((at_f, rt_f, bt_f, kt_f, bb_f, kb_f, pl_f), (at_b, rt_b, bt_b, kt_b, bb_b, kb_b, pl_b))
    for d in range(2):
        ds_ = slice(d * width, (d + 1) * width)
        lwd = lw[:, ds_]
        pre = _dot_split_left(tri_pre, lwd, 3)
        suf = _dot_split_left(tri_suf, lwd, 3)
        cin, rem = (pre, suf - lwd) if d == 0 else (suf, pre - lwd)
        cex = cin - lwd
        a_d = ad[:, ds_]
        b = kk * a_d
        kd = k * (1.0 + (a_d - 1.0) * kk1_ref[...])
        at_o, rt_o, bt_o, kt_o, bb_o, kb_o, pl_o = outs[d]
        at_o[...] = _bf(-kk * jnp.exp(cex))
        rt_o[...] = _bf(r * jnp.exp(cin))
        pinv = jnp.exp(-cin)
        bt_o[...] = _bf(b * pinv)
        kt_o[...] = _bf(kd * pinv)
        pend = jnp.exp(rem)
        bb_o[...] = _bf(b * pend)
        kb_o[...] = _bf(kd * pend)
        for c in range(nchunk):
            last = (c + 1) * SCAN_CHUNK - 1
            pl_o[c] = jnp.exp(pre[last:last + 1, :])


def _rwkv_prep(big, conv_w, w0, w2s, a0, a2s, kk0, kk1, rk, s_tot, c_len):
    n = big.shape[0]
    width = rk.shape[1]
    tm = MOD_ROWS
    halo = 16
    hb = tm // halo
    nhalo = n // halo
    nchunk = tm // SCAN_CHUNK
    kern = functools.partial(_rwkv_prep_kernel, tm=tm, width=width, blocks_per_seq=s_tot // tm,
                             ctx_blocks=c_len // tm)
    tok = pl.BlockSpec((tm, width), lambda i: (i, 0))
    plspec = pl.BlockSpec((nchunk, 1, width), lambda i: (i, 0, 0))
    tok_shape = jax.ShapeDtypeStruct((n, width), BF16)
    pl_shape = jax.ShapeDtypeStruct((n // SCAN_CHUNK, 1, width), F32)
    full = lambda a: pl.BlockSpec(a.shape, lambda i: (0,) * a.ndim)
    return pl.pallas_call(
        kern,
        grid=(n // tm,),
        in_specs=[pl.BlockSpec((tm, 3 * width), lambda i: (i, 0)),
                  pl.BlockSpec((halo, 3 * width), lambda i: (jnp.maximum(i * hb - 1, 0), 0)),
                  pl.BlockSpec((halo, 3 * width), lambda i: (jnp.minimum((i + 1) * hb, nhalo - 1), 0)),
                  pl.BlockSpec((tm, 2 * LANES), lambda i: (i, COL_WLOW // (2 * LANES))),
                  full(conv_w), full(w0), full(w2s), full(a0), full(a2s), full(kk0), full(kk1), full(rk)],
        out_specs=[tok, tok] + [tok] * 6 + [plspec] + [tok] * 6 + [plspec],
        out_shape=[tok_shape, tok_shape] + [tok_shape] * 6 + [pl_shape] + [tok_shape] * 6 + [pl_shape],
        compiler_params=_cparams("parallel"),
        name="rwkv_prepare",
    )(big, big, big, big, conv_w, w0, w2s, a0, a2s, kk0, kk1, rk)


def _scan_chunks(chains):
    L = SCAN_CHUNK
    lane = lax.broadcasted_iota(jnp.int32, (L, LANES), 1)
    m0 = _bf(jnp.where(lane < HEAD_DIM, 1.0, 0.0))
    m1 = _bf(jnp.where(lane < HEAD_DIM, 0.0, 1.0))

    def stack(x):
        blocks = [x[:, c:c + LANES] for c in range(0, x.shape[1], LANES)]
        top = [b * m0 for b in blocks]
        bot = [b * m1 for b in blocks]
        if len(blocks) == 1:
            return jnp.concatenate([top[0], bot[0]], axis=0)
        return jnp.concatenate([jnp.concatenate(top, axis=1), jnp.concatenate(bot, axis=1)], axis=0)

    trow = lax.broadcasted_iota(jnp.int32, (L, LANES), 0)
    tcol = lax.broadcasted_iota(jnp.int32, (L, LANES), 1) % L
    masks = {True: (tcol < trow, tcol <= trow), False: (tcol > trow, tcol >= trow)}
    eye = lax.broadcasted_iota(jnp.int32, (LANES, LANES), 0) == lax.broadcasted_iota(jnp.int32, (LANES, LANES), 1)
    fwd = [ch[9] for ch in chains]
    nc = range(len(chains))

    v_s = [stack(ch[6]) for ch in chains]
    big1 = [_dot_nt(jnp.concatenate([chains[i][0], chains[i][1]], axis=0),
                    jnp.concatenate([stack(chains[i][2]), stack(chains[i][3])], axis=0)) for i in nc]
    pm = [jnp.where(masks[fwd[i]][0], big1[i][:L, :LANES], 0.0) for i in nc]
    mak = [_bf(jnp.where(masks[fwd[i]][0], big1[i][:L, LANES:], 0.0)) for i in nc]
    lhs_top = [_bf(jnp.where(jnp.concatenate([masks[fwd[i]][1]] * 2, axis=1), big1[i][L:], 0.0)) for i in nc]
    mv = [_dot(mak[i], v_s[i]) for i in nc]
    px = [jnp.concatenate([chains[i][0].astype(F32), mv[i]], axis=1) for i in nc]
    steps = int(np.log2(L))
    for it in range(steps):
        if it < steps - 1:
            res = [_dot(_bf(pm[i]), stack(_bf(jnp.concatenate([pm[i], px[i]], axis=1)))) for i in nc]
            px = [px[i] + res[i][:, LANES:] for i in nc]
            pm = [res[i][:, :LANES] for i in nc]
        else:
            res = [_dot(_bf(pm[i]), stack(_bf(px[i]))) for i in nc]
            px = [px[i] + res[i] for i in nc]
    rhs2 = [jnp.concatenate([stack(_bf(px[i])), jnp.concatenate([jnp.zeros_like(v_s[i]), v_s[i]], axis=1)], axis=0)
            for i in nc]
    lhs_bot = [_bf(jnp.concatenate([stack(chains[i][4]), stack(chains[i][5])], axis=0).astype(F32).T)
               for i in nc]
    res2 = [_dot(jnp.concatenate([lhs_top[i], lhs_bot[i]], axis=0), rhs2[i]) for i in nc]
    lhs3 = [_bf(jnp.concatenate(
        [chains[i][1].astype(F32) + res2[i][:L, :LANES],
         res2[i][L:, :LANES] + jnp.where(eye, jnp.broadcast_to(chains[i][7], (LANES, LANES)), 0.0)], axis=0))
        for i in nc]
    res3 = [_dot(lhs3[i], _bf(chains[i][8])) for i in nc]
    return [(res3[i][:L] + res2[i][:L, LANES:], res3[i][L:] + res2[i][L:, LANES:]) for i in nc]


def _rwkv_scan_kernel(v_f, at_f, rt_f, bt_f, kt_f, bb_f, kb_f, pl_f,
                      v_b, at_b, rt_b, bt_b, kt_b, bb_b, kb_b, pl_b,
                      yf_ref, yb_ref, z_ref, *, npairs):
    @pl.when(pl.program_id(1) == 0)
    def _():
        z_ref[...] = jnp.zeros_like(z_ref)

    dirs = ((v_f, at_f, rt_f, bt_f, kt_f, bb_f, kb_f, pl_f, yf_ref, True),
            (v_b, at_b, rt_b, bt_b, kt_b, bb_b, kb_b, pl_b, yb_ref, False))
    chains, dest = [], []
    for d, (v, at, rt, bt, kt, bb, kb, plr, y_ref, fwd) in enumerate(dirs):
        for p in range(npairs):
            cs = slice(p * LANES, (p + 1) * LANES)
            chains.append((at[:, cs], rt[:, cs], bt[:, cs], kt[:, cs], bb[:, cs], kb[:, cs],
                           v[:, cs], plr[0][:, cs], z_ref[d, p], fwd))
            dest.append((y_ref, cs, d, p))
    for (y_ref, cs, d, p), (y, znew) in zip(dest, _scan_chunks(chains)):
        y_ref[:, cs] = y
        z_ref[d, p] = znew


def _rwkv_scan(prep, batch, s_tot, c_len):
    (v, _bonus, at_f, rt_f, bt_f, kt_f, bb_f, kb_f, pl_f, at_b, rt_b, bt_b, kt_b, bb_b, kb_b, pl_b) = prep
    n, width = v.shape
    L = SCAN_CHUNK
    nch = s_tot // L
    ncc = c_len // L
    npairs = width // LANES

    def fmap(b, c):
        return (b * nch + c, 0)

    def bmap(b, c):
        return (b * nch + jnp.where(c < ncc, ncc - 1 - c, nch - 1 - (c - ncc)), 0)

    def tok(m):
        return pl.BlockSpec((L, width), m)

    def pls(m):
        return pl.BlockSpec((1, 1, width), lambda b, c: m(b, c) + (0,))

    kern = functools.partial(_rwkv_scan_kernel, npairs=npairs)
    return pl.pallas_call(
        kern,
        grid=(batch, nch),
        in_specs=[tok(fmap)] * 7 + [pls(fmap)] + [tok(bmap)] * 7 + [pls(bmap)],
        out_specs=[tok(fmap), tok(bmap)],
        out_shape=[jax.ShapeDtypeStruct((n, width), F32)] * 2,
        scratch_shapes=[pltpu.VMEM((2, npairs, LANES, LANES), F32)],
        compiler_params=_cparams("parallel", "arbitrary"),
        name="rwkv_scan",
    )(v, at_f, rt_f, bt_f, kt_f, bb_f, kb_f, pl_f, v, at_b, rt_b, bt_b, kt_b, bb_b, kb_b, pl_b)


def _rwkv_readout_kernel(yf_ref, yb_ref, bonus_ref, gl_ref, g2_ref, lg_ref, lb_ref, o_ref):
    avg = _group_matrix(1.0 / HEAD_DIM)
    gate = _dot(_bf(jax.nn.sigmoid(gl_ref[...].astype(F32))), g2_ref[...])
    for c in range(o_ref.shape[1] // LANES):
        cs = slice(c * LANES, (c + 1) * LANES)
        y = yf_ref[:, cs] + yb_ref[:, cs]
        mu = _dot_split(y, avg, 2)
        dy = y - mu
        var = _dot_split(dy * dy, avg, 2)
        yn = dy * lax.rsqrt(var + LNX_EPS) * lg_ref[:, cs] + lb_ref[:, cs]
        o_ref[:, cs] = _bf((yn + bonus_ref[:, cs].astype(F32)) * gate[:, cs])


def _rwkv_readout(yf, yb, bonus, big, g2, lnx_g, lnx_b):
    n, width = yf.shape
    tm = _pick_tile(n, (512, 256))
    tok = pl.BlockSpec((tm, width), lambda i: (i, 0))
    full = lambda a: pl.BlockSpec(a.shape, lambda i: (0,) * a.ndim)
    return pl.pallas_call(
        _rwkv_readout_kernel,
        grid=(n // tm,),
        in_specs=[tok, tok, tok, pl.BlockSpec((tm, LANES), lambda i: (i, COL_GLOW // LANES)),
                  full(g2), full(lnx_g), full(lnx_b)],
        out_specs=tok,
        out_shape=jax.ShapeDtypeStruct((n, width), BF16),
        compiler_params=_cparams("parallel"),
        name="rwkv_readout",
    )(yf, yb, bonus, big, g2, lnx_g, lnx_b)


def _merge_kernel(oa_ref, ob_ref, oc_ref, gt_ref, x_ref, wb_ref, wo_ref, ng_ref, mod_ref, o_ref, *, nsub, d):
    y = None
    for br, ref in enumerate((oa_ref, ob_ref, oc_ref)):
        g = jax.nn.sigmoid(gt_ref[:, br * d:(br + 1) * d].astype(F32))
        t = g * _dot(ref[...], wb_ref[br])
        y = t if y is None else y + t
    o = _dot(_bf(y), wo_ref[...])
    o_ref[...] = x_ref[...] + _mod_rows(mod_ref, nsub, 2, d) * _rms(o, ng_ref[...])


def _merge(oa, ob, oc, big, x, wb, wo, ng, modblk):
    n, d = x.shape
    width = oa.shape[1]
    tm = _pick_tile(n, (512, 256))
    nsub = tm // MOD_ROWS
    kern = functools.partial(_merge_kernel, nsub=nsub, d=d)
    br = pl.BlockSpec((tm, width), lambda i: (i, 0))
    return pl.pallas_call(
        kern,
        grid=(n // tm,),
        in_specs=[br, br, br,
                  pl.BlockSpec((tm, 3 * d), lambda i: (i, COL_GATES // (3 * d))),
                  pl.BlockSpec((tm, d), lambda i: (i, 0)),
                  pl.BlockSpec(wb.shape, lambda i: (0, 0, 0)),
                  pl.BlockSpec(wo.shape, lambda i: (0, 0)),
                  pl.BlockSpec((1, d), lambda i: (0, 0)),
                  pl.BlockSpec((nsub, 1, 6 * d), lambda i: (i, 0, 0))],
        out_specs=pl.BlockSpec((tm, d), lambda i: (i, 0)),
        out_shape=jax.ShapeDtypeStruct((n, d), F32),
        compiler_params=_cparams("parallel"),
        name="merge_out_proj",
    )(oa, ob, oc, big, x, wb, wo, ng, modblk)


def _ffn_kernel(x_ref, g_ref, mod_ref, wg_ref, wu_ref, wd_ref, ng_ref, o_ref, h_ref, acc_ref, *, nsub, d):
    j = pl.program_id(1)

    @pl.when(j == 0)
    def _():
        y = _rms(x_ref[...], g_ref[...])
        h_ref[...] = _bf(y * (1.0 + _mod_rows(mod_ref, nsub, 4, d)) + _mod_rows(mod_ref, nsub, 3, d))
        acc_ref[...] = jnp.zeros_like(acc_ref)

    h = h_ref[...]
    g = _dot(h, wg_ref[...])
    u = _dot(h, wu_ref[...])
    acc_ref[...] += _dot(_bf(g * jax.nn.sigmoid(g) * u), wd_ref[...])

    @pl.when(j == pl.num_programs(1) - 1)
    def _():
        o_ref[...] = x_ref[...] + _mod_rows(mod_ref, nsub, 5, d) * _rms(acc_ref[...], ng_ref[...])


def _dense_ffn(x, gain_in, gain_out, modblk, w_gu, w_down):
    n, d = x.shape
    f = w_down.shape[0]
    tm = _pick_tile(n, (1024, 512, 256))
    tf = _pick_tile(f, (1408, 1024, 512, 256, 128))
    nf = f // tf
    nsub = tm // MOD_ROWS
    kern = functools.partial(_ffn_kernel, nsub=nsub, d=d)
    return pl.pallas_call(
        kern,
        grid=(n // tm, nf),
        in_specs=[pl.BlockSpec((tm, d), lambda i, j: (i, 0)),
                  pl.BlockSpec((1, d), lambda i, j: (0, 0)),
                  pl.BlockSpec((nsub, 1, 6 * d), lambda i, j: (i, 0, 0)),
                  pl.BlockSpec((d, tf), lambda i, j: (0, j)),
                  pl.BlockSpec((d, tf), lambda i, j: (0, j + nf)),
                  pl.BlockSpec((tf, d), lambda i, j: (j, 0)),
                  pl.BlockSpec((1, d), lambda i, j: (0, 0))],
        out_specs=pl.BlockSpec((tm, d), lambda i, j: (i, 0)),
        out_shape=jax.ShapeDtypeStruct((n, d), F32),
        scratch_shapes=[pltpu.VMEM((tm, d), BF16), pltpu.VMEM((tm, d), F32)],
        compiler_params=_cparams("parallel", "arbitrary"),
        name="dense_swiglu_ffn",
    )(x, gain_in, modblk, w_gu, w_gu, w_down, gain_out)


def _router_kernel(x_ref, g_ref, mod_ref, wr_ref, h_ref, comb_ref, rank_ref, combt_ref, rankt_ref, cnt_ref,
                   *, nsub, d):
    lane = lax.broadcasted_iota(jnp.int32, (MOD_ROWS, LANES), 1)
    r2 = lax.broadcasted_iota(jnp.int32, (MOD_ROWS, MOD_ROWS), 0)
    c2 = lax.broadcasted_iota(jnp.int32, (MOD_ROWS, MOD_ROWS), 1)
    tri = jnp.where(c2 < r2, 1.0, 0.0).astype(BF16)
    ninf = jnp.float32(-jnp.inf)
    running = jnp.zeros((1, LANES), F32)
    for s in range(nsub):
        rows = slice(s * MOD_ROWS, (s + 1) * MOD_ROWS)
        m = mod_ref[s]
        h = _rms(x_ref[rows, :], g_ref[...]) * (1.0 + m[:, 4 * d:5 * d]) + m[:, 3 * d:4 * d]
        h_ref[rows, :] = _bf(h)
        logits = jnp.dot(h, wr_ref[...], precision=HIGHEST, preferred_element_type=F32)
        logits = jnp.where(lane < N_EXPERTS, logits, ninf)
        m1 = jnp.max(logits, axis=-1, keepdims=True)
        i1 = jnp.min(jnp.where(logits == m1, lane, LANES), axis=-1, keepdims=True)
        rest = jnp.where(lane == i1, ninf, logits)
        m2 = jnp.max(rest, axis=-1, keepdims=True)
        i2 = jnp.min(jnp.where(rest == m2, lane, LANES), axis=-1, keepdims=True)
        e2 = jnp.exp(m2 - m1)
        w1 = 1.0 / (1.0 + e2)
        comb = jnp.where(lane == i1, w1, 0.0) + jnp.where(lane == i2, e2 * w1, 0.0)
        ind = jnp.where(comb > 0.0, 1.0, 0.0)
        rank = _dot(tri, _bf(ind)) + running
        running = running + jnp.sum(ind, axis=0, keepdims=True)
        comb_ref[rows, :] = comb
        rank_ref[rows, :] = rank
        combt_ref[:, rows] = comb.T[:N_EXPERTS, :]
        rankt_ref[:, rows] = rank.T[:N_EXPERTS, :]
    cnt_ref[0] = running


def _router(x, gain_in, modblk, w_router_pad, tm):
    n, d = x.shape
    nsub = tm // MOD_ROWS
    nt = n // tm
    kern = functools.partial(_router_kernel, nsub=nsub, d=d)
    tokm = pl.BlockSpec((tm, LANES), lambda i: (i, 0))
    expm = pl.BlockSpec((N_EXPERTS, tm), lambda i: (0, i))
    return pl.pallas_call(
        kern,
        grid=(nt,),
        in_specs=[pl.BlockSpec((tm, d), lambda i: (i, 0)),
                  pl.BlockSpec((1, d), lambda i: (0, 0)),
                  pl.BlockSpec((nsub, 1, 6 * d), lambda i: (i, 0, 0)),
                  pl.BlockSpec((d, LANES), lambda i: (0, 0))],
        out_specs=[pl.BlockSpec((tm, d), lambda i: (i, 0)), tokm, tokm, expm, expm,
                   pl.BlockSpec((1, 1, LANES), lambda i: (i, 0, 0))],
        out_shape=[jax.ShapeDtypeStruct((n, d), BF16),
                   jax.ShapeDtypeStruct((n, LANES), F32), jax.ShapeDtypeStruct((n, LANES), F32),
                   jax.ShapeDtypeStruct((N_EXPERTS, n), F32), jax.ShapeDtypeStruct((N_EXPERTS, n), F32),
                   jax.ShapeDtypeStruct((nt, 1, LANES), F32)],
        compiler_params=_cparams("parallel"),
        name="moe_router",
    )(x, gain_in, modblk, w_router_pad)


def _moe_kernel(cnt_ref, h_ref, comb_ref, rank_ref, combt_ref, rankt_ref, wg_ref, wu_ref, wd_ref, o_ref,
                xe_ref, y_ref, acc_ref, *, tm, rb):
    i = pl.program_id(0)
    e = pl.program_id(1)
    j = pl.program_id(2)
    nf = pl.num_programs(2)
    cnt = cnt_ref[i * N_EXPERTS + e]
    half, quarter = rb // 2, rb // 4
    nblk = cnt // rb
    tail0 = pl.multiple_of(nblk * rb, rb)
    rem = cnt - tail0

    def for_blocks(fn):
        lax.fori_loop(0, nblk, lambda b, carry: fn(pl.multiple_of(b * rb, rb), rb) or carry, 0)

        @pl.when(rem > half + quarter)
        def _():
            fn(tail0, rb)

        @pl.when(jnp.logical_and(rem > quarter, rem <= half + quarter))
        def _():
            fn(tail0, half)

        @pl.when(jnp.logical_or(jnp.logical_and(rem > 0, rem <= quarter),
                                jnp.logical_and(rem > half, rem <= half + quarter)))
        def _():
            fn(pl.multiple_of(tail0 + jnp.where(rem > half, half, 0), quarter), quarter)

    @pl.when(jnp.logical_and(e == 0, j == 0))
    def _():
        acc_ref[...] = jnp.zeros_like(acc_ref)

    @pl.when(j == 0)
    def _():
        key = jnp.where(combt_ref[pl.ds(e, 1), :] > 0.0, rankt_ref[pl.ds(e, 1), :], -1.0)

        def gather(r0, nr):
            want = (r0 + lax.broadcasted_iota(jnp.int32, (nr, tm), 0)).astype(F32)
            sel = jnp.where(key == want, 1.0, 0.0).astype(BF16)
            xe_ref[pl.ds(r0, nr), :] = _bf(_dot(sel, h_ref[...]))

        for_blocks(gather)

    def expert(r0, nr):
        rows = pl.ds(r0, nr)
        xb = xe_ref[rows, :]
        g = _dot(xb, wg_ref[0])
        u = _dot(xb, wu_ref[0])
        part = _dot(_bf(g * jax.nn.sigmoid(g) * u), wd_ref[0])

        @pl.when(j == 0)
        def _():
            y_ref[rows, :] = part

        @pl.when(j > 0)
        def _():
            y_ref[rows, :] += part

    for_blocks(expert)

    @pl.when(j == nf - 1)
    def _():
        lane = lax.broadcasted_iota(jnp.int32, (tm, LANES), 1)
        rank_col = jnp.sum(jnp.where(lane == e, rank_ref[...], 0.0), axis=1, keepdims=True)
        w_col = jnp.sum(jnp.where(lane == e, comb_ref[...], 0.0), axis=1, keepdims=True)

        def scatter(r0, nr):
            want = (r0 + lax.broadcasted_iota(jnp.int32, (tm, nr), 1)).astype(F32)
            selw = _bf(jnp.where(rank_col == want, w_col, 0.0))
            acc_ref[...] += _dot(selw, _bf(y_ref[pl.ds(r0, nr), :]))

        for_blocks(scatter)

    @pl.when(jnp.logical_and(e == pl.num_programs(1) - 1, j == nf - 1))
    def _():
        o_ref[...] = _bf(acc_ref[...])


def _moe_ffn(hb, comb, rank, combt, rankt, counts, w_gu, w_down, tm):
    n, d = hb.shape
    ne, f, _ = w_down.shape
    tf = _pick_tile(f, (512, 256, 128))
    nf = f // tf
    rb = MOE_ROW_BLOCK
    kern = functools.partial(_moe_kernel, tm=tm, rb=rb)
    tokm = pl.BlockSpec((tm, LANES), lambda i, e, j, c: (i, 0))
    expm = pl.BlockSpec((N_EXPERTS, tm), lambda i, e, j, c: (0, i))
    grid_spec = pltpu.PrefetchScalarGridSpec(
        num_scalar_prefetch=1,
        grid=(n // tm, ne, nf),
        in_specs=[pl.BlockSpec((tm, d), lambda i, e, j, c: (i, 0)), tokm, tokm, expm, expm,
                  pl.BlockSpec((1, d, tf), lambda i, e, j, c: (e, 0, j)),
                  pl.BlockSpec((1, d, tf), lambda i, e, j, c: (e, 0, j + nf)),
                  pl.BlockSpec((1, tf, d), lambda i, e, j, c: (e, j, 0))],
        out_specs=pl.BlockSpec((tm, d), lambda i, e, j, c: (i, 0)),
        scratch_shapes=[pltpu.VMEM((tm, d), BF16), pltpu.VMEM((tm, d), F32), pltpu.VMEM((tm, d), F32)])
    return pl.pallas_call(
        kern,
        grid_spec=grid_spec,
        out_shape=jax.ShapeDtypeStruct((n, d), BF16),
        compiler_params=_cparams("parallel", "arbitrary", "arbitrary"),
        name="moe_swiglu_ffn",
    )(counts, hb, comb, rank, combt, rankt, w_gu, w_gu, w_down)


def _residual_kernel(x_ref, y_ref, ng_ref, mod_ref, o_ref, *, nsub, d):
    o_ref[...] = x_ref[...] + _mod_rows(mod_ref, nsub, 5, d) * _rms(y_ref[...].astype(F32), ng_ref[...])


def _gated_residual(x, y, gain_out, modblk):
    n, d = x.shape
    tm = _pick_tile(n, (512, 256))
    nsub = tm // MOD_ROWS
    tok = pl.BlockSpec((tm, d), lambda i: (i, 0))
    return pl.pallas_call(
        functools.partial(_residual_kernel, nsub=nsub, d=d),
        grid=(n // tm,),
        in_specs=[tok, tok, pl.BlockSpec((1, d), lambda i: (0, 0)),
                  pl.BlockSpec((nsub, 1, 6 * d), lambda i: (i, 0, 0))],
        out_specs=tok,
        out_shape=jax.ShapeDtypeStruct((n, d), F32),
        compiler_params=_cparams("parallel"),
        name="moe_gated_residual",
    )(x, y, gain_out, modblk)


def _rope_tables(t_len, c_len):
    pairs = HEAD_DIM // 4
    rows = t_len // GRID_W
    row = jnp.repeat(jnp.arange(rows, dtype=F32), GRID_W)
    col = jnp.tile(jnp.arange(GRID_W, dtype=F32), rows)
    freqs = ROPE_BASE ** (-jnp.arange(pairs, dtype=F32) / pairs)
    ar = row[:, None] * freqs
    ac = col[:, None] * freqs
    cos = jnp.concatenate([jnp.cos(ar), jnp.cos(ar), jnp.cos(ac), jnp.cos(ac)], axis=1)
    sin = jnp.concatenate([-jnp.sin(ar), jnp.sin(ar), -jnp.sin(ac), jnp.sin(ac)], axis=1)
    cos = jnp.concatenate([jnp.ones((c_len, HEAD_DIM), F32), cos], axis=0)
    sin = jnp.concatenate([jnp.zeros((c_len, HEAD_DIM), F32), sin], axis=0)
    return jnp.tile(cos, (1, 2)), jnp.tile(sin, (1, 2))


def _block_diag2(w):
    z = jnp.zeros_like(w[0])
    return jnp.concatenate([jnp.concatenate([w[0], z], axis=1), jnp.concatenate([z, w[1]], axis=1)], axis=0)


def kernel(x, c, ctx, c_ctx, w_mod, b_mod, norm_gain, w_in, qk_gain, rwkv_conv, decay_w0, decay_w2, iclr_a0, iclr_a2, key_k, bonus_rk, gate_g2, lnx_gain, lnx_bias, cmlp_ln_gain, cmlp_ln_bias, cmlp_ws, cmlp_bs, w_branch, w_out, ffn_w_gu, ffn_w_down, moe_router, moe_w_gu, moe_w_down):
    batch, t_len, d = x.shape
    c_len = ctx.shape[1]
    depth = w_mod.shape[0]
    s_tot = c_len + t_len
    n = batch * s_tot
    assert c_len % MOD_ROWS == 0 and t_len % MOD_ROWS == 0 and d % LANES == 0
    width = bonus_rk.shape[1] * bonus_rk.shape[2]
    nheads = width // HEAD_DIM

    xs = jnp.concatenate([ctx, x], axis=1).reshape(n, d)

    mod_rows = 8 * ((batch + 1 + 7) // 8)
    cvec = jnp.zeros((mod_rows, d), F32).at[0].set(c_ctx).at[1:batch + 1].set(c)
    mods = _modulation(cvec, w_mod, b_mod)
    mod_ctx = jnp.broadcast_to(mods[:, 0:1, None, :], (depth, batch, c_len // MOD_ROWS, 6 * d))
    mod_lat = jnp.broadcast_to(mods[:, 1:batch + 1, None, :], (depth, batch, t_len // MOD_ROWS, 6 * d))
    modblk_all = jnp.concatenate([mod_ctx, mod_lat], axis=2).reshape(depth, n // MOD_ROWS, 1, 6 * d)

    order = np.array(Q_HEAD_ORDER)
    nl = depth
    wq = w_in[:, :, 2048:2560].reshape(nl, d, nheads, HEAD_DIM)[:, :, order].reshape(nl, d, width)
    w_in_p = jnp.concatenate([
        w_in[:, :, 256:1792], wq, w_in[:, :, 2688:3712], w_in[:, :, 3712:6784],
        w_in[:, :, 0:128], w_in[:, :, 128:256], w_in[:, :, 1792:1920], w_in[:, :, 1920:2048],
        w_in[:, :, 2560:2688], jnp.zeros((nl, d, IN_PAD - 6784), F32)], axis=2).astype(BF16)
    wb = w_branch.astype(BF16)
    wb0 = wb[:, 0].reshape(nl, nheads, HEAD_DIM, d)[:, order].reshape(nl, width, d)
    wb = jnp.concatenate([wb0[:, None], wb[:, 1:]], axis=1)
    wo = w_out.astype(BF16)
    cos, sin = _rope_tables(t_len, c_len)
    qg = jnp.tile(qk_gain[:, 0], (1, 2))[:, None, :]
    kg = jnp.tile(qk_gain[:, 1], (1, 2))[:, None, :]
    ws_b = cmlp_ws.astype(BF16)
    bs_b = jnp.broadcast_to(cmlp_bs[..., None], cmlp_bs.shape + (CMLP_CHUNK,))
    w2s = jnp.stack([_block_diag2(decay_w2[l]) for l in range(nl)]).astype(BF16)
    a2s = jnp.stack([_block_diag2(iclr_a2[l]) for l in range(nl)]).astype(BF16)
    w0 = decay_w0.reshape(nl, 1, 2 * width)
    a0 = iclr_a0.reshape(nl, 1, 2 * width)
    rk = bonus_rk.reshape(nl, 1, width)
    g2 = gate_g2.astype(BF16)
    ffn_gu = ffn_w_gu.astype(BF16)
    ffn_dn = ffn_w_down.astype(BF16)
    moe_gu = moe_w_gu.astype(BF16)
    moe_dn = moe_w_down.astype(BF16)
    router_pad = jnp.pad(moe_router, ((0, 0), (0, 0), (0, LANES - moe_router.shape[2])))

    for l in range(depth):
        modblk = modblk_all[l]
        ng = norm_gain[l]
        big = _norm_mod_matmul(xs, ng[0:1], modblk, w_in_p[l], 0, 1)
        qh, kbd, vt = _qk_prep(big, qg[l], kg[l], cos, sin, s_tot)
        oa = _attention(qh, kbd, vt, batch, s_tot, c_len)
        prep = _rwkv_prep(big, rwkv_conv[l], w0[l], w2s[l], a0[l], a2s[l],
                          key_k[l, 0:1], key_k[l, 1:2], rk[l], s_tot, c_len)
        yf, yb = _rwkv_scan(prep, batch, s_tot, c_len)
        ob = _rwkv_readout(yf, yb, prep[1], big, g2[l], lnx_gain[l][None], lnx_bias[l][None])
        oc = _chunk_mlp(big, cmlp_ln_gain[l][None], cmlp_ln_bias[l][None], ws_b[l], bs_b[l])
        xs = _merge(oa, ob, oc, big, xs, wb[l], wo[l], ng[1:2], modblk)
        if l % 2 == 0:
            xs = _dense_ffn(xs, ng[2:3], ng[3:4], modblk, ffn_gu[l // 2], ffn_dn[l // 2])
        else:
            tmoe = _pick_tile(n, (2048, 1024, 512, 256))
            hb, comb, rank, combt, rankt, cnt = _router(xs, ng[2:3], modblk, router_pad[l // 2], tmoe)
            counts = cnt[:, 0, :N_EXPERTS].astype(jnp.int32).reshape(-1)
            y = _moe_ffn(hb, comb, rank, combt, rankt, counts, moe_gu[l // 2], moe_dn[l // 2], tmoe)
            xs = _gated_residual(xs, y, ng[3:4], modblk)
    return xs.reshape(batch, s_tot, d)[:, c_len:, :]
```

```python
import functools

import jax
import jax.numpy as jnp
import numpy as np
from jax import lax
from jax.experimental import pallas as pl
from jax.experimental.pallas import tpu as pltpu

F32 = jnp.float32
BF16 = jnp.bfloat16
HIGHEST = lax.Precision.HIGHEST

EPS = 1e-6
LNX_EPS = 64e-5
HEAD_DIM = 64
ROPE_BASE = 10000.0
GRID_W = 64
LANES = 128
MOD_ROWS = 256
SCAN_CHUNK = 64
CMLP_CHUNK = 128
ATT_TK = 256
ATT_VROWS = HEAD_DIM + 16
ATT_GROUP = 1
N_EXPERTS = 8
MOE_ROW_BLOCK = 512
VMEM_LIMIT = 56 * 1024 * 1024

COL_RKV, COL_Q, COL_UV, COL_GATES = 0, 1536, 2048, 3072
COL_K, COL_V, COL_WLOW, COL_ALOW, COL_GLOW = 6144, 6272, 6400, 6528, 6656
IN_PAD = 7168
Q_HEAD_ORDER = (0, 4, 1, 5, 2, 6, 3, 7)


def _cparams(*sem):
    return pltpu.CompilerParams(dimension_semantics=sem, vmem_limit_bytes=VMEM_LIMIT)


def _dot(a, b):
    return jnp.dot(a, b, preferred_element_type=F32)


def _dot_nt(a, b):
    return lax.dot_general(a, b, (((1,), (1,)), ((), ())), preferred_element_type=F32)


def _bf(x):
    return x.astype(BF16)


def _dot_split(a, b_exact, terms):
    acc = None
    rem = a
    for _ in range(terms):
        piece = _bf(rem)
        rem = rem - piece.astype(F32)
        part = _dot(piece, b_exact)
        acc = part if acc is None else acc + part
    return acc


def _dot_split_left(a_exact, b, terms):
    acc = None
    rem = b
    for _ in range(terms):
        piece = _bf(rem)
        rem = rem - piece.astype(F32)
        part = _dot(a_exact, piece)
        acc = part if acc is None else acc + part
    return acc


def _group_matrix(scale):
    r = lax.broadcasted_iota(jnp.int32, (LANES, LANES), 0) // HEAD_DIM
    c = lax.broadcasted_iota(jnp.int32, (LANES, LANES), 1) // HEAD_DIM
    return jnp.where(r == c, scale, 0.0).astype(BF16)


def _pick_tile(n, candidates):
    for t in candidates:
        if n % t == 0:
            return t
    raise ValueError(f"no tile in {candidates} divides {n}")


def _mod_rows(mod_ref, nsub, idx, d):
    parts = [jnp.broadcast_to(mod_ref[s][:, idx * d:(idx + 1) * d], (MOD_ROWS, d)) for s in range(nsub)]
    return parts[0] if nsub == 1 else jnp.concatenate(parts, axis=0)


def _rms(x, g):
    return x * lax.rsqrt(jnp.mean(x * x, axis=-1, keepdims=True) + EPS) * g


def _mod_kernel(c_ref, w_ref, b_ref, o_ref):
    cv = c_ref[...]
    s = cv * jax.nn.sigmoid(cv)
    o_ref[0] = jnp.dot(s, w_ref[0], precision=HIGHEST, preferred_element_type=F32) + b_ref[0]


def _modulation(cvec, w_mod, b_mod):
    nl, d, d6 = w_mod.shape
    rows = cvec.shape[0]
    tn = 1024
    return pl.pallas_call(
        _mod_kernel,
        grid=(nl, d6 // tn),
        in_specs=[pl.BlockSpec((rows, d), lambda l, j: (0, 0)),
                  pl.BlockSpec((1, d, tn), lambda l, j: (l, 0, j)),
                  pl.BlockSpec((1, 1, tn), lambda l, j: (l, 0, j))],
        out_specs=pl.BlockSpec((1, rows, tn), lambda l, j: (l, 0, j)),
        out_shape=jax.ShapeDtypeStruct((nl, rows, d6), F32),
        compiler_params=_cparams("parallel", "parallel"),
        name="modulation",
    )(cvec, w_mod, b_mod.reshape(nl, 1, d6))


def _nmm_kernel(x_ref, g_ref, mod_ref, w_ref, o_ref, h_ref, *, nsub, d, shift_idx, scale_idx):
    @pl.when(pl.program_id(1) == 0)
    def _():
        y = _rms(x_ref[...], g_ref[...])
        sc = _mod_rows(mod_ref, nsub, scale_idx, d)
        sh = _mod_rows(mod_ref, nsub, shift_idx, d)
        h_ref[...] = _bf(y * (1.0 + sc) + sh)

    o_ref[...] = _bf(_dot(h_ref[...], w_ref[...]))


def _norm_mod_matmul(x, gain, modblk, w, shift_idx, scale_idx):
    n, d = x.shape
    nout = w.shape[1]
    tm = _pick_tile(n, (1024, 512, 256))
    tn = 1024
    nsub = tm // MOD_ROWS
    kern = functools.partial(_nmm_kernel, nsub=nsub, d=d, shift_idx=shift_idx, scale_idx=scale_idx)
    return pl.pallas_call(
        kern,
        grid=(n // tm, nout // tn),
        in_specs=[pl.BlockSpec((tm, d), lambda i, j: (i, 0)),
                  pl.BlockSpec((1, d), lambda i, j: (0, 0)),
                  pl.BlockSpec((nsub, 1, 6 * d), lambda i, j: (i, 0, 0)),
                  pl.BlockSpec((d, tn), lambda i, j: (0, j))],
        out_specs=pl.BlockSpec((tm, tn), lambda i, j: (i, j)),
        out_shape=jax.ShapeDtypeStruct((n, nout), BF16),
        scratch_shapes=[pltpu.VMEM((tm, d), BF16)],
        compiler_params=_cparams("parallel", "arbitrary"),
        name="norm_mod_in_proj",
    )(x, gain, modblk, w)


def _qkprep_kernel(q_ref, k_ref, v_ref, qg_ref, kg_ref, cos_ref, sin_ref, qo_ref, ko_ref, vo_ref):
    cos = cos_ref[...]
    sin = sin_ref[...]
    avg = _group_matrix(1.0 / HEAD_DIM)
    lane = lax.broadcasted_iota(jnp.int32, cos.shape, 1)
    first = (lane % 32) < 16
    left = lane < HEAD_DIM

    def norm_rope(x, g):
        ms = _dot_split(x * x, avg, 2)
        xn = x * lax.rsqrt(ms + EPS) * g
        partner = jnp.where(first, pltpu.roll(xn, LANES - 16, 1), pltpu.roll(xn, 16, 1))
        return xn * cos + partner * sin

    qscale = (HEAD_DIM ** -0.5) * float(np.log2(np.e))
    for j in range(q_ref.shape[1] // LANES):
        q = q_ref[:, j * LANES:(j + 1) * LANES].astype(F32)
        qo_ref[:, j * LANES:(j + 1) * LANES] = _bf(norm_rope(q, qg_ref[...]) * qscale)
    k = norm_rope(k_ref[...].astype(F32), kg_ref[...])
    zero = jnp.zeros_like(k)
    k0 = _bf(jnp.where(left, k, zero))
    k1 = _bf(jnp.where(left, zero, k))
    v = v_ref[...].astype(F32)
    ones = jnp.ones((ATT_VROWS - HEAD_DIM, ATT_TK), BF16)
    for c in range(vo_ref.shape[0]):
        rows = slice(c * ATT_TK, (c + 1) * ATT_TK)
        ko_ref[c, :ATT_TK, :] = k0[rows]
        ko_ref[c, ATT_TK:, :] = k1[rows]
        vt = _bf(v[rows, :].T)
        for t in range(2):
            vo_ref[c, t, :HEAD_DIM, :] = vt[t * HEAD_DIM:(t + 1) * HEAD_DIM]
            vo_ref[c, t, HEAD_DIM:, :] = ones


def _qk_prep(big, qg, kg, cos, sin, s_tot):
    n = big.shape[0]
    tm = MOD_ROWS
    npos = s_tot // tm
    qw = 512
    vchunks = tm // ATT_TK
    return pl.pallas_call(
        _qkprep_kernel,
        grid=(n // tm,),
        in_specs=[pl.BlockSpec((tm, qw), lambda i: (i, COL_Q // qw)),
                  pl.BlockSpec((tm, LANES), lambda i: (i, COL_K // LANES)),
                  pl.BlockSpec((tm, LANES), lambda i: (i, COL_V // LANES)),
                  pl.BlockSpec((1, LANES), lambda i: (0, 0)),
                  pl.BlockSpec((1, LANES), lambda i: (0, 0)),
                  pl.BlockSpec((tm, LANES), lambda i: (i % npos, 0)),
                  pl.BlockSpec((tm, LANES), lambda i: (i % npos, 0))],
        out_specs=[pl.BlockSpec((tm, qw), lambda i: (i, 0)),
                   pl.BlockSpec((vchunks, 2 * ATT_TK, LANES), lambda i: (i, 0, 0)),
                   pl.BlockSpec((vchunks, 2, ATT_VROWS, ATT_TK), lambda i: (i, 0, 0, 0))],
        out_shape=[jax.ShapeDtypeStruct((n, qw), BF16),
                   jax.ShapeDtypeStruct((n // ATT_TK, 2 * ATT_TK, LANES), BF16),
                   jax.ShapeDtypeStruct((n // ATT_TK, 2, ATT_VROWS, ATT_TK), BF16)],
        compiler_params=_cparams("parallel"),
        name="qk_norm_rope",
    )(big, big, big, qg, kg, cos, sin)


def _attn_kernel(q_ref, k_ref, vt_ref, o_ref, acc_ref, sa_ref, sb_ref, *, tq, tk, n_ctx_q, n_ctx_kv, n_kv):
    i = pl.program_id(1)
    nkv = jnp.where(i < n_ctx_q, n_ctx_kv, n_kv)
    hd = HEAD_DIM
    npair = q_ref.shape[1] // LANES
    nh = 2 * npair
    qs = [q_ref[:, j * LANES:(j + 1) * LANES] for j in range(npair)]
    vr = ATT_VROWS
    acc_ref[...] = jnp.zeros_like(acc_ref)

    def scores_to(dst_ref, kb, h):
        s = _dot_nt(kb[(h % 2) * tk:(h % 2 + 1) * tk], qs[h // 2])
        dst_ref[h, :tk, :] = s
        dst_ref[h, tk:, :] = jnp.broadcast_to(jnp.max(s, axis=0, keepdims=True), (8, tq))

    def consume(src_ref, c, h, mh):
        n = jnp.maximum(mh, src_ref[h, tk:tk + 1, :])
        p = _bf(jnp.exp2(src_ref[h, :tk, :] - n))
        rows = slice(h * vr, (h + 1) * vr)
        acc_ref[rows, :] = acc_ref[rows, :] * jnp.exp2(mh - n) + _dot(vt_ref[c, h % 2], p)
        return n

    def step(src_ref, dst_ref, c, m, prefetch):
        kb = k_ref[c + 1] if prefetch else None
        new_m = []
        for h in range(nh):
            if prefetch:
                scores_to(dst_ref, kb, h)
            new_m.append(consume(src_ref, c, h, m[h]))
        return tuple(new_m)

    def body(u, m):
        c = 2 * u
        m = step(sa_ref, sb_ref, c, m, True)
        return step(sb_ref, sa_ref, c + 1, m, True)

    kb0 = k_ref[0]
    for h in range(nh):
        scores_to(sa_ref, kb0, h)
    m = lax.fori_loop(0, (nkv - 1) // 2, body, (jnp.full((1, tq), -1e30, F32),) * nh)
    step(sa_ref, sb_ref, nkv - 1, m, False)
    for j in range(npair):
        o = [acc_ref[h * vr:h * vr + hd, :] * (1.0 / acc_ref[h * vr + hd:h * vr + hd + 1, :]) for h in (2 * j, 2 * j + 1)]
        o_ref[:, j * LANES:(j + 1) * LANES] = _bf(jnp.concatenate(o, axis=0).T)


def _attention(qh, kbd, vt, batch, s_tot, c_len):
    n, qw = qh.shape
    tq = 256
    tk = ATT_TK
    nq = s_tot // tq
    assert (c_len // tk) % 2 == 1 and (s_tot // tk) % 2 == 1, "the key-chunk loop is unrolled by two plus a tail"
    kern = functools.partial(_attn_kernel, tq=tq, tk=tk, n_ctx_q=c_len // tq,
                             n_ctx_kv=c_len // tk, n_kv=s_tot // tk)
    return pl.pallas_call(
        kern,
        grid=(batch, nq),
        in_specs=[pl.BlockSpec((tq, qw), lambda b, i: (b * nq + i, 0)),
                  pl.BlockSpec((s_tot // tk, 2 * tk, LANES), lambda b, i: (b, 0, 0)),
                  pl.BlockSpec((s_tot // tk, 2, ATT_VROWS, tk), lambda b, i: (b, 0, 0, 0))],
        out_specs=pl.BlockSpec((tq, qw), lambda b, i: (b * nq + i, 0)),
        out_shape=jax.ShapeDtypeStruct((n, qw), BF16),
        scratch_shapes=[pltpu.VMEM((2 * (qw // LANES) * ATT_VROWS, tq), F32),
                        pltpu.VMEM((2 * (qw // LANES), tk + 8, tq), F32),
                        pltpu.VMEM((2 * (qw // LANES), tk + 8, tq), F32)],
        compiler_params=_cparams("parallel", "parallel"),
        name="gqa_attention",
    )(qh, kbd, vt)


def _cmlp_kernel(uv_ref, lng_ref, lnb_ref, ws_ref, bs_ref, o_ref, *, nchunk, width):
    x = uv_ref[...].astype(F32)
    g = 0.5 * x * (1.0 + jnp.tanh(0.7978845608028654 * (x + 0.044715 * (x * x * x))))
    u = g[:, :width]
    v = g[:, width:]
    mu = jnp.mean(v, axis=-1, keepdims=True)
    dv = v - mu
    var = jnp.mean(dv * dv, axis=-1, keepdims=True)
    vn = _bf(dv * lax.rsqrt(var + EPS) * lng_ref[...] + lnb_ref[...])
    ngroups = width // CMLP_CHUNK
    for c in range(nchunk):
        r0 = c * CMLP_CHUNK
        for gi in range(ngroups):
            c0 = gi * CMLP_CHUNK
            s = _dot(ws_ref[gi], vn[r0:r0 + CMLP_CHUNK, c0:c0 + CMLP_CHUNK]) + bs_ref[gi]
            o_ref[r0:r0 + CMLP_CHUNK, c0:c0 + CMLP_CHUNK] = _bf(u[r0:r0 + CMLP_CHUNK, c0:c0 + CMLP_CHUNK] * s)


def _chunk_mlp(big, ln_g, ln_b, ws, bs_b):
    n = big.shape[0]
    width = ln_g.shape[1]
    tr = _pick_tile(n, (512, 256, 128))
    kern = functools.partial(_cmlp_kernel, nchunk=tr // CMLP_CHUNK, width=width)
    ng = ws.shape[0]
    return pl.pallas_call(
        kern,
        grid=(n // tr,),
        in_specs=[pl.BlockSpec((tr, 2 * width), lambda i: (i, COL_UV // (2 * width))),
                  pl.BlockSpec((1, width), lambda i: (0, 0)),
                  pl.BlockSpec((1, width), lambda i: (0, 0)),
                  pl.BlockSpec((ng, CMLP_CHUNK, CMLP_CHUNK), lambda i: (0, 0, 0)),
                  pl.BlockSpec((ng, CMLP_CHUNK, CMLP_CHUNK), lambda i: (0, 0, 0))],
        out_specs=pl.BlockSpec((tr, width), lambda i: (i, 0)),
        out_shape=jax.ShapeDtypeStruct((n, width), BF16),
        compiler_params=_cparams("parallel"),
        name="chunk_gmlp",
    )(big, ln_g, ln_b, ws, bs_b)


def _rwkv_prep_kernel(x_ref, xp_ref, xn_ref, lo_ref, conv_ref, w0_ref, w2_ref, a0_ref, a2_ref,
                      kk0_ref, kk1_ref, rk_ref,
                      v_o, bonus_o, at_f, rt_f, bt_f, kt_f, bb_f, kb_f, pl_f,
                      at_b, rt_b, bt_b, kt_b, bb_b, kb_b, pl_b, *, tm, width, blocks_per_seq, ctx_blocks):
    i = pl.program_id(0)
    j = i % blocks_per_seq
    is_first = jnp.logical_or(j == 0, j == ctx_blocks)
    is_last = jnp.logical_or(j == ctx_blocks - 1, j == blocks_per_seq - 1)
    row = lax.broadcasted_iota(jnp.int32, (tm, width), 0)
    gsum = _group_matrix(1.0)
    halo = xp_ref.shape[0]

    def conv(c):
        cs = slice(c * width, (c + 1) * width)
        x = x_ref[:, cs].astype(F32)
        prev_row = jnp.where(is_first, 0.0, xp_ref[halo - 1:halo, cs].astype(F32))
        next_row = jnp.where(is_last, 0.0, xn_ref[0:1, cs].astype(F32))
        xprev = jnp.where(row == 0, prev_row, pltpu.roll(x, 1, 0))
        xnext = jnp.where(row == tm - 1, next_row, pltpu.roll(x, tm - 1, 0))
        return xprev * conv_ref[0:1, cs] + x * conv_ref[1:2, cs] + xnext * conv_ref[2:3, cs]

    r = conv(0)
    k = conv(1)
    v = conv(2)
    v_o[...] = _bf(v)

    def group_sum(x):
        parts = [_dot_split(x[:, c * LANES:(c + 1) * LANES], gsum, 2) for c in range(width // LANES)]
        return jnp.concatenate(parts, axis=1)

    kk = k * kk0_ref[...]
    kk = kk * lax.rsqrt(group_sum(kk * kk) + 1e-12)
    bonus_o[...] = _bf(group_sum(r * k * rk_ref[...]) * v)

    lo = lo_ref[...].astype(F32)
    wd = w0_ref[...] + _dot(_bf(jnp.tanh(lo[:, :LANES])), w2_ref[...])
    ad = jax.nn.sigmoid(a0_ref[...] + _dot(_bf(lo[:, LANES:]), a2_ref[...]))
    lw = -float(np.exp(-0.5)) * jax.nn.sigmoid(wd)

    r2 = lax.broadcasted_iota(jnp.int32, (tm, tm), 0)
    c2 = lax.broadcasted_iota(jnp.int32, (tm, tm), 1)
    same = (r2 // SCAN_CHUNK) == (c2 // SCAN_CHUNK)
    tri_pre = jnp.where(jnp.logical_and(same, c2 <= r2), 1.0, 0.0).astype(BF16)
    tri_suf = jnp.where(jnp.logical_and(same, c2 >= r2), 1.0, 0.0).astype(BF16)
    nchunk = tm // SCAN_CHUNK

    outs = ((at_f, rt_f, bt_f, kt_f, bb_f, kb_f, pl_f), (at_b, rt_b, bt_b, kt_b, bb_b, kb_b, pl_b))
    for d in range(2):
        ds_ = slice(d * width, (d + 1) * width)
        lwd = lw[:, ds_]
        pre = _dot_split_left(tri_pre, lwd, 3)
        suf = _dot_split_left(tri_suf, lwd, 3)
        cin, rem = (pre, suf - lwd) if d == 0 else (suf, pre - lwd)
        cex = cin - lwd
        a_d = ad[:, ds_]
        b = kk * a_d
        kd = k * (1.0 + (a_d - 1.0) * kk1_ref[...])
        at_o, rt_o, bt_o, kt_o, bb_o, kb_o, pl_o = outs[d]
        at_o[...] = _bf(-kk * jnp.exp(cex))
        rt_o[...] = _bf(r * jnp.exp(cin))
        pinv = jnp.exp(-cin)
        bt_o[...] = _bf(b * pinv)
        kt_o[...] = _bf(kd * pinv)
        pend = jnp.exp(rem)
        bb_o[...] = _bf(b * pend)
        kb_o[...] = _bf(kd * pend)
        for c in range(nchunk):
            last = (c + 1) * SCAN_CHUNK - 1
            pl_o[c] = jnp.exp(pre[last:last + 1, :])


def _rwkv_prep(big, conv_w, w0, w2s, a0, a2s, kk0, kk1, rk, s_tot, c_len):
    n = big.shape[0]
    width = rk.shape[1]
    tm = MOD_ROWS
    halo = 16
    hb = tm // halo
    nhalo = n // halo
    nchunk = tm // SCAN_CHUNK
    kern = functools.partial(_rwkv_prep_kernel, tm=tm, width=width, blocks_per_seq=s_tot // tm,
                             ctx_blocks=c_len // tm)
    tok = pl.BlockSpec((tm, width), lambda i: (i, 0))
    plspec = pl.BlockSpec((nchunk, 1, width), lambda i: (i, 0, 0))
    tok_shape = jax.ShapeDtypeStruct((n, width), BF16)
    pl_shape = jax.ShapeDtypeStruct((n // SCAN_CHUNK, 1, width), F32)
    full = lambda a: pl.BlockSpec(a.shape, lambda i: (0,) * a.ndim)
    return pl.pallas_call(
        kern,
        grid=(n // tm,),
        in_specs=[pl.BlockSpec((tm, 3 * width), lambda i: (i, 0)),
                  pl.BlockSpec((halo, 3 * width), lambda i: (jnp.maximum(i * hb - 1, 0), 0)),
                  pl.BlockSpec((halo, 3 * width), lambda i: (jnp.minimum((i + 1) * hb, nhalo - 1), 0)),
                  pl.BlockSpec((tm, 2 * LANES), lambda i: (i, COL_WLOW // (2 * LANES))),
                  full(conv_w), full(w0), full(w2s), full(a0), full(a2s), full(kk0), full(kk1), full(rk)],
        out_specs=[tok, tok] + [tok] * 6 + [plspec] + [tok] * 6 + [plspec],
        out_shape=[tok_shape, tok_shape] + [tok_shape] * 6 + [pl_shape] + [tok_shape] * 6 + [pl_shape],
        compiler_params=_cparams("parallel"),
        name="rwkv_prepare",
    )(big, big, big, big, conv_w, w0, w2s, a0, a2s, kk0, kk1, rk)


def _scan_chunks(chains):
    L = SCAN_CHUNK
    lane = lax.broadcasted_iota(jnp.int32, (L, LANES), 1)
    m0 = _bf(jnp.where(lane < HEAD_DIM, 1.0, 0.0))
    m1 = _bf(jnp.where(lane < HEAD_DIM, 0.0, 1.0))

    def stack(x):
        blocks = [x[:, c:c + LANES] for c in range(0, x.shape[1], LANES)]
        top = [b * m0 for b in blocks]
        bot = [b * m1 for b in blocks]
        if len(blocks) == 1:
            return jnp.concatenate([top[0], bot[0]], axis=0)
        return jnp.concatenate([jnp.concatenate(top, axis=1), jnp.concatenate(bot, axis=1)], axis=0)

    trow = lax.broadcasted_iota(jnp.int32, (L, LANES), 0)
    tcol = lax.broadcasted_iota(jnp.int32, (L, LANES), 1) % L
    masks = {True: (tcol < trow, tcol <= trow), False: (tcol > trow, tcol >= trow)}
    eye = lax.broadcasted_iota(jnp.int32, (LANES, LANES), 0) == lax.broadcasted_iota(jnp.int32, (LANES, LANES), 1)
    fwd = [ch[9] for ch in chains]
    nc = range(len(chains))

    v_s = [stack(ch[6]) for ch in chains]
    big1 = [_dot_nt(jnp.concatenate([chains[i][0], chains[i][1]], axis=0),
                    jnp.concatenate([stack(chains[i][2]), stack(chains[i][3])], axis=0)) for i in nc]
    pm = [jnp.where(masks[fwd[i]][0], big1[i][:L, :LANES], 0.0) for i in nc]
    mak = [_bf(jnp.where(masks[fwd[i]][0], big1[i][:L, LANES:], 0.0)) for i in nc]
    lhs_top = [_bf(jnp.where(jnp.concatenate([masks[fwd[i]][1]] * 2, axis=1), big1[i][L:], 0.0)) for i in nc]
    mv = [_dot(mak[i], v_s[i]) for i in nc]
    px = [jnp.concatenate([chains[i][0].astype(F32), mv[i]], axis=1) for i in nc]
    steps = int(np.log2(L))
    for it in range(steps):
        if it < steps - 1:
            res = [_dot(_bf(pm[i]), stack(_bf(jnp.concatenate([pm[i], px[i]], axis=1)))) for i in nc]
            px = [px[i] + res[i][:, LANES:] for i in nc]
            pm = [res[i][:, :LANES] for i in nc]
        else:
            res = [_dot(_bf(pm[i]), stack(_bf(px[i]))) for i in nc]
            px = [px[i] + res[i] for i in nc]
    rhs2 = [jnp.concatenate([stack(_bf(px[i])), jnp.concatenate([jnp.zeros_like(v_s[i]), v_s[i]], axis=1)], axis=0)
            for i in nc]
    lhs_bot = [_bf(jnp.concatenate([stack(chains[i][4]), stack(chains[i][5])], axis=0).astype(F32).T)
               for i in nc]
    res2 = [_dot(jnp.concatenate([lhs_top[i], lhs_bot[i]], axis=0), rhs2[i]) for i in nc]
    lhs3 = [_bf(jnp.concatenate(
        [chains[i][1].astype(F32) + res2[i][:L, :LANES],
         res2[i][L:, :LANES] + jnp.where(eye, jnp.broadcast_to(chains[i][7], (LANES, LANES)), 0.0)], axis=0))
        for i in nc]
    res3 = [_dot(lhs3[i], _bf(chains[i][8])) for i in nc]
    return [(res3[i][:L] + res2[i][:L, LANES:], res3[i][L:] + res2[i][L:, LANES:]) for i in nc]


def _rwkv_scan_kernel(v_f, at_f, rt_f, bt_f, kt_f, bb_f, kb_f, pl_f,
                      v_b, at_b, rt_b, bt_b, kt_b, bb_b, kb_b, pl_b,
                      yf_ref, yb_ref, z_ref, *, npairs, nb):
    @pl.when(pl.program_id(1) == 0)
    def _():
        z_ref[...] = jnp.zeros_like(z_ref)

    dirs = ((v_f, at_f, rt_f, bt_f, kt_f, bb_f, kb_f, pl_f, yf_ref, True),
            (v_b, at_b, rt_b, bt_b, kt_b, bb_b, kb_b, pl_b, yb_ref, False))
    chains, dest = [], []
    for s in range(nb):
        for d, (v, at, rt, bt, kt, bb, kb, plr, y_ref, fwd) in enumerate(dirs):
            for p in range(npairs):
                cs = slice(p * LANES, (p + 1) * LANES)
                chains.append((at[s, :, cs], rt[s, :, cs], bt[s, :, cs], kt[s, :, cs], bb[s, :, cs], kb[s, :, cs],
                               v[s, :, cs], plr[s, 0][:, cs], z_ref[s, d, p], fwd))
                dest.append((y_ref, s, cs, d, p))
    for (y_ref, s, cs, d, p), (y, znew) in zip(dest, _scan_chunks(chains)):
        y_ref[s, :, cs] = y
        z_ref[s, d, p] = znew


def _rwkv_scan(prep, batch, s_tot, c_len):
    (v, _bonus, at_f, rt_f, bt_f, kt_f, bb_f, kb_f, pl_f, at_b, rt_b, bt_b, kt_b, bb_b, kb_b, pl_b) = prep
    n, width = v.shape
    L = SCAN_CHUNK
    nch = s_tot // L
    ncc = c_len // L
    npairs = width // LANES
    nb = 2 if batch % 2 == 0 else 1

    def fchunk(c):
        return c

    def bchunk(c):
        return jnp.where(c < ncc, ncc - 1 - c, nch - 1 - (c - ncc))

    def tok(chunk):
        return pl.BlockSpec((nb, L, width), lambda b, c: (b, chunk(c), 0))

    def pls(chunk):
        return pl.BlockSpec((nb, 1, 1, width), lambda b, c: (b, chunk(c), 0, 0))

    def tok3(a):
        return a.reshape(batch, s_tot, width)

    def pl4(a):
        return a.reshape(batch, nch, 1, width)

    kern = functools.partial(_rwkv_scan_kernel, npairs=npairs, nb=nb)
    fwd_in = [tok3(a) for a in (v, at_f, rt_f, bt_f, kt_f, bb_f, kb_f)] + [pl4(pl_f)]
    bwd_in = [tok3(a) for a in (v, at_b, rt_b, bt_b, kt_b, bb_b, kb_b)] + [pl4(pl_b)]
    yf, yb = pl.pallas_call(
        kern,
        grid=(batch // nb, nch),
        in_specs=[tok(fchunk)] * 7 + [pls(fchunk)] + [tok(bchunk)] * 7 + [pls(bchunk)],
        out_specs=[tok(fchunk), tok(bchunk)],
        out_shape=[jax.ShapeDtypeStruct((batch, s_tot, width), F32)] * 2,
        scratch_shapes=[pltpu.VMEM((nb, 2, npairs, LANES, LANES), F32)],
        compiler_params=_cparams("parallel", "arbitrary"),
        name="rwkv_scan",
    )(*fwd_in, *bwd_in)
    return yf.reshape(n, width), yb.reshape(n, width)


def _rwkv_readout_kernel(yf_ref, yb_ref, bonus_ref, gl_ref, g2_ref, lg_ref, lb_ref, o_ref):
    avg = _group_matrix(1.0 / HEAD_DIM)
    gate = _dot(_bf(jax.nn.sigmoid(gl_ref[...].astype(F32))), g2_ref[...])
    for c in range(o_ref.shape[1] // LANES):
        cs = slice(c * LANES, (c + 1) * LANES)
        y = yf_ref[:, cs] + yb_ref[:, cs]
        mu = _dot_split(y, avg, 2)
        dy = y - mu
        var = _dot_split(dy * dy, avg, 2)
        yn = dy * lax.rsqrt(var + LNX_EPS) * lg_ref[:, cs] + lb_ref[:, cs]
        o_ref[:, cs] = _bf((yn + bonus_ref[:, cs].astype(F32)) * gate[:, cs])


def _rwkv_readout(yf, yb, bonus, big, g2, lnx_g, lnx_b):
    n, width = yf.shape
    tm = _pick_tile(n, (512, 256))
    tok = pl.BlockSpec((tm, width), lambda i: (i, 0))
    full = lambda a: pl.BlockSpec(a.shape, lambda i: (0,) * a.ndim)
    return pl.pallas_call(
        _rwkv_readout_kernel,
        grid=(n // tm,),
        in_specs=[tok, tok, tok, pl.BlockSpec((tm, LANES), lambda i: (i, COL_GLOW // LANES)),
                  full(g2), full(lnx_g), full(lnx_b)],
        out_specs=tok,
        out_shape=jax.ShapeDtypeStruct((n, width), BF16),
        compiler_params=_cparams("parallel"),
        name="rwkv_readout",
    )(yf, yb, bonus, big, g2, lnx_g, lnx_b)


def _merge_kernel(oa_ref, ob_ref, oc_ref, gt_ref, x_ref, wb_ref, wo_ref, ng_ref, mod_ref, o_ref, *, nsub, d):
    y = None
    for br, ref in enumerate((oa_ref, ob_ref, oc_ref)):
        g = jax.nn.sigmoid(gt_ref[:, br * d:(br + 1) * d].astype(F32))
        t = g * _dot(ref[...], wb_ref[br])
        y = t if y is None else y + t
    o = _dot(_bf(y), wo_ref[...])
    o_ref[...] = x_ref[...] + _mod_rows(mod_ref, nsub, 2, d) * _rms(o, ng_ref[...])


def _merge(oa, ob, oc, big, x, wb, wo, ng, modblk):
    n, d = x.shape
    width = oa.shape[1]
    tm = _pick_tile(n, (512, 256))
    nsub = tm // MOD_ROWS
    kern = functools.partial(_merge_kernel, nsub=nsub, d=d)
    br = pl.BlockSpec((tm, width), lambda i: (i, 0))
    return pl.pallas_call(
        kern,
        grid=(n // tm,),
        in_specs=[br, br, br,
                  pl.BlockSpec((tm, 3 * d), lambda i: (i, COL_GATES // (3 * d))),
                  pl.BlockSpec((tm, d), lambda i: (i, 0)),
                  pl.BlockSpec(wb.shape, lambda i: (0, 0, 0)),
                  pl.BlockSpec(wo.shape, lambda i: (0, 0)),
                  pl.BlockSpec((1, d), lambda i: (0, 0)),
                  pl.BlockSpec((nsub, 1, 6 * d), lambda i: (i, 0, 0))],
        out_specs=pl.BlockSpec((tm, d), lambda i: (i, 0)),
        out_shape=jax.ShapeDtypeStruct((n, d), F32),
        compiler_params=_cparams("parallel"),
        name="merge_out_proj",
    )(oa, ob, oc, big, x, wb, wo, ng, modblk)


def _ffn_kernel(x_ref, g_ref, mod_ref, wg_ref, wu_ref, wd_ref, ng_ref, o_ref, h_ref, acc_ref, *, nsub, d):
    j = pl.program_id(1)

    @pl.when(j == 0)
    def _():
        y = _rms(x_ref[...], g_ref[...])
        h_ref[...] = _bf(y * (1.0 + _mod_rows(mod_ref, nsub, 4, d)) + _mod_rows(mod_ref, nsub, 3, d))
        acc_ref[...] = jnp.zeros_like(acc_ref)

    h = h_ref[...]
    g = _dot(h, wg_ref[...])
    u = _dot(h, wu_ref[...])
    acc_ref[...] += _dot(_bf(g * jax.nn.sigmoid(g) * u), wd_ref[...])

    @pl.when(j == pl.num_programs(1) - 1)
    def _():
        o_ref[...] = x_ref[...] + _mod_rows(mod_ref, nsub, 5, d) * _rms(acc_ref[...], ng_ref[...])


def _dense_ffn(x, gain_in, gain_out, modblk, w_gu, w_down):
    n, d = x.shape
    f = w_down.shape[0]
    tm = _pick_tile(n, (1024, 512, 256))
    tf = _pick_tile(f, (1408, 1024, 512, 256, 128))
    nf = f // tf
    nsub = tm // MOD_ROWS
    kern = functools.partial(_ffn_kernel, nsub=nsub, d=d)
    return pl.pallas_call(
        kern,
        grid=(n // tm, nf),
        in_specs=[pl.BlockSpec((tm, d), lambda i, j: (i, 0)),
                  pl.BlockSpec((1, d), lambda i, j: (0, 0)),
                  pl.BlockSpec((nsub, 1, 6 * d), lambda i, j: (i, 0, 0)),
                  pl.BlockSpec((d, tf), lambda i, j: (0, j)),
                  pl.BlockSpec((d, tf), lambda i, j: (0, j + nf)),
                  pl.BlockSpec((tf, d), lambda i, j: (j, 0)),
                  pl.BlockSpec((1, d), lambda i, j: (0, 0))],
        out_specs=pl.BlockSpec((tm, d), lambda i, j: (i, 0)),
        out_shape=jax.ShapeDtypeStruct((n, d), F32),
        scratch_shapes=[pltpu.VMEM((tm, d), BF16), pltpu.VMEM((tm, d), F32)],
        compiler_params=_cparams("parallel", "arbitrary"),
        name="dense_swiglu_ffn",
    )(x, gain_in, modblk, w_gu, w_gu, w_down, gain_out)


def _router_kernel(x_ref, g_ref, mod_ref, wr_ref, h_ref, comb_ref, rank_ref, combt_ref, rankt_ref, cnt_ref,
                   *, nsub, d):
    lane = lax.broadcasted_iota(jnp.int32, (MOD_ROWS, LANES), 1)
    r2 = lax.broadcasted_iota(jnp.int32, (MOD_ROWS, MOD_ROWS), 0)
    c2 = lax.broadcasted_iota(jnp.int32, (MOD_ROWS, MOD_ROWS), 1)
    tri = jnp.where(c2 < r2, 1.0, 0.0).astype(BF16)
    ninf = jnp.float32(-jnp.inf)
    running = jnp.zeros((1, LANES), F32)
    for s in range(nsub):
        rows = slice(s * MOD_ROWS, (s + 1) * MOD_ROWS)
        m = mod_ref[s]
        h = _rms(x_ref[rows, :], g_ref[...]) * (1.0 + m[:, 4 * d:5 * d]) + m[:, 3 * d:4 * d]
        h_ref[rows, :] = _bf(h)
        logits = jnp.dot(h, wr_ref[...], precision=HIGHEST, preferred_element_type=F32)
        logits = jnp.where(lane < N_EXPERTS, logits, ninf)
        m1 = jnp.max(logits, axis=-1, keepdims=True)
        i1 = jnp.min(jnp.where(logits == m1, lane, LANES), axis=-1, keepdims=True)
        rest = jnp.where(lane == i1, ninf, logits)
        m2 = jnp.max(rest, axis=-1, keepdims=True)
        i2 = jnp.min(jnp.where(rest == m2, lane, LANES), axis=-1, keepdims=True)
        e2 = jnp.exp(m2 - m1)
        w1 = 1.0 / (1.0 + e2)
        comb = jnp.where(lane == i1, w1, 0.0) + jnp.where(lane == i2, e2 * w1, 0.0)
        ind = jnp.where(comb > 0.0, 1.0, 0.0)
        rank = _dot(tri, _bf(ind)) + running
        running = running + jnp.sum(ind, axis=0, keepdims=True)
        comb_ref[rows, :] = comb
        rank_ref[rows, :] = rank
        combt_ref[:, rows] = comb.T[:N_EXPERTS, :]
        rankt_ref[:, rows] = rank.T[:N_EXPERTS, :]
    cnt_ref[0] = running


def _router(x, gain_in, modblk, w_router_pad, tm):
    n, d = x.shape
    nsub = tm // MOD_ROWS
    nt = n // tm
    kern = functools.partial(_router_kernel, nsub=nsub, d=d)
    tokm = pl.BlockSpec((tm, LANES), lambda i: (i, 0))
    expm = pl.BlockSpec((N_EXPERTS, tm), lambda i: (0, i))
    return pl.pallas_call(
        kern,
        grid=(nt,),
        in_specs=[pl.BlockSpec((tm, d), lambda i: (i, 0)),
                  pl.BlockSpec((1, d), lambda i: (0, 0)),
                  pl.BlockSpec((nsub, 1, 6 * d), lambda i: (i, 0, 0)),
                  pl.BlockSpec((d, LANES), lambda i: (0, 0))],
        out_specs=[pl.BlockSpec((tm, d), lambda i: (i, 0)), tokm, tokm, expm, expm,
                   pl.BlockSpec((1, 1, LANES), lambda i: (i, 0, 0))],
        out_shape=[jax.ShapeDtypeStruct((n, d), BF16),
                   jax.ShapeDtypeStruct((n, LANES), F32), jax.ShapeDtypeStruct((n, LANES), F32),
                   jax.ShapeDtypeStruct((N_EXPERTS, n), F32), jax.ShapeDtypeStruct((N_EXPERTS, n), F32),
                   jax.ShapeDtypeStruct((nt, 1, LANES), F32)],
        compiler_params=_cparams("parallel"),
        name="moe_router",
    )(x, gain_in, modblk, w_router_pad)


def _moe_kernel(cnt_ref, h_ref, comb_ref, rank_ref, combt_ref, rankt_ref, wg_ref, wu_ref, wd_ref, o_ref,
                xe_ref, y_ref, acc_ref, *, tm, rb):
    i = pl.program_id(0)
    e = pl.program_id(1)
    j = pl.program_id(2)
    nf = pl.num_programs(2)
    cnt = cnt_ref[i * N_EXPERTS + e]
    half, quarter = rb // 2, rb // 4
    nblk = cnt // rb
    tail0 = pl.multiple_of(nblk * rb, rb)
    rem = cnt - tail0

    def for_blocks(fn):
        lax.fori_loop(0, nblk, lambda b, carry: fn(pl.multiple_of(b * rb, rb), rb) or carry, 0)

        @pl.when(rem > half + quarter)
        def _():
            fn(tail0, rb)

        @pl.when(jnp.logical_and(rem > quarter, rem <= half + quarter))
        def _():
            fn(tail0, half)

        @pl.when(jnp.logical_or(jnp.logical_and(rem > 0, rem <= quarter),
                                jnp.logical_and(rem > half, rem <= half + quarter)))
        def _():
            fn(pl.multiple_of(tail0 + jnp.where(rem > half, half, 0), quarter), quarter)

    @pl.when(jnp.logical_and(e == 0, j == 0))
    def _():
        acc_ref[...] = jnp.zeros_like(acc_ref)

    @pl.when(j == 0)
    def _():
        key = jnp.where(combt_ref[pl.ds(e, 1), :] > 0.0, rankt_ref[pl.ds(e, 1), :], -1.0)

        def gather(r0, nr):
            want = (r0 + lax.broadcasted_iota(jnp.int32, (nr, tm), 0)).astype(F32)
            sel = jnp.where(key == want, 1.0, 0.0).astype(BF16)
            xe_ref[pl.ds(r0, nr), :] = _bf(_dot(sel, h_ref[...]))

        for_blocks(gather)

    def expert(r0, nr):
        rows = pl.ds(r0, nr)
        xb = xe_ref[rows, :]
        g = _dot(xb, wg_ref[0])
        u = _dot(xb, wu_ref[0])
        part = _dot(_bf(g * jax.nn.sigmoid(g) * u), wd_ref[0])

        @pl.when(j == 0)
        def _():
            y_ref[rows, :] = part

        @pl.when(j > 0)
        def _():
            y_ref[rows, :] += part

    for_blocks(expert)

    @pl.when(j == nf - 1)
    def _():
        lane = lax.broadcasted_iota(jnp.int32, (tm, LANES), 1)
        rank_col = jnp.sum(jnp.where(lane == e, rank_ref[...], 0.0), axis=1, keepdims=True)
        w_col = jnp.sum(jnp.where(lane == e, comb_ref[...], 0.0), axis=1, keepdims=True)

        def scatter(r0, nr):
            want = (r0 + lax.broadcasted_iota(jnp.int32, (tm, nr), 1)).astype(F32)
            selw = _bf(jnp.where(rank_col == want, w_col, 0.0))
            acc_ref[...] += _dot(selw, _bf(y_ref[pl.ds(r0, nr), :]))

        for_blocks(scatter)

    @pl.when(jnp.logical_and(e == pl.num_programs(1) - 1, j == nf - 1))
    def _():
        o_ref[...] = _bf(acc_ref[...])


def _moe_ffn(hb, comb, rank, combt, rankt, counts, w_gu, w_down, tm):
    n, d = hb.shape
    ne, f, _ = w_down.shape
    tf = _pick_tile(f, (512, 256, 128))
    nf = f // tf
    rb = MOE_ROW_BLOCK
    kern = functools.partial(_moe_kernel, tm=tm, rb=rb)
    tokm = pl.BlockSpec((tm, LANES), lambda i, e, j, c: (i, 0))
    expm = pl.BlockSpec((N_EXPERTS, tm), lambda i, e, j, c: (0, i))
    grid_spec = pltpu.PrefetchScalarGridSpec(
        num_scalar_prefetch=1,
        grid=(n // tm, ne, nf),
        in_specs=[pl.BlockSpec((tm, d), lambda i, e, j, c: (i, 0)), tokm, tokm, expm, expm,
                  pl.BlockSpec((1, d, tf), lambda i, e, j, c: (e, 0, j)),
                  pl.BlockSpec((1, d, tf), lambda i, e, j, c: (e, 0, j + nf)),
                  pl.BlockSpec((1, tf, d), lambda i, e, j, c: (e, j, 0))],
        out_specs=pl.BlockSpec((tm, d), lambda i, e, j, c: (i, 0)),
        scratch_shapes=[pltpu.VMEM((tm, d), BF16), pltpu.VMEM((tm, d), F32), pltpu.VMEM((tm, d), F32)])
    return pl.pallas_call(
        kern,
        grid_spec=grid_spec,
        out_shape=jax.ShapeDtypeStruct((n, d), BF16),
        compiler_params=_cparams("parallel", "arbitrary", "arbitrary"),
        name="moe_swiglu_ffn",
    )(counts, hb, comb, rank, combt, rankt, w_gu, w_gu, w_down)


def _residual_kernel(x_ref, y_ref, ng_ref, mod_ref, o_ref, *, nsub, d):
    o_ref[...] = x_ref[...] + _mod_rows(mod_ref, nsub, 5, d) * _rms(y_ref[...].astype(F32), ng_ref[...])


def _gated_residual(x, y, gain_out, modblk):
    n, d = x.shape
    tm = _pick_tile(n, (512, 256))
    nsub = tm // MOD_ROWS
    tok = pl.BlockSpec((tm, d), lambda i: (i, 0))
    return pl.pallas_call(
        functools.partial(_residual_kernel, nsub=nsub, d=d),
        grid=(n // tm,),
        in_specs=[tok, tok, pl.BlockSpec((1, d), lambda i: (0, 0)),
                  pl.BlockSpec((nsub, 1, 6 * d), lambda i: (i, 0, 0))],
        out_specs=tok,
        out_shape=jax.ShapeDtypeStruct((n, d), F32),
        compiler_params=_cparams("parallel"),
        name="moe_gated_residual",
    )(x, y, gain_out, modblk)


def _rope_tables(t_len, c_len):
    pairs = HEAD_DIM // 4
    rows = t_len // GRID_W
    row = jnp.repeat(jnp.arange(rows, dtype=F32), GRID_W)
    col = jnp.tile(jnp.arange(GRID_W, dtype=F32), rows)
    freqs = ROPE_BASE ** (-jnp.arange(pairs, dtype=F32) / pairs)
    ar = row[:, None] * freqs
    ac = col[:, None] * freqs
    cos = jnp.concatenate([jnp.cos(ar), jnp.cos(ar), jnp.cos(ac), jnp.cos(ac)], axis=1)
    sin = jnp.concatenate([-jnp.sin(ar), jnp.sin(ar), -jnp.sin(ac), jnp.sin(ac)], axis=1)
    cos = jnp.concatenate([jnp.ones((c_len, HEAD_DIM), F32), cos], axis=0)
    sin = jnp.concatenate([jnp.zeros((c_len, HEAD_DIM), F32), sin], axis=0)
    return jnp.tile(cos, (1, 2)), jnp.tile(sin, (1, 2))


def _block_diag2(w):
    z = jnp.zeros_like(w[0])
    return jnp.concatenate([jnp.concatenate([w[0], z], axis=1), jnp.concatenate([z, w[1]], axis=1)], axis=0)


def kernel(x, c, ctx, c_ctx, w_mod, b_mod, norm_gain, w_in, qk_gain, rwkv_conv, decay_w0, decay_w2, iclr_a0, iclr_a2, key_k, bonus_rk, gate_g2, lnx_gain, lnx_bias, cmlp_ln_gain, cmlp_ln_bias, cmlp_ws, cmlp_bs, w_branch, w_out, ffn_w_gu, ffn_w_down, moe_router, moe_w_gu, moe_w_down):
    batch, t_len, d = x.shape
    c_len = ctx.shape[1]
    depth = w_mod.shape[0]
    s_tot = c_len + t_len
    n = batch * s_tot
    assert c_len % MOD_ROWS == 0 and t_len % MOD_ROWS == 0 and d % LANES == 0
    width = bonus_rk.shape[1] * bonus_rk.shape[2]
    nheads = width // HEAD_DIM

    xs = jnp.concatenate([ctx, x], axis=1).reshape(n, d)

    mod_rows = 8 * ((batch + 1 + 7) // 8)
    cvec = jnp.zeros((mod_rows, d), F32).at[0].set(c_ctx).at[1:batch + 1].set(c)
    mods = _modulation(cvec, w_mod, b_mod)
    mod_ctx = jnp.broadcast_to(mods[:, 0:1, None, :], (depth, batch, c_len // MOD_ROWS, 6 * d))
    mod_lat = jnp.broadcast_to(mods[:, 1:batch + 1, None, :], (depth, batch, t_len // MOD_ROWS, 6 * d))
    modblk_all = jnp.concatenate([mod_ctx, mod_lat], axis=2).reshape(depth, n // MOD_ROWS, 1, 6 * d)

    order = np.array(Q_HEAD_ORDER)
    nl = depth
    wq = w_in[:, :, 2048:2560].reshape(nl, d, nheads, HEAD_DIM)[:, :, order].reshape(nl, d, width)
    w_in_p = jnp.concatenate([
        w_in[:, :, 256:1792], wq, w_in[:, :, 2688:3712], w_in[:, :, 3712:6784],
        w_in[:, :, 0:128], w_in[:, :, 128:256], w_in[:, :, 1792:1920], w_in[:, :, 1920:2048],
        w_in[:, :, 2560:2688], jnp.zeros((nl, d, IN_PAD - 6784), F32)], axis=2).astype(BF16)
    wb = w_branch.astype(BF16)
    wb0 = wb[:, 0].reshape(nl, nheads, HEAD_DIM, d)[:, order].reshape(nl, width, d)
    wb = jnp.concatenate([wb0[:, None], wb[:, 1:]], axis=1)
    wo = w_out.astype(BF16)
    cos, sin = _rope_tables(t_len, c_len)
    qg = jnp.tile(qk_gain[:, 0], (1, 2))[:, None, :]
    kg = jnp.tile(qk_gain[:, 1], (1, 2))[:, None, :]
    ws_b = cmlp_ws.astype(BF16)
    bs_b = jnp.broadcast_to(cmlp_bs[..., None], cmlp_bs.shape + (CMLP_CHUNK,))
    w2s = jnp.stack([_block_diag2(decay_w2[l]) for l in range(nl)]).astype(BF16)
    a2s = jnp.stack([_block_diag2(iclr_a2[l]) for l in range(nl)]).astype(BF16)
    w0 = decay_w0.reshape(nl, 1, 2 * width)
    a0 = iclr_a0.reshape(nl, 1, 2 * width)
    rk = bonus_rk.reshape(nl, 1, width)
    g2 = gate_g2.astype(BF16)
    ffn_gu = ffn_w_gu.astype(BF16)
    ffn_dn = ffn_w_down.astype(BF16)
    moe_gu = moe_w_gu.astype(BF16)
    moe_dn = moe_w_down.astype(BF16)
    router_pad = jnp.pad(moe_router, ((0, 0), (0, 0), (0, LANES - moe_router.shape[2])))

    for l in range(depth):
        modblk = modblk_all[l]
        ng = norm_gain[l]
        big = _norm_mod_matmul(xs, ng[0:1], modblk, w_in_p[l], 0, 1)
        qh, kbd, vt = _qk_prep(big, qg[l], kg[l], cos, sin, s_tot)
        oa = _attention(qh, kbd, vt, batch, s_tot, c_len)
        prep = _rwkv_prep(big, rwkv_conv[l], w0[l], w2s[l], a0[l], a2s[l],
                          key_k[l, 0:1], key_k[l, 1:2], rk[l], s_tot, c_len)
        yf, yb = _rwkv_scan(prep, batch, s_tot, c_len)
        ob = _rwkv_readout(yf, yb, prep[1], big, g2[l], lnx_gain[l][None], lnx_bias[l][None])
        oc = _chunk_mlp(big, cmlp_ln_gain[l][None], cmlp_ln_bias[l][None], ws_b[l], bs_b[l])
        xs = _merge(oa, ob, oc, big, xs, wb[l], wo[l], ng[1:2], modblk)
        if l % 2 == 0:
            xs = _dense_ffn(xs, ng[2:3], ng[3:4], modblk, ffn_gu[l // 2], ffn_dn[l // 2])
        else:
            tmoe = _pick_tile(n, (2048, 1024, 512, 256))
            hb, comb, rank, combt, rankt, cnt = _router(xs, ng[2:3], modblk, router_pad[l // 2], tmoe)
            counts = cnt[:, 0, :N_EXPERTS].astype(jnp.int32).reshape(-1)
            y = _moe_ffn(hb, comb, rank, combt, rankt, counts, moe_gu[l // 2], moe_dn[l // 2], tmoe)
            xs = _gated_residual(xs, y, ng[3:4], modblk)
    return xs.reshape(batch, s_tot, d)[:, c_len:, :]
```

```python
import functools

import jax
import jax.numpy as jnp
import numpy as np
from jax import lax
from jax.experimental import pallas as pl
from jax.experimental.pallas import tpu as pltpu

F32 = jnp.float32
BF16 = jnp.bfloat16
HIGHEST = lax.Precision.HIGHEST

EPS = 1e-6
LNX_EPS = 64e-5
HEAD_DIM = 64
ROPE_BASE = 10000.0
GRID_W = 64
LANES = 128
MOD_ROWS = 256
SCAN_CHUNK = 64
CMLP_CHUNK = 128
ATT_TK = 256
ATT_VROWS = HEAD_DIM + 16
ATT_GROUP = 1
N_EXPERTS = 8
MOE_ROW_BLOCK = 512
VMEM_LIMIT = 56 * 1024 * 1024

COL_RKV, COL_Q, COL_UV, COL_GATES = 0, 1536, 2048, 3072
COL_K, COL_V, COL_WLOW, COL_ALOW, COL_GLOW = 6144, 6272, 6400, 6528, 6656
IN_PAD = 7168
Q_HEAD_ORDER = (0, 4, 1, 5, 2, 6, 3, 7)


def _cparams(*sem):
    return pltpu.CompilerParams(dimension_semantics=sem, vmem_limit_bytes=VMEM_LIMIT)


def _dot(a, b):
    return jnp.dot(a, b, preferred_element_type=F32)


def _dot_nt(a, b):
    return lax.dot_general(a, b, (((1,), (1,)), ((), ())), preferred_element_type=F32)


def _bf(x):
    return x.astype(BF16)


def _dot_split(a, b_exact, terms):
    acc = None
    rem = a
    for _ in range(terms):
        piece = _bf(rem)
        rem = rem - piece.astype(F32)
        part = _dot(piece, b_exact)
        acc = part if acc is None else acc + part
    return acc


def _dot_split_left(a_exact, b, terms):
    acc = None
    rem = b
    for _ in range(terms):
        piece = _bf(rem)
        rem = rem - piece.astype(F32)
        part = _dot(a_exact, piece)
        acc = part if acc is None else acc + part
    return acc


def _group_matrix(scale):
    r = lax.broadcasted_iota(jnp.int32, (LANES, LANES), 0) // HEAD_DIM
    c = lax.broadcasted_iota(jnp.int32, (LANES, LANES), 1) // HEAD_DIM
    return jnp.where(r == c, scale, 0.0).astype(BF16)


def _pick_tile(n, candidates):
    for t in candidates:
        if n % t == 0:
            return t
    raise ValueError(f"no tile in {candidates} divides {n}")


def _mod_rows(mod_ref, nsub, idx, d):
    parts = [jnp.broadcast_to(mod_ref[s][:, idx * d:(idx + 1) * d], (MOD_ROWS, d)) for s in range(nsub)]
    return parts[0] if nsub == 1 else jnp.concatenate(parts, axis=0)


def _rms(x, g):
    return x * lax.rsqrt(jnp.mean(x * x, axis=-1, keepdims=True) + EPS) * g


def _mod_kernel(c_ref, w_ref, b_ref, o_ref):
    cv = c_ref[...]
    s = cv * jax.nn.sigmoid(cv)
    o_ref[0] = jnp.dot(s, w_ref[0], precision=HIGHEST, preferred_element_type=F32) + b_ref[0]


def _modulation(cvec, w_mod, b_mod):
    nl, d, d6 = w_mod.shape
    rows = cvec.shape[0]
    tn = 1024
    return pl.pallas_call(
        _mod_kernel,
        grid=(nl, d6 // tn),
        in_specs=[pl.BlockSpec((rows, d), lambda l, j: (0, 0)),
                  pl.BlockSpec((1, d, tn), lambda l, j: (l, 0, j)),
                  pl.BlockSpec((1, 1, tn), lambda l, j: (l, 0, j))],
        out_specs=pl.BlockSpec((1, rows, tn), lambda l, j: (l, 0, j)),
        out_shape=jax.ShapeDtypeStruct((nl, rows, d6), F32),
        compiler_params=_cparams("parallel", "parallel"),
        name="modulation",
    )(cvec, w_mod, b_mod.reshape(nl, 1, d6))


def _nmm_kernel(x_ref, g_ref, mod_ref, w_ref, o_ref, h_ref, *, nsub, d, shift_idx, scale_idx):
    @pl.when(pl.program_id(1) == 0)
    def _():
        y = _rms(x_ref[...], g_ref[...])
        sc = _mod_rows(mod_ref, nsub, scale_idx, d)
        sh = _mod_rows(mod_ref, nsub, shift_idx, d)
        h_ref[...] = _bf(y * (1.0 + sc) + sh)

    o_ref[...] = _bf(_dot(h_ref[...], w_ref[...]))


def _norm_mod_matmul(x, gain, modblk, w, shift_idx, scale_idx):
    n, d = x.shape
    nout = w.shape[1]
    tm = _pick_tile(n, (1024, 512, 256))
    tn = 1024
    nsub = tm // MOD_ROWS
    kern = functools.partial(_nmm_kernel, nsub=nsub, d=d, shift_idx=shift_idx, scale_idx=scale_idx)
    return pl.pallas_call(
        kern,
        grid=(n // tm, nout // tn),
        in_specs=[pl.BlockSpec((tm, d), lambda i, j: (i, 0)),
                  pl.BlockSpec((1, d), lambda i, j: (0, 0)),
                  pl.BlockSpec((nsub, 1, 6 * d), lambda i, j: (i, 0, 0)),
                  pl.BlockSpec((d, tn), lambda i, j: (0, j))],
        out_specs=pl.BlockSpec((tm, tn), lambda i, j: (i, j)),
        out_shape=jax.ShapeDtypeStruct((n, nout), BF16),
        scratch_shapes=[pltpu.VMEM((tm, d), BF16)],
        compiler_params=_cparams("parallel", "arbitrary"),
        name="norm_mod_in_proj",
    )(x, gain, modblk, w)


def _qkprep_kernel(q_ref, k_ref, v_ref, qg_ref, kg_ref, cos_ref, sin_ref, qo_ref, ko_ref, vo_ref):
    cos = cos_ref[...]
    sin = sin_ref[...]
    avg = _group_matrix(1.0 / HEAD_DIM)
    lane = lax.broadcasted_iota(jnp.int32, cos.shape, 1)
    first = (lane % 32) < 16
    left = lane < HEAD_DIM

    def norm_rope(x, g):
        ms = _dot_split(x * x, avg, 2)
        xn = x * lax.rsqrt(ms + EPS) * g
        partner = jnp.where(first, pltpu.roll(xn, LANES - 16, 1), pltpu.roll(xn, 16, 1))
        return xn * cos + partner * sin

    qscale = (HEAD_DIM ** -0.5) * float(np.log2(np.e))
    for j in range(q_ref.shape[1] // LANES):
        q = q_ref[:, j * LANES:(j + 1) * LANES].astype(F32)
        qo_ref[:, j * LANES:(j + 1) * LANES] = _bf(norm_rope(q, qg_ref[...]) * qscale)
    k = norm_rope(k_ref[...].astype(F32), kg_ref[...])
    zero = jnp.zeros_like(k)
    k0 = _bf(jnp.where(left, k, zero))
    k1 = _bf(jnp.where(left, zero, k))
    v = v_ref[...].astype(F32)
    ones = jnp.ones((ATT_VROWS - HEAD_DIM, ATT_TK), BF16)
    for c in range(vo_ref.shape[0]):
        rows = slice(c * ATT_TK, (c + 1) * ATT_TK)
        ko_ref[c, :ATT_TK, :] = k0[rows]
        ko_ref[c, ATT_TK:, :] = k1[rows]
        vt = _bf(v[rows, :].T)
        for t in range(2):
            vo_ref[c, t, :HEAD_DIM, :] = vt[t * HEAD_DIM:(t + 1) * HEAD_DIM]
            vo_ref[c, t, HEAD_DIM:, :] = ones


def _qk_prep(big, qg, kg, cos, sin, s_tot):
    n = big.shape[0]
    tm = MOD_ROWS
    npos = s_tot // tm
    qw = 512
    vchunks = tm // ATT_TK
    return pl.pallas_call(
        _qkprep_kernel,
        grid=(n // tm,),
        in_specs=[pl.BlockSpec((tm, qw), lambda i: (i, COL_Q // qw)),
                  pl.BlockSpec((tm, LANES), lambda i: (i, COL_K // LANES)),
                  pl.BlockSpec((tm, LANES), lambda i: (i, COL_V // LANES)),
                  pl.BlockSpec((1, LANES), lambda i: (0, 0)),
                  pl.BlockSpec((1, LANES), lambda i: (0, 0)),
                  pl.BlockSpec((tm, LANES), lambda i: (i % npos, 0)),
                  pl.BlockSpec((tm, LANES), lambda i: (i % npos, 0))],
        out_specs=[pl.BlockSpec((tm, qw), lambda i: (i, 0)),
                   pl.BlockSpec((vchunks, 2 * ATT_TK, LANES), lambda i: (i, 0, 0)),
                   pl.BlockSpec((vchunks, 2, ATT_VROWS, ATT_TK), lambda i: (i, 0, 0, 0))],
        out_shape=[jax.ShapeDtypeStruct((n, qw), BF16),
                   jax.ShapeDtypeStruct((n // ATT_TK, 2 * ATT_TK, LANES), BF16),
                   jax.ShapeDtypeStruct((n // ATT_TK, 2, ATT_VROWS, ATT_TK), BF16)],
        compiler_params=_cparams("parallel"),
        name="qk_norm_rope",
    )(big, big, big, qg, kg, cos, sin)


def _attn_kernel(q_ref, k_ref, vt_ref, o_ref, acc_ref, sa_ref, sb_ref, *, tq, tk, n_ctx_q, n_ctx_kv, n_kv):
    i = pl.program_id(1)
    nkv = jnp.where(i < n_ctx_q, n_ctx_kv, n_kv)
    hd = HEAD_DIM
    npair = q_ref.shape[1] // LANES
    nh = 2 * npair
    qs = [q_ref[:, j * LANES:(j + 1) * LANES] for j in range(npair)]
    vr = ATT_VROWS
    acc_ref[...] = jnp.zeros_like(acc_ref)

    def scores_to(dst_ref, kb, h):
        s = _dot_nt(kb[(h % 2) * tk:(h % 2 + 1) * tk], qs[h // 2])
        dst_ref[h, :tk, :] = s
        dst_ref[h, tk:, :] = jnp.broadcast_to(jnp.max(s, axis=0, keepdims=True), (8, tq))

    def consume(src_ref, c, h, mh):
        n = jnp.maximum(mh, src_ref[h, tk:tk + 1, :])
        p = _bf(jnp.exp2(src_ref[h, :tk, :] - n))
        rows = slice(h * vr, (h + 1) * vr)
        acc_ref[rows, :] = acc_ref[rows, :] * jnp.exp2(mh - n) + _dot(vt_ref[c, h % 2], p)
        return n

    def step(src_ref, dst_ref, c, m, prefetch):
        kb = k_ref[c + 1] if prefetch else None
        new_m = []
        for h in range(nh):
            if prefetch:
                scores_to(dst_ref, kb, h)
            new_m.append(consume(src_ref, c, h, m[h]))
        return tuple(new_m)

    def body(u, m):
        c = 2 * u
        m = step(sa_ref, sb_ref, c, m, True)
        return step(sb_ref, sa_ref, c + 1, m, True)

    kb0 = k_ref[0]
    for h in range(nh):
        scores_to(sa_ref, kb0, h)
    m = lax.fori_loop(0, (nkv - 1) // 2, body, (jnp.full((1, tq), -1e30, F32),) * nh)
    step(sa_ref, sb_ref, nkv - 1, m, False)
    for j in range(npair):
        o = [acc_ref[h * vr:h * vr + hd, :] * (1.0 / acc_ref[h * vr + hd:h * vr + hd + 1, :]) for h in (2 * j, 2 * j + 1)]
        o_ref[:, j * LANES:(j + 1) * LANES] = _bf(jnp.concatenate(o, axis=0).T)


def _attention(qh, kbd, vt, batch, s_tot, c_len):
    n, qw = qh.shape
    tq = 256
    tk = ATT_TK
    nq = s_tot // tq
    assert (c_len // tk) % 2 == 1 and (s_tot // tk) % 2 == 1, "the key-chunk loop is unrolled by two plus a tail"
    kern = functools.partial(_attn_kernel, tq=tq, tk=tk, n_ctx_q=c_len // tq,
                             n_ctx_kv=c_len // tk, n_kv=s_tot // tk)
    return pl.pallas_call(
        kern,
        grid=(batch, nq),
        in_specs=[pl.BlockSpec((tq, qw), lambda b, i: (b * nq + i, 0)),
                  pl.BlockSpec((s_tot // tk, 2 * tk, LANES), lambda b, i: (b, 0, 0)),
                  pl.BlockSpec((s_tot // tk, 2, ATT_VROWS, tk), lambda b, i: (b, 0, 0, 0))],
        out_specs=pl.BlockSpec((tq, qw), lambda b, i: (b * nq + i, 0)),
        out_shape=jax.ShapeDtypeStruct((n, qw), BF16),
        scratch_shapes=[pltpu.VMEM((2 * (qw // LANES) * ATT_VROWS, tq), F32),
                        pltpu.VMEM((2 * (qw // LANES), tk + 8, tq), F32),
                        pltpu.VMEM((2 * (qw // LANES), tk + 8, tq), F32)],
        compiler_params=_cparams("parallel", "parallel"),
        name="gqa_attention",
    )(qh, kbd, vt)


def _cmlp_kernel(uv_ref, lng_ref, lnb_ref, ws_ref, bs_ref, o_ref, *, nchunk, width):
    x = uv_ref[...].astype(F32)
    g = 0.5 * x * (1.0 + jnp.tanh(0.7978845608028654 * (x + 0.044715 * (x * x * x))))
    u = g[:, :width]
    v = g[:, width:]
    mu = jnp.mean(v, axis=-1, keepdims=True)
    dv = v - mu
    var = jnp.mean(dv * dv, axis=-1, keepdims=True)
    vn = _bf(dv * lax.rsqrt(var + EPS) * lng_ref[...] + lnb_ref[...])
    ngroups = width // CMLP_CHUNK
    for c in range(nchunk):
        r0 = c * CMLP_CHUNK
        for gi in range(ngroups):
            c0 = gi * CMLP_CHUNK
            s = _dot(ws_ref[gi], vn[r0:r0 + CMLP_CHUNK, c0:c0 + CMLP_CHUNK]) + bs_ref[gi]
            o_ref[r0:r0 + CMLP_CHUNK, c0:c0 + CMLP_CHUNK] = _bf(u[r0:r0 + CMLP_CHUNK, c0:c0 + CMLP_CHUNK] * s)


def _chunk_mlp(big, ln_g, ln_b, ws, bs_b):
    n = big.shape[0]
    width = ln_g.shape[1]
    tr = _pick_tile(n, (512, 256, 128))
    kern = functools.partial(_cmlp_kernel, nchunk=tr // CMLP_CHUNK, width=width)
    ng = ws.shape[0]
    return pl.pallas_call(
        kern,
        grid=(n // tr,),
        in_specs=[pl.BlockSpec((tr, 2 * width), lambda i: (i, COL_UV // (2 * width))),
                  pl.BlockSpec((1, width), lambda i: (0, 0)),
                  pl.BlockSpec((1, width), lambda i: (0, 0)),
                  pl.BlockSpec((ng, CMLP_CHUNK, CMLP_CHUNK), lambda i: (0, 0, 0)),
                  pl.BlockSpec((ng, CMLP_CHUNK, CMLP_CHUNK), lambda i: (0, 0, 0))],
        out_specs=pl.BlockSpec((tr, width), lambda i: (i, 0)),
        out_shape=jax.ShapeDtypeStruct((n, width), BF16),
        compiler_params=_cparams("parallel"),
        name="chunk_gmlp",
    )(big, ln_g, ln_b, ws, bs_b)


def _rwkv_prep_kernel(x_ref, xp_ref, xn_ref, lo_ref, conv_ref, w0_ref, w2_ref, a0_ref, a2_ref,
                      kk0_ref, kk1_ref, rk_ref,
                      v_o, bonus_o, at_f, rt_f, bt_f, kt_f, bb_f, kb_f, pl_f,
                      at_b, rt_b, bt_b, kt_b, bb_b, kb_b, pl_b, *, tm, width, blocks_per_seq, ctx_blocks):
    i = pl.program_id(0)
    j = i % blocks_per_seq
    is_first = jnp.logical_or(j == 0, j == ctx_blocks)
    is_last = jnp.logical_or(j == ctx_blocks - 1, j == blocks_per_seq - 1)
    row = lax.broadcasted_iota(jnp.int32, (tm, width), 0)
    gsum = _group_matrix(1.0)
    halo = xp_ref.shape[0]

    def conv(c):
        cs = slice(c * width, (c + 1) * width)
        x = x_ref[:, cs].astype(F32)
        prev_row = jnp.where(is_first, 0.0, xp_ref[halo - 1:halo, cs].astype(F32))
        next_row = jnp.where(is_last, 0.0, xn_ref[0:1, cs].astype(F32))
        xprev = jnp.where(row == 0, prev_row, pltpu.roll(x, 1, 0))
        xnext = jnp.where(row == tm - 1, next_row, pltpu.roll(x, tm - 1, 0))
        return xprev * conv_ref[0:1, cs] + x * conv_ref[1:2, cs] + xnext * conv_ref[2:3, cs]

    r = conv(0)
    k = conv(1)
    v = conv(2)
    v_o[...] = _bf(v)

    def group_sum(x):
        parts = [_dot_split(x[:, c * LANES:(c + 1) * LANES], gsum, 2) for c in range(width // LANES)]
        return jnp.concatenate(parts, axis=1)

    kk = k * kk0_ref[...]
    kk = kk * lax.rsqrt(group_sum(kk * kk) + 1e-12)
    bonus_o[...] = _bf(group_sum(r * k * rk_ref[...]) * v)

    lo = lo_ref[...].astype(F32)
    wd = w0_ref[...] + _dot(_bf(jnp.tanh(lo[:, :LANES])), w2_ref[...])
    ad = jax.nn.sigmoid(a0_ref[...] + _dot(_bf(lo[:, LANES:]), a2_ref[...]))
    lw = -float(np.exp(-0.5) * np.log2(np.e)) * jax.nn.sigmoid(wd)

    r2 = lax.broadcasted_iota(jnp.int32, (tm, tm), 0)
    c2 = lax.broadcasted_iota(jnp.int32, (tm, tm), 1)
    same = (r2 // SCAN_CHUNK) == (c2 // SCAN_CHUNK)
    tri_pre = jnp.where(jnp.logical_and(same, c2 <= r2), 1.0, 0.0).astype(BF16)
    tri_suf = jnp.where(jnp.logical_and(same, c2 >= r2), 1.0, 0.0).astype(BF16)
    nchunk = tm // SCAN_CHUNK

    outs = ((at_f, rt_f, bt_f, kt_f, bb_f, kb_f, pl_f), (at_b, rt_b, bt_b, kt_b, bb_b, kb_b, pl_b))
    for d in range(2):
        ds_ = slice(d * width, (d + 1) * width)
        lwd = lw[:, ds_]
        pre = _dot_split_left(tri_pre, lwd, 3)
        suf = _dot_split_left(tri_suf, lwd, 3)
        cin, rem = (pre, suf - lwd) if d == 0 else (suf, pre - lwd)
        cex = cin - lwd
        a_d = ad[:, ds_]
        b = kk * a_d
        kd = k * (1.0 + (a_d - 1.0) * kk1_ref[...])
        at_o, rt_o, bt_o, kt_o, bb_o, kb_o, pl_o = outs[d]
        at_o[...] = _bf(-kk * jnp.exp2(cex))
        rt_o[...] = _bf(r * jnp.exp2(cin))
        pinv = jnp.exp2(-cin)
        bt_o[...] = _bf(b * pinv)
        kt_o[...] = _bf(kd * pinv)
        pend = jnp.exp2(rem)
        bb_o[...] = _bf(b * pend)
        kb_o[...] = _bf(kd * pend)
        for c in range(nchunk):
            last = (c + 1) * SCAN_CHUNK - 1
            pl_o[c] = jnp.exp2(pre[last:last + 1, :])


def _rwkv_prep(big, conv_w, w0, w2s, a0, a2s, kk0, kk1, rk, s_tot, c_len):
    n = big.shape[0]
    width = rk.shape[1]
    tm = MOD_ROWS
    halo = 16
    hb = tm // halo
    nhalo = n // halo
    nchunk = tm // SCAN_CHUNK
    kern = functools.partial(_rwkv_prep_kernel, tm=tm, width=width, blocks_per_seq=s_tot // tm,
                             ctx_blocks=c_len // tm)
    tok = pl.BlockSpec((tm, width), lambda i: (i, 0))
    plspec = pl.BlockSpec((nchunk, 1, width), lambda i: (i, 0, 0))
    tok_shape = jax.ShapeDtypeStruct((n, width), BF16)
    pl_shape = jax.ShapeDtypeStruct((n // SCAN_CHUNK, 1, width), F32)
    full = lambda a: pl.BlockSpec(a.shape, lambda i: (0,) * a.ndim)
    return pl.pallas_call(
        kern,
        grid=(n // tm,),
        in_specs=[pl.BlockSpec((tm, 3 * width), lambda i: (i, 0)),
                  pl.BlockSpec((halo, 3 * width), lambda i: (jnp.maximum(i * hb - 1, 0), 0)),
                  pl.BlockSpec((halo, 3 * width), lambda i: (jnp.minimum((i + 1) * hb, nhalo - 1), 0)),
                  pl.BlockSpec((tm, 2 * LANES), lambda i: (i, COL_WLOW // (2 * LANES))),
                  full(conv_w), full(w0), full(w2s), full(a0), full(a2s), full(kk0), full(kk1), full(rk)],
        out_specs=[tok, tok] + [tok] * 6 + [plspec] + [tok] * 6 + [plspec],
        out_shape=[tok_shape, tok_shape] + [tok_shape] * 6 + [pl_shape] + [tok_shape] * 6 + [pl_shape],
        compiler_params=_cparams("parallel"),
        name="rwkv_prepare",
    )(big, big, big, big, conv_w, w0, w2s, a0, a2s, kk0, kk1, rk)


def _scan_chunks(chains):
    L = SCAN_CHUNK
    lane = lax.broadcasted_iota(jnp.int32, (L, LANES), 1)
    m0 = _bf(jnp.where(lane < HEAD_DIM, 1.0, 0.0))
    m1 = _bf(jnp.where(lane < HEAD_DIM, 0.0, 1.0))

    def stack(x):
        blocks = [x[:, c:c + LANES] for c in range(0, x.shape[1], LANES)]
        top = [b * m0 for b in blocks]
        bot = [b * m1 for b in blocks]
        if len(blocks) == 1:
            return jnp.concatenate([top[0], bot[0]], axis=0)
        return jnp.concatenate([jnp.concatenate(top, axis=1), jnp.concatenate(bot, axis=1)], axis=0)

    trow = lax.broadcasted_iota(jnp.int32, (L, LANES), 0)
    tcol = lax.broadcasted_iota(jnp.int32, (L, LANES), 1) % L
    masks = {True: (tcol < trow, tcol <= trow), False: (tcol > trow, tcol >= trow)}
    eye = lax.broadcasted_iota(jnp.int32, (LANES, LANES), 0) == lax.broadcasted_iota(jnp.int32, (LANES, LANES), 1)
    fwd = [ch[9] for ch in chains]
    nc = range(len(chains))

    v_s = [stack(ch[6]) for ch in chains]
    big1 = [_dot_nt(jnp.concatenate([chains[i][0], chains[i][1]], axis=0),
                    jnp.concatenate([stack(chains[i][2]), stack(chains[i][3])], axis=0)) for i in nc]
    pm = [jnp.where(masks[fwd[i]][0], big1[i][:L, :LANES], 0.0) for i in nc]
    mak = [_bf(jnp.where(masks[fwd[i]][0], big1[i][:L, LANES:], 0.0)) for i in nc]
    lhs_top = [_bf(jnp.where(jnp.concatenate([masks[fwd[i]][1]] * 2, axis=1), big1[i][L:], 0.0)) for i in nc]
    mv = [_dot(mak[i], v_s[i]) for i in nc]
    px = [jnp.concatenate([chains[i][0].astype(F32), mv[i]], axis=1) for i in nc]
    steps = int(np.log2(L))
    for it in range(steps):
        if it < steps - 1:
            res = [_dot(_bf(pm[i]), stack(_bf(jnp.concatenate([pm[i], px[i]], axis=1)))) for i in nc]
            px = [px[i] + res[i][:, LANES:] for i in nc]
            pm = [res[i][:, :LANES] for i in nc]
        else:
            res = [_dot(_bf(pm[i]), stack(_bf(px[i]))) for i in nc]
            px = [px[i] + res[i] for i in nc]
    rhs2 = [jnp.concatenate([stack(_bf(px[i])), jnp.concatenate([jnp.zeros_like(v_s[i]), v_s[i]], axis=1)], axis=0)
            for i in nc]
    lhs_bot = [_bf(jnp.concatenate([stack(chains[i][4]), stack(chains[i][5])], axis=0).astype(F32).T)
               for i in nc]
    res2 = [_dot(jnp.concatenate([lhs_top[i], lhs_bot[i]], axis=0), rhs2[i]) for i in nc]
    lhs3 = [_bf(jnp.concatenate(
        [chains[i][1].astype(F32) + res2[i][:L, :LANES],
         res2[i][L:, :LANES] + jnp.where(eye, jnp.broadcast_to(chains[i][7], (LANES, LANES)), 0.0)], axis=0))
        for i in nc]
    res3 = [_dot(lhs3[i], _bf(chains[i][8])) for i in nc]
    return [(res3[i][:L] + res2[i][:L, LANES:], res3[i][L:] + res2[i][L:, LANES:]) for i in nc]


def _rwkv_scan_kernel(v_f, at_f, rt_f, bt_f, kt_f, bb_f, kb_f, pl_f,
                      v_b, at_b, rt_b, bt_b, kt_b, bb_b, kb_b, pl_b,
                      yf_ref, yb_ref, z_ref, *, npairs, nb):
    @pl.when(pl.program_id(1) == 0)
    def _():
        z_ref[...] = jnp.zeros_like(z_ref)

    dirs = ((v_f, at_f, rt_f, bt_f, kt_f, bb_f, kb_f, pl_f, yf_ref, True),
            (v_b, at_b, rt_b, bt_b, kt_b, bb_b, kb_b, pl_b, yb_ref, False))
    chains, dest = [], []
    for s in range(nb):
        for d, (v, at, rt, bt, kt, bb, kb, plr, y_ref, fwd) in enumerate(dirs):
            for p in range(npairs):
                cs = slice(p * LANES, (p + 1) * LANES)
                chains.append((at[s, :, cs], rt[s, :, cs], bt[s, :, cs], kt[s, :, cs], bb[s, :, cs], kb[s, :, cs],
                               v[s, :, cs], plr[s, 0][:, cs], z_ref[s, d, p], fwd))
                dest.append((y_ref, s, cs, d, p))
    for (y_ref, s, cs, d, p), (y, znew) in zip(dest, _scan_chunks(chains)):
        y_ref[s, :, cs] = y
        z_ref[s, d, p] = znew


def _rwkv_scan(prep, batch, s_tot, c_len):
    (v, _bonus, at_f, rt_f, bt_f, kt_f, bb_f, kb_f, pl_f, at_b, rt_b, bt_b, kt_b, bb_b, kb_b, pl_b) = prep
    n, width = v.shape
    L = SCAN_CHUNK
    nch = s_tot // L
    ncc = c_len // L
    npairs = width // LANES
    nb = 2 if batch % 2 == 0 else 1

    def fchunk(c):
        return c

    def bchunk(c):
        return jnp.where(c < ncc, ncc - 1 - c, nch - 1 - (c - ncc))

    def tok(chunk):
        return pl.BlockSpec((nb, L, width), lambda b, c: (b, chunk(c), 0))

    def pls(chunk):
        return pl.BlockSpec((nb, 1, 1, width), lambda b, c: (b, chunk(c), 0, 0))

    def tok3(a):
        return a.reshape(batch, s_tot, width)

    def pl4(a):
        return a.reshape(batch, nch, 1, width)

    kern = functools.partial(_rwkv_scan_kernel, npairs=npairs, nb=nb)
    fwd_in = [tok3(a) for a in (v, at_f, rt_f, bt_f, kt_f, bb_f, kb_f)] + [pl4(pl_f)]
    bwd_in = [tok3(a) for a in (v, at_b, rt_b, bt_b, kt_b, bb_b, kb_b)] + [pl4(pl_b)]
    yf, yb = pl.pallas_call(
        kern,
        grid=(batch // nb, nch),
        in_specs=[tok(fchunk)] * 7 + [pls(fchunk)] + [tok(bchunk)] * 7 + [pls(bchunk)],
        out_specs=[tok(fchunk), tok(bchunk)],
        out_shape=[jax.ShapeDtypeStruct((batch, s_tot, width), F32)] * 2,
        scratch_shapes=[pltpu.VMEM((nb, 2, npairs, LANES, LANES), F32)],
        compiler_params=_cparams("parallel", "arbitrary"),
        name="rwkv_scan",
    )(*fwd_in, *bwd_in)
    return yf.reshape(n, width), yb.reshape(n, width)


def _rwkv_readout_kernel(yf_ref, yb_ref, bonus_ref, gl_ref, g2_ref, lg_ref, lb_ref, o_ref):
    avg = _group_matrix(1.0 / HEAD_DIM)
    gate = _dot(_bf(jax.nn.sigmoid(gl_ref[...].astype(F32))), g2_ref[...])
    for c in range(o_ref.shape[1] // LANES):
        cs = slice(c * LANES, (c + 1) * LANES)
        y = yf_ref[:, cs] + yb_ref[:, cs]
        mu = _dot_split(y, avg, 2)
        dy = y - mu
        var = _dot_split(dy * dy, avg, 2)
        yn = dy * lax.rsqrt(var + LNX_EPS) * lg_ref[:, cs] + lb_ref[:, cs]
        o_ref[:, cs] = _bf((yn + bonus_ref[:, cs].astype(F32)) * gate[:, cs])


def _rwkv_readout(yf, yb, bonus, big, g2, lnx_g, lnx_b):
    n, width = yf.shape
    tm = _pick_tile(n, (512, 256))
    tok = pl.BlockSpec((tm, width), lambda i: (i, 0))
    full = lambda a: pl.BlockSpec(a.shape, lambda i: (0,) * a.ndim)
    return pl.pallas_call(
        _rwkv_readout_kernel,
        grid=(n // tm,),
        in_specs=[tok, tok, tok, pl.BlockSpec((tm, LANES), lambda i: (i, COL_GLOW // LANES)),
                  full(g2), full(lnx_g), full(lnx_b)],
        out_specs=tok,
        out_shape=jax.ShapeDtypeStruct((n, width), BF16),
        compiler_params=_cparams("parallel"),
        name="rwkv_readout",
    )(yf, yb, bonus, big, g2, lnx_g, lnx_b)


def _merge_kernel(oa_ref, ob_ref, oc_ref, gt_ref, x_ref, wb_ref, wo_ref, ng_ref, mod_ref, o_ref, *, nsub, d):
    y = None
    for br, ref in enumerate((oa_ref, ob_ref, oc_ref)):
        g = jax.nn.sigmoid(gt_ref[:, br * d:(br + 1) * d].astype(F32))
        t = g * _dot(ref[...], wb_ref[br])
        y = t if y is None else y + t
    o = _dot(_bf(y), wo_ref[...])
    o_ref[...] = x_ref[...] + _mod_rows(mod_ref, nsub, 2, d) * _rms(o, ng_ref[...])


def _merge(oa, ob, oc, big, x, wb, wo, ng, modblk):
    n, d = x.shape
    width = oa.shape[1]
    tm = _pick_tile(n, (512, 256))
    nsub = tm // MOD_ROWS
    kern = functools.partial(_merge_kernel, nsub=nsub, d=d)
    br = pl.BlockSpec((tm, width), lambda i: (i, 0))
    return pl.pallas_call(
        kern,
        grid=(n // tm,),
        in_specs=[br, br, br,
                  pl.BlockSpec((tm, 3 * d), lambda i: (i, COL_GATES // (3 * d))),
                  pl.BlockSpec((tm, d), lambda i: (i, 0)),
                  pl.BlockSpec(wb.shape, lambda i: (0, 0, 0)),
                  pl.BlockSpec(wo.shape, lambda i: (0, 0)),
                  pl.BlockSpec((1, d), lambda i: (0, 0)),
                  pl.BlockSpec((nsub, 1, 6 * d), lambda i: (i, 0, 0))],
        out_specs=pl.BlockSpec((tm, d), lambda i: (i, 0)),
        out_shape=jax.ShapeDtypeStruct((n, d), F32),
        compiler_params=_cparams("parallel"),
        name="merge_out_proj",
    )(oa, ob, oc, big, x, wb, wo, ng, modblk)


def _ffn_kernel(x_ref, g_ref, mod_ref, wg_ref, wu_ref, wd_ref, ng_ref, o_ref, h_ref, acc_ref, *, nsub, d):
    j = pl.program_id(1)

    @pl.when(j == 0)
    def _():
        y = _rms(x_ref[...], g_ref[...])
        h_ref[...] = _bf(y * (1.0 + _mod_rows(mod_ref, nsub, 4, d)) + _mod_rows(mod_ref, nsub, 3, d))
        acc_ref[...] = jnp.zeros_like(acc_ref)

    h = h_ref[...]
    g = _dot(h, wg_ref[...])
    u = _dot(h, wu_ref[...])
    acc_ref[...] += _dot(_bf(g * jax.nn.sigmoid(g) * u), wd_ref[...])

    @pl.when(j == pl.num_programs(1) - 1)
    def _():
        o_ref[...] = x_ref[...] + _mod_rows(mod_ref, nsub, 5, d) * _rms(acc_ref[...], ng_ref[...])


def _dense_ffn(x, gain_in, gain_out, modblk, w_gu, w_down):
    n, d = x.shape
    f = w_down.shape[0]
    tm = _pick_tile(n, (1024, 512, 256))
    tf = _pick_tile(f, (1408, 1024, 512, 256, 128))
    nf = f // tf
    nsub = tm // MOD_ROWS
    kern = functools.partial(_ffn_kernel, nsub=nsub, d=d)
    return pl.pallas_call(
        kern,
        grid=(n // tm, nf),
        in_specs=[pl.BlockSpec((tm, d), lambda i, j: (i, 0)),
                  pl.BlockSpec((1, d), lambda i, j: (0, 0)),
                  pl.BlockSpec((nsub, 1, 6 * d), lambda i, j: (i, 0, 0)),
                  pl.BlockSpec((d, tf), lambda i, j: (0, j)),
                  pl.BlockSpec((d, tf), lambda i, j: (0, j + nf)),
                  pl.BlockSpec((tf, d), lambda i, j: (j, 0)),
                  pl.BlockSpec((1, d), lambda i, j: (0, 0))],
        out_specs=pl.BlockSpec((tm, d), lambda i, j: (i, 0)),
        out_shape=jax.ShapeDtypeStruct((n, d), F32),
        scratch_shapes=[pltpu.VMEM((tm, d), BF16), pltpu.VMEM((tm, d), F32)],
        compiler_params=_cparams("parallel", "arbitrary"),
        name="dense_swiglu_ffn",
    )(x, gain_in, modblk, w_gu, w_gu, w_down, gain_out)


def _router_kernel(x_ref, g_ref, mod_ref, wr_ref, h_ref, comb_ref, rank_ref, combt_ref, rankt_ref, cnt_ref,
                   *, nsub, d):
    lane = lax.broadcasted_iota(jnp.int32, (MOD_ROWS, LANES), 1)
    r2 = lax.broadcasted_iota(jnp.int32, (MOD_ROWS, MOD_ROWS), 0)
    c2 = lax.broadcasted_iota(jnp.int32, (MOD_ROWS, MOD_ROWS), 1)
    tri = jnp.where(c2 < r2, 1.0, 0.0).astype(BF16)
    ninf = jnp.float32(-jnp.inf)
    running = jnp.zeros((1, LANES), F32)
    for s in range(nsub):
        rows = slice(s * MOD_ROWS, (s + 1) * MOD_ROWS)
        m = mod_ref[s]
        h = _rms(x_ref[rows, :], g_ref[...]) * (1.0 + m[:, 4 * d:5 * d]) + m[:, 3 * d:4 * d]
        hh = _bf(h)
        h_ref[rows, :] = hh
        hl = _bf(h - hh.astype(F32))
        both = _dot(hh, wr_ref[...])
        logits = both[:, :LANES] + both[:, LANES:] + _dot(hl, wr_ref[:, :LANES])
        logits = jnp.where(lane < N_EXPERTS, logits, ninf)
        m1 = jnp.max(logits, axis=-1, keepdims=True)
        i1 = jnp.min(jnp.where(logits == m1, lane, LANES), axis=-1, keepdims=True)
        rest = jnp.where(lane == i1, ninf, logits)
        m2 = jnp.max(rest, axis=-1, keepdims=True)
        i2 = jnp.min(jnp.where(rest == m2, lane, LANES), axis=-1, keepdims=True)
        e2 = jnp.exp(m2 - m1)
        w1 = 1.0 / (1.0 + e2)
        comb = jnp.where(lane == i1, w1, 0.0) + jnp.where(lane == i2, e2 * w1, 0.0)
        ind = jnp.where(comb > 0.0, 1.0, 0.0)
        rank = _dot(tri, _bf(ind)) + running
        running = running + jnp.sum(ind, axis=0, keepdims=True)
        comb_ref[rows, :] = comb
        rank_ref[rows, :] = rank
        combt_ref[:, rows] = comb.T[:N_EXPERTS, :]
        rankt_ref[:, rows] = rank.T[:N_EXPERTS, :]
    cnt_ref[0] = running


def _router(x, gain_in, modblk, w_router_pad, tm):
    n, d = x.shape
    nsub = tm // MOD_ROWS
    nt = n // tm
    kern = functools.partial(_router_kernel, nsub=nsub, d=d)
    tokm = pl.BlockSpec((tm, LANES), lambda i: (i, 0))
    expm = pl.BlockSpec((N_EXPERTS, tm), lambda i: (0, i))
    return pl.pallas_call(
        kern,
        grid=(nt,),
        in_specs=[pl.BlockSpec((tm, d), lambda i: (i, 0)),
                  pl.BlockSpec((1, d), lambda i: (0, 0)),
                  pl.BlockSpec((nsub, 1, 6 * d), lambda i: (i, 0, 0)),
                  pl.BlockSpec((d, 2 * LANES), lambda i: (0, 0))],
        out_specs=[pl.BlockSpec((tm, d), lambda i: (i, 0)), tokm, tokm, expm, expm,
                   pl.BlockSpec((1, 1, LANES), lambda i: (i, 0, 0))],
        out_shape=[jax.ShapeDtypeStruct((n, d), BF16),
                   jax.ShapeDtypeStruct((n, LANES), F32), jax.ShapeDtypeStruct((n, LANES), F32),
                   jax.ShapeDtypeStruct((N_EXPERTS, n), F32), jax.ShapeDtypeStruct((N_EXPERTS, n), F32),
                   jax.ShapeDtypeStruct((nt, 1, LANES), F32)],
        compiler_params=_cparams("parallel"),
        name="moe_router",
    )(x, gain_in, modblk, w_router_pad)


def _moe_kernel(cnt_ref, h_ref, comb_ref, rank_ref, combt_ref, rankt_ref, wg_ref, wu_ref, wd_ref, o_ref,
                xe_ref, y_ref, acc_ref, *, tm, rb):
    i = pl.program_id(0)
    e = pl.program_id(1)
    j = pl.program_id(2)
    nf = pl.num_programs(2)
    cnt = cnt_ref[i * N_EXPERTS + e]
    half, quarter = rb // 2, rb // 4
    nblk = cnt // rb
    tail0 = pl.multiple_of(nblk * rb, rb)
    rem = cnt - tail0

    def for_blocks(fn):
        lax.fori_loop(0, nblk, lambda b, carry: fn(pl.multiple_of(b * rb, rb), rb) or carry, 0)

        @pl.when(rem > half + quarter)
        def _():
            fn(tail0, rb)

        @pl.when(jnp.logical_and(rem > quarter, rem <= half + quarter))
        def _():
            fn(tail0, half)

        @pl.when(jnp.logical_or(jnp.logical_and(rem > 0, rem <= quarter),
                                jnp.logical_and(rem > half, rem <= half + quarter)))
        def _():
            fn(pl.multiple_of(tail0 + jnp.where(rem > half, half, 0), quarter), quarter)

    @pl.when(jnp.logical_and(e == 0, j == 0))
    def _():
        acc_ref[...] = jnp.zeros_like(acc_ref)

    @pl.when(j == 0)
    def _():
        key = jnp.where(combt_ref[pl.ds(e, 1), :] > 0.0, rankt_ref[pl.ds(e, 1), :], -1.0)

        def gather(r0, nr):
            want = (r0 + lax.broadcasted_iota(jnp.int32, (nr, tm), 0)).astype(F32)
            sel = jnp.where(key == want, 1.0, 0.0).astype(BF16)
            xe_ref[pl.ds(r0, nr), :] = _bf(_dot(sel, h_ref[...]))

        for_blocks(gather)

    def expert(r0, nr):
        rows = pl.ds(r0, nr)
        xb = xe_ref[rows, :]
        g = _dot(xb, wg_ref[0])
        u = _dot(xb, wu_ref[0])
        part = _dot(_bf(g * jax.nn.sigmoid(g) * u), wd_ref[0])

        @pl.when(j == 0)
        def _():
            y_ref[rows, :] = part

        @pl.when(j > 0)
        def _():
            y_ref[rows, :] += part

    for_blocks(expert)

    @pl.when(j == nf - 1)
    def _():
        lane = lax.broadcasted_iota(jnp.int32, (tm, LANES), 1)
        rank_col = jnp.sum(jnp.where(lane == e, rank_ref[...], 0.0), axis=1, keepdims=True)
        w_col = jnp.sum(jnp.where(lane == e, comb_ref[...], 0.0), axis=1, keepdims=True)

        def scatter(r0, nr):
            want = (r0 + lax.broadcasted_iota(jnp.int32, (tm, nr), 1)).astype(F32)
            selw = _bf(jnp.where(rank_col == want, w_col, 0.0))
            acc_ref[...] += _dot(selw, _bf(y_ref[pl.ds(r0, nr), :]))

        for_blocks(scatter)

    @pl.when(jnp.logical_and(e == pl.num_programs(1) - 1, j == nf - 1))
    def _():
        o_ref[...] = _bf(acc_ref[...])


def _moe_ffn(hb, comb, rank, combt, rankt, counts, w_gu, w_down, tm):
    n, d = hb.shape
    ne, f, _ = w_down.shape
    tf = _pick_tile(f, (896, 512, 256, 128))
    nf = f // tf
    rb = MOE_ROW_BLOCK
    kern = functools.partial(_moe_kernel, tm=tm, rb=rb)
    once = pl.Buffered(1)
    tokm = pl.BlockSpec((tm, LANES), lambda i, e, j, c: (i, 0), pipeline_mode=once)
    expm = pl.BlockSpec((N_EXPERTS, tm), lambda i, e, j, c: (0, i))
    grid_spec = pltpu.PrefetchScalarGridSpec(
        num_scalar_prefetch=1,
        grid=(n // tm, ne, nf),
        in_specs=[pl.BlockSpec((tm, d), lambda i, e, j, c: (i, 0), pipeline_mode=once), tokm, tokm, expm, expm,
                  pl.BlockSpec((1, d, tf), lambda i, e, j, c: (e, 0, j)),
                  pl.BlockSpec((1, d, tf), lambda i, e, j, c: (e, 0, j + nf)),
                  pl.BlockSpec((1, tf, d), lambda i, e, j, c: (e, j, 0))],
        out_specs=pl.BlockSpec((tm, d), lambda i, e, j, c: (i, 0), pipeline_mode=once),
        scratch_shapes=[pltpu.VMEM((tm, d), BF16), pltpu.VMEM((tm, d), F32), pltpu.VMEM((tm, d), F32)])
    return pl.pallas_call(
        kern,
        grid_spec=grid_spec,
        out_shape=jax.ShapeDtypeStruct((n, d), BF16),
        compiler_params=_cparams("parallel", "arbitrary", "arbitrary"),
        name="moe_swiglu_ffn",
    )(counts, hb, comb, rank, combt, rankt, w_gu, w_gu, w_down)


def _residual_kernel(x_ref, y_ref, ng_ref, mod_ref, o_ref, *, nsub, d):
    o_ref[...] = x_ref[...] + _mod_rows(mod_ref, nsub, 5, d) * _rms(y_ref[...].astype(F32), ng_ref[...])


def _gated_residual(x, y, gain_out, modblk):
    n, d = x.shape
    tm = _pick_tile(n, (512, 256))
    nsub = tm // MOD_ROWS
    tok = pl.BlockSpec((tm, d), lambda i: (i, 0))
    return pl.pallas_call(
        functools.partial(_residual_kernel, nsub=nsub, d=d),
        grid=(n // tm,),
        in_specs=[tok, tok, pl.BlockSpec((1, d), lambda i: (0, 0)),
                  pl.BlockSpec((nsub, 1, 6 * d), lambda i: (i, 0, 0))],
        out_specs=tok,
        out_shape=jax.ShapeDtypeStruct((n, d), F32),
        compiler_params=_cparams("parallel"),
        name="moe_gated_residual",
    )(x, y, gain_out, modblk)


def _rope_tables(t_len, c_len):
    pairs = HEAD_DIM // 4
    rows = t_len // GRID_W
    row = jnp.repeat(jnp.arange(rows, dtype=F32), GRID_W)
    col = jnp.tile(jnp.arange(GRID_W, dtype=F32), rows)
    freqs = ROPE_BASE ** (-jnp.arange(pairs, dtype=F32) / pairs)
    ar = row[:, None] * freqs
    ac = col[:, None] * freqs
    cos = jnp.concatenate([jnp.cos(ar), jnp.cos(ar), jnp.cos(ac), jnp.cos(ac)], axis=1)
    sin = jnp.concatenate([-jnp.sin(ar), jnp.sin(ar), -jnp.sin(ac), jnp.sin(ac)], axis=1)
    cos = jnp.concatenate([jnp.ones((c_len, HEAD_DIM), F32), cos], axis=0)
    sin = jnp.concatenate([jnp.zeros((c_len, HEAD_DIM), F32), sin], axis=0)
    return jnp.tile(cos, (1, 2)), jnp.tile(sin, (1, 2))


def _block_diag2(w):
    z = jnp.zeros_like(w[0])
    return jnp.concatenate([jnp.concatenate([w[0], z], axis=1), jnp.concatenate([z, w[1]], axis=1)], axis=0)


def kernel(x, c, ctx, c_ctx, w_mod, b_mod, norm_gain, w_in, qk_gain, rwkv_conv, decay_w0, decay_w2, iclr_a0, iclr_a2, key_k, bonus_rk, gate_g2, lnx_gain, lnx_bias, cmlp_ln_gain, cmlp_ln_bias, cmlp_ws, cmlp_bs, w_branch, w_out, ffn_w_gu, ffn_w_down, moe_router, moe_w_gu, moe_w_down):
    batch, t_len, d = x.shape
    c_len = ctx.shape[1]
    depth = w_mod.shape[0]
    s_tot = c_len + t_len
    n = batch * s_tot
    assert c_len % MOD_ROWS == 0 and t_len % MOD_ROWS == 0 and d % LANES == 0
    width = bonus_rk.shape[1] * bonus_rk.shape[2]
    nheads = width // HEAD_DIM

    xs = jnp.concatenate([ctx, x], axis=1).reshape(n, d)

    mod_rows = 8 * ((batch + 1 + 7) // 8)
    cvec = jnp.zeros((mod_rows, d), F32).at[0].set(c_ctx).at[1:batch + 1].set(c)
    mods = _modulation(cvec, w_mod, b_mod)
    mod_ctx = jnp.broadcast_to(mods[:, 0:1, None, :], (depth, batch, c_len // MOD_ROWS, 6 * d))
    mod_lat = jnp.broadcast_to(mods[:, 1:batch + 1, None, :], (depth, batch, t_len // MOD_ROWS, 6 * d))
    modblk_all = jnp.concatenate([mod_ctx, mod_lat], axis=2).reshape(depth, n // MOD_ROWS, 1, 6 * d)

    order = np.array(Q_HEAD_ORDER)
    nl = depth
    wq = w_in[:, :, 2048:2560].reshape(nl, d, nheads, HEAD_DIM)[:, :, order].reshape(nl, d, width)
    w_in_p = jnp.concatenate([
        w_in[:, :, 256:1792], wq, w_in[:, :, 2688:3712], w_in[:, :, 3712:6784],
        w_in[:, :, 0:128], w_in[:, :, 128:256], w_in[:, :, 1792:1920], w_in[:, :, 1920:2048],
        w_in[:, :, 2560:2688], jnp.zeros((nl, d, IN_PAD - 6784), F32)], axis=2).astype(BF16)
    wb = w_branch.astype(BF16)
    wb0 = wb[:, 0].reshape(nl, nheads, HEAD_DIM, d)[:, order].reshape(nl, width, d)
    wb = jnp.concatenate([wb0[:, None], wb[:, 1:]], axis=1)
    wo = w_out.astype(BF16)
    cos, sin = _rope_tables(t_len, c_len)
    qg = jnp.tile(qk_gain[:, 0], (1, 2))[:, None, :]
    kg = jnp.tile(qk_gain[:, 1], (1, 2))[:, None, :]
    ws_b = cmlp_ws.astype(BF16)
    bs_b = jnp.broadcast_to(cmlp_bs[..., None], cmlp_bs.shape + (CMLP_CHUNK,))
    w2s = jnp.stack([_block_diag2(decay_w2[l]) for l in range(nl)]).astype(BF16)
    a2s = jnp.stack([_block_diag2(iclr_a2[l]) for l in range(nl)]).astype(BF16)
    w0 = decay_w0.reshape(nl, 1, 2 * width)
    a0 = iclr_a0.reshape(nl, 1, 2 * width)
    rk = bonus_rk.reshape(nl, 1, width)
    g2 = gate_g2.astype(BF16)
    ffn_gu = ffn_w_gu.astype(BF16)
    ffn_dn = ffn_w_down.astype(BF16)
    moe_gu = moe_w_gu.astype(BF16)
    moe_dn = moe_w_down.astype(BF16)
    router_pad = jnp.pad(moe_router, ((0, 0), (0, 0), (0, LANES - moe_router.shape[2])))
    router_hi = router_pad.astype(BF16)
    router_pad = jnp.concatenate([router_hi, (router_pad - router_hi.astype(F32)).astype(BF16)], axis=2)

    for l in range(depth):
        modblk = modblk_all[l]
        ng = norm_gain[l]
        big = _norm_mod_matmul(xs, ng[0:1], modblk, w_in_p[l], 0, 1)
        qh, kbd, vt = _qk_prep(big, qg[l], kg[l], cos, sin, s_tot)
        oa = _attention(qh, kbd, vt, batch, s_tot, c_len)
        prep = _rwkv_prep(big, rwkv_conv[l], w0[l], w2s[l], a0[l], a2s[l],
                          key_k[l, 0:1], key_k[l, 1:2], rk[l], s_tot, c_len)
        yf, yb = _rwkv_scan(prep, batch, s_tot, c_len)
        ob = _rwkv_readout(yf, yb, prep[1], big, g2[l], lnx_gain[l][None], lnx_bias[l][None])
        oc = _chunk_mlp(big, cmlp_ln_gain[l][None], cmlp_ln_bias[l][None], ws_b[l], bs_b[l])
        xs = _merge(oa, ob, oc, big, xs, wb[l], wo[l], ng[1:2], modblk)
        if l % 2 == 0:
            xs = _dense_ffn(xs, ng[2:3], ng[3:4], modblk, ffn_gu[l // 2], ffn_dn[l // 2])
        else:
            tmoe = _pick_tile(n, (2048, 1024, 512, 256))
            hb, comb, rank, combt, rankt, cnt = _router(xs, ng[2:3], modblk, router_pad[l // 2], tmoe)
            counts = cnt[:, 0, :N_EXPERTS].astype(jnp.int32).reshape(-1)
            y = _moe_ffn(hb, comb, rank, combt, rankt, counts, moe_gu[l // 2], moe_dn[l // 2], tmoe)
            xs = _gated_residual(xs, y, ng[3:4], modblk)
    return xs.reshape(batch, s_tot, d)[:, c_len:, :]
```

```python
import functools

import jax
import jax.numpy as jnp
import numpy as np
from jax import lax
from jax.experimental import pallas as pl
from jax.experimental.pallas import tpu as pltpu

F32 = jnp.float32
BF16 = jnp.bfloat16
HIGHEST = lax.Precision.HIGHEST

EPS = 1e-6
LNX_EPS = 64e-5
HEAD_DIM = 64
ROPE_BASE = 10000.0
GRID_W = 64
LANES = 128
MOD_ROWS = 256
SCAN_CHUNK = 64
CMLP_CHUNK = 128
ATT_TK = 256
ATT_VROWS = HEAD_DIM + 16
ATT_GROUP = 1
N_EXPERTS = 8
MOE_ROW_BLOCK = 512
VMEM_LIMIT = 56 * 1024 * 1024

COL_RKV, COL_Q, COL_UV, COL_GATES = 0, 1536, 2048, 3072
COL_K, COL_V, COL_WLOW, COL_ALOW, COL_GLOW = 6144, 6272, 6400, 6528, 6656
IN_PAD = 7168
Q_HEAD_ORDER = (0, 4, 1, 5, 2, 6, 3, 7)


def _cparams(*sem):
    return pltpu.CompilerParams(dimension_semantics=sem, vmem_limit_bytes=VMEM_LIMIT)


def _dot(a, b):
    return jnp.dot(a, b, preferred_element_type=F32)


def _dot_nt(a, b):
    return lax.dot_general(a, b, (((1,), (1,)), ((), ())), preferred_element_type=F32)


def _bf(x):
    return x.astype(BF16)


def _dot_split(a, b_exact, terms):
    acc = None
    rem = a
    for _ in range(terms):
        piece = _bf(rem)
        rem = rem - piece.astype(F32)
        part = _dot(piece, b_exact)
        acc = part if acc is None else acc + part
    return acc


def _dot_split_left(a_exact, b, terms):
    acc = None
    rem = b
    for _ in range(terms):
        piece = _bf(rem)
        rem = rem - piece.astype(F32)
        part = _dot(a_exact, piece)
        acc = part if acc is None else acc + part
    return acc


def _group_matrix(scale):
    r = lax.broadcasted_iota(jnp.int32, (LANES, LANES), 0) // HEAD_DIM
    c = lax.broadcasted_iota(jnp.int32, (LANES, LANES), 1) // HEAD_DIM
    return jnp.where(r == c, scale, 0.0).astype(BF16)


def _pick_tile(n, candidates):
    for t in candidates:
        if n % t == 0:
            return t
    raise ValueError(f"no tile in {candidates} divides {n}")


def _mod_rows(mod_ref, nsub, idx, d):
    parts = [jnp.broadcast_to(mod_ref[s][:, idx * d:(idx + 1) * d], (MOD_ROWS, d)) for s in range(nsub)]
    return parts[0] if nsub == 1 else jnp.concatenate(parts, axis=0)


def _rms(x, g):
    return x * lax.rsqrt(jnp.mean(x * x, axis=-1, keepdims=True) + EPS) * g


def _mod_kernel(c_ref, w_ref, b_ref, o_ref):
    cv = c_ref[...]
    s = cv * jax.nn.sigmoid(cv)
    o_ref[0] = jnp.dot(s, w_ref[0], precision=HIGHEST, preferred_element_type=F32) + b_ref[0]


def _modulation(cvec, w_mod, b_mod):
    nl, d, d6 = w_mod.shape
    rows = cvec.shape[0]
    tn = 1024
    return pl.pallas_call(
        _mod_kernel,
        grid=(nl, d6 // tn),
        in_specs=[pl.BlockSpec((rows, d), lambda l, j: (0, 0)),
                  pl.BlockSpec((1, d, tn), lambda l, j: (l, 0, j)),
                  pl.BlockSpec((1, 1, tn), lambda l, j: (l, 0, j))],
        out_specs=pl.BlockSpec((1, rows, tn), lambda l, j: (l, 0, j)),
        out_shape=jax.ShapeDtypeStruct((nl, rows, d6), F32),
        compiler_params=_cparams("parallel", "parallel"),
        name="modulation",
    )(cvec, w_mod, b_mod.reshape(nl, 1, d6))


def _nmm_kernel(x_ref, g_ref, mod_ref, w_ref, o_ref, h_ref, *, nsub, d, shift_idx, scale_idx):
    @pl.when(pl.program_id(1) == 0)
    def _():
        y = _rms(x_ref[...], g_ref[...])
        sc = _mod_rows(mod_ref, nsub, scale_idx, d)
        sh = _mod_rows(mod_ref, nsub, shift_idx, d)
        h_ref[...] = _bf(y * (1.0 + sc) + sh)

    o_ref[...] = _bf(_dot(h_ref[...], w_ref[...]))


def _norm_mod_matmul(x, gain, modblk, w, shift_idx, scale_idx):
    n, d = x.shape
    nout = w.shape[1]
    tm = _pick_tile(n, (1024, 512, 256))
    tn = 1024
    nsub = tm // MOD_ROWS
    kern = functools.partial(_nmm_kernel, nsub=nsub, d=d, shift_idx=shift_idx, scale_idx=scale_idx)
    return pl.pallas_call(
        kern,
        grid=(n // tm, nout // tn),
        in_specs=[pl.BlockSpec((tm, d), lambda i, j: (i, 0)),
                  pl.BlockSpec((1, d), lambda i, j: (0, 0)),
                  pl.BlockSpec((nsub, 1, 6 * d), lambda i, j: (i, 0, 0)),
                  pl.BlockSpec((d, tn), lambda i, j: (0, j))],
        out_specs=pl.BlockSpec((tm, tn), lambda i, j: (i, j)),
        out_shape=jax.ShapeDtypeStruct((n, nout), BF16),
        scratch_shapes=[pltpu.VMEM((tm, d), BF16)],
        compiler_params=_cparams("parallel", "arbitrary"),
        name="norm_mod_in_proj",
    )(x, gain, modblk, w)


def _qkprep_kernel(q_ref, k_ref, v_ref, qg_ref, kg_ref, cos_ref, sin_ref, qo_ref, ko_ref, vo_ref):
    cos = cos_ref[...]
    sin = sin_ref[...]
    avg = _group_matrix(1.0 / HEAD_DIM)
    lane = lax.broadcasted_iota(jnp.int32, cos.shape, 1)
    first = (lane % 32) < 16
    left = lane < HEAD_DIM

    def norm_rope(x, g):
        ms = _dot_split(x * x, avg, 2)
        xn = x * lax.rsqrt(ms + EPS) * g
        partner = jnp.where(first, pltpu.roll(xn, LANES - 16, 1), pltpu.roll(xn, 16, 1))
        return xn * cos + partner * sin

    qscale = (HEAD_DIM ** -0.5) * float(np.log2(np.e))
    for j in range(q_ref.shape[1] // LANES):
        q = q_ref[:, j * LANES:(j + 1) * LANES].astype(F32)
        qo_ref[:, j * LANES:(j + 1) * LANES] = _bf(norm_rope(q, qg_ref[...]) * qscale)
    k = norm_rope(k_ref[...].astype(F32), kg_ref[...])
    zero = jnp.zeros_like(k)
    k0 = _bf(jnp.where(left, k, zero))
    k1 = _bf(jnp.where(left, zero, k))
    v = v_ref[...].astype(F32)
    ones = jnp.ones((ATT_VROWS - HEAD_DIM, ATT_TK), BF16)
    for c in range(vo_ref.shape[0]):
        rows = slice(c * ATT_TK, (c + 1) * ATT_TK)
        ko_ref[c, :ATT_TK, :] = k0[rows]
        ko_ref[c, ATT_TK:, :] = k1[rows]
        vt = _bf(v[rows, :].T)
        for t in range(2):
            vo_ref[c, t, :HEAD_DIM, :] = vt[t * HEAD_DIM:(t + 1) * HEAD_DIM]
            vo_ref[c, t, HEAD_DIM:, :] = ones


def _qk_prep(big, qg, kg, cos, sin, s_tot):
    n = big.shape[0]
    tm = MOD_ROWS
    npos = s_tot // tm
    qw = 512
    vchunks = tm // ATT_TK
    return pl.pallas_call(
        _qkprep_kernel,
        grid=(n // tm,),
        in_specs=[pl.BlockSpec((tm, qw), lambda i: (i, COL_Q // qw)),
                  pl.BlockSpec((tm, LANES), lambda i: (i, COL_K // LANES)),
                  pl.BlockSpec((tm, LANES), lambda i: (i, COL_V // LANES)),
                  pl.BlockSpec((1, LANES), lambda i: (0, 0)),
                  pl.BlockSpec((1, LANES), lambda i: (0, 0)),
                  pl.BlockSpec((tm, LANES), lambda i: (i % npos, 0)),
                  pl.BlockSpec((tm, LANES), lambda i: (i % npos, 0))],
        out_specs=[pl.BlockSpec((tm, qw), lambda i: (i, 0)),
                   pl.BlockSpec((vchunks, 2 * ATT_TK, LANES), lambda i: (i, 0, 0)),
                   pl.BlockSpec((vchunks, 2, ATT_VROWS, ATT_TK), lambda i: (i, 0, 0, 0))],
        out_shape=[jax.ShapeDtypeStruct((n, qw), BF16),
                   jax.ShapeDtypeStruct((n // ATT_TK, 2 * ATT_TK, LANES), BF16),
                   jax.ShapeDtypeStruct((n // ATT_TK, 2, ATT_VROWS, ATT_TK), BF16)],
        compiler_params=_cparams("parallel"),
        name="qk_norm_rope",
    )(big, big, big, qg, kg, cos, sin)


def _attn_kernel(q_ref, k_ref, vt_ref, o_ref, acc_ref, sa_ref, sb_ref, *, tq, tk, n_ctx_q, n_ctx_kv, n_kv):
    i = pl.program_id(1)
    nkv = jnp.where(i < n_ctx_q, n_ctx_kv, n_kv)
    hd = HEAD_DIM
    npair = q_ref.shape[1] // LANES
    nh = 2 * npair
    qs = [q_ref[:, j * LANES:(j + 1) * LANES] for j in range(npair)]
    vr = ATT_VROWS
    acc_ref[...] = jnp.zeros_like(acc_ref)

    def scores_to(dst_ref, kb, h):
        s = _dot_nt(kb[(h % 2) * tk:(h % 2 + 1) * tk], qs[h // 2])
        dst_ref[h, :tk, :] = s
        dst_ref[h, tk:, :] = jnp.broadcast_to(jnp.max(s, axis=0, keepdims=True), (8, tq))

    def consume(src_ref, c, h, mh):
        n = jnp.maximum(mh, src_ref[h, tk:tk + 1, :])
        p = _bf(jnp.exp2(src_ref[h, :tk, :] - n))
        rows = slice(h * vr, (h + 1) * vr)
        acc_ref[rows, :] = acc_ref[rows, :] * jnp.exp2(mh - n) + _dot(vt_ref[c, h % 2], p)
        return n

    def step(src_ref, dst_ref, c, m, prefetch):
        kb = k_ref[c + 1] if prefetch else None
        new_m = []
        for h in range(nh):
            if prefetch:
                scores_to(dst_ref, kb, h)
            new_m.append(consume(src_ref, c, h, m[h]))
        return tuple(new_m)

    def body(u, m):
        c = 2 * u
        m = step(sa_ref, sb_ref, c, m, True)
        return step(sb_ref, sa_ref, c + 1, m, True)

    kb0 = k_ref[0]
    for h in range(nh):
        scores_to(sa_ref, kb0, h)
    m = lax.fori_loop(0, (nkv - 1) // 2, body, (jnp.full((1, tq), -1e30, F32),) * nh)
    step(sa_ref, sb_ref, nkv - 1, m, False)
    for j in range(npair):
        o = [acc_ref[h * vr:h * vr + hd, :] * (1.0 / acc_ref[h * vr + hd:h * vr + hd + 1, :]) for h in (2 * j, 2 * j + 1)]
        o_ref[:, j * LANES:(j + 1) * LANES] = _bf(jnp.concatenate(o, axis=0).T)


def _attention(qh, kbd, vt, batch, s_tot, c_len):
    n, qw = qh.shape
    tq = 256
    tk = ATT_TK
    nq = s_tot // tq
    assert (c_len // tk) % 2 == 1 and (s_tot // tk) % 2 == 1, "the key-chunk loop is unrolled by two plus a tail"
    kern = functools.partial(_attn_kernel, tq=tq, tk=tk, n_ctx_q=c_len // tq,
                             n_ctx_kv=c_len // tk, n_kv=s_tot // tk)
    return pl.pallas_call(
        kern,
        grid=(batch, nq),
        in_specs=[pl.BlockSpec((tq, qw), lambda b, i: (b * nq + i, 0)),
                  pl.BlockSpec((s_tot // tk, 2 * tk, LANES), lambda b, i: (b, 0, 0)),
                  pl.BlockSpec((s_tot // tk, 2, ATT_VROWS, tk), lambda b, i: (b, 0, 0, 0))],
        out_specs=pl.BlockSpec((tq, qw), lambda b, i: (b * nq + i, 0)),
        out_shape=jax.ShapeDtypeStruct((n, qw), BF16),
        scratch_shapes=[pltpu.VMEM((2 * (qw // LANES) * ATT_VROWS, tq), F32),
                        pltpu.VMEM((2 * (qw // LANES), tk + 8, tq), F32),
                        pltpu.VMEM((2 * (qw // LANES), tk + 8, tq), F32)],
        compiler_params=_cparams("parallel", "parallel"),
        name="gqa_attention",
    )(qh, kbd, vt)


def _cmlp_kernel(uv_ref, lng_ref, lnb_ref, ws_ref, bs_ref, o_ref, *, nchunk, width):
    x = uv_ref[...].astype(F32)
    g = 0.5 * x * (1.0 + jnp.tanh(0.7978845608028654 * (x + 0.044715 * (x * x * x))))
    u = g[:, :width]
    v = g[:, width:]
    mu = jnp.mean(v, axis=-1, keepdims=True)
    dv = v - mu
    var = jnp.mean(dv * dv, axis=-1, keepdims=True)
    vn = _bf(dv * lax.rsqrt(var + EPS) * lng_ref[...] + lnb_ref[...])
    ngroups = width // CMLP_CHUNK
    for c in range(nchunk):
        r0 = c * CMLP_CHUNK
        for gi in range(ngroups):
            c0 = gi * CMLP_CHUNK
            s = _dot(ws_ref[gi], vn[r0:r0 + CMLP_CHUNK, c0:c0 + CMLP_CHUNK]) + bs_ref[gi]
            o_ref[r0:r0 + CMLP_CHUNK, c0:c0 + CMLP_CHUNK] = _bf(u[r0:r0 + CMLP_CHUNK, c0:c0 + CMLP_CHUNK] * s)


def _chunk_mlp(big, ln_g, ln_b, ws, bs_b):
    n = big.shape[0]
    width = ln_g.shape[1]
    tr = _pick_tile(n, (512, 256, 128))
    kern = functools.partial(_cmlp_kernel, nchunk=tr // CMLP_CHUNK, width=width)
    ng = ws.shape[0]
    return pl.pallas_call(
        kern,
        grid=(n // tr,),
        in_specs=[pl.BlockSpec((tr, 2 * width), lambda i: (i, COL_UV // (2 * width))),
                  pl.BlockSpec((1, width), lambda i: (0, 0)),
                  pl.BlockSpec((1, width), lambda i: (0, 0)),
                  pl.BlockSpec((ng, CMLP_CHUNK, CMLP_CHUNK), lambda i: (0, 0, 0)),
                  pl.BlockSpec((ng, CMLP_CHUNK, CMLP_CHUNK), lambda i: (0, 0, 0))],
        out_specs=pl.BlockSpec((tr, width), lambda i: (i, 0)),
        out_shape=jax.ShapeDtypeStruct((n, width), BF16),
        compiler_params=_cparams("parallel"),
        name="chunk_gmlp",
    )(big, ln_g, ln_b, ws, bs_b)


def _rwkv_prep_kernel(x_ref, xp_ref, xn_ref, lo_ref, conv_ref, w0_ref, w2_ref, a0_ref, a2_ref,
                      kk0_ref, kk1_ref, rk_ref,
                      v_o, bonus_o, at_f, rt_f, bt_f, kt_f, bb_f, kb_f, pl_f,
                      at_b, rt_b, bt_b, kt_b, bb_b, kb_b, pl_b, *, tm, width, blocks_per_seq, ctx_blocks):
    i = pl.program_id(0)
    j = i % blocks_per_seq
    is_first = jnp.logical_or(j == 0, j == ctx_blocks)
    is_last = jnp.logical_or(j == ctx_blocks - 1, j == blocks_per_seq - 1)
    row = lax.broadcasted_iota(jnp.int32, (tm, width), 0)
    gsum = _group_matrix(1.0)
    halo = xp_ref.shape[0]

    def conv(c):
        cs = slice(c * width, (c + 1) * width)
        x = x_ref[:, cs].astype(F32)
        prev_row = jnp.where(is_first, 0.0, xp_ref[halo - 1:halo, cs].astype(F32))
        next_row = jnp.where(is_last, 0.0, xn_ref[0:1, cs].astype(F32))
        xprev = jnp.where(row == 0, prev_row, pltpu.roll(x, 1, 0))
        xnext = jnp.where(row == tm - 1, next_row, pltpu.roll(x, tm - 1, 0))
        return xprev * conv_ref[0:1, cs] + x * conv_ref[1:2, cs] + xnext * conv_ref[2:3, cs]

    r = conv(0)
    k = conv(1)
    v = conv(2)
    v_o[...] = _bf(v)

    def group_sum(x):
        parts = [_dot_split(x[:, c * LANES:(c + 1) * LANES], gsum, 2) for c in range(width // LANES)]
        return jnp.concatenate(parts, axis=1)

    kk = k * kk0_ref[...]
    kk = kk * lax.rsqrt(group_sum(kk * kk) + 1e-12)
    bonus_o[...] = _bf(group_sum(r * k * rk_ref[...]) * v)

    lo = lo_ref[...].astype(F32)
    wd = w0_ref[...] + _dot(_bf(jnp.tanh(lo[:, :LANES])), w2_ref[...])
    ad = jax.nn.sigmoid(a0_ref[...] + _dot(_bf(lo[:, LANES:]), a2_ref[...]))
    lw = -float(np.exp(-0.5) * np.log2(np.e)) * jax.nn.sigmoid(wd)

    r2 = lax.broadcasted_iota(jnp.int32, (tm, tm), 0)
    c2 = lax.broadcasted_iota(jnp.int32, (tm, tm), 1)
    same = (r2 // SCAN_CHUNK) == (c2 // SCAN_CHUNK)
    tri_pre = jnp.where(jnp.logical_and(same, c2 <= r2), 1.0, 0.0).astype(BF16)
    tri_suf = jnp.where(jnp.logical_and(same, c2 >= r2), 1.0, 0.0).astype(BF16)
    nchunk = tm // SCAN_CHUNK

    outs = ((at_f, rt_f, bt_f, kt_f, bb_f, kb_f, pl_f), (at_b, rt_b, bt_b, kt_b, bb_b, kb_b, pl_b))
    for d in range(2):
        ds_ = slice(d * width, (d + 1) * width)
        lwd = lw[:, ds_]
        pre = _dot_split_left(tri_pre, lwd, 3)
        suf = _dot_split_left(tri_suf, lwd, 3)
        cin, rem = (pre, suf - lwd) if d == 0 else (suf, pre - lwd)
        cex = cin - lwd
        a_d = ad[:, ds_]
        b = kk * a_d
        kd = k * (1.0 + (a_d - 1.0) * kk1_ref[...])
        at_o, rt_o, bt_o, kt_o, bb_o, kb_o, pl_o = outs[d]
        at_o[...] = _bf(-kk * jnp.exp2(cex))
        rt_o[...] = _bf(r * jnp.exp2(cin))
        pinv = jnp.exp2(-cin)
        bt_o[...] = _bf(b * pinv)
        kt_o[...] = _bf(kd * pinv)
        pend = jnp.exp2(rem)
        bb_o[...] = _bf(b * pend)
        kb_o[...] = _bf(kd * pend)
        for c in range(nchunk):
            last = (c + 1) * SCAN_CHUNK - 1
            pl_o[c] = jnp.exp2(pre[last:last + 1, :])


def _rwkv_prep(big, conv_w, w0, w2s, a0, a2s, kk0, kk1, rk, s_tot, c_len):
    n = big.shape[0]
    width = rk.shape[1]
    tm = MOD_ROWS
    halo = 16
    hb = tm // halo
    nhalo = n // halo
    nchunk = tm // SCAN_CHUNK
    kern = functools.partial(_rwkv_prep_kernel, tm=tm, width=width, blocks_per_seq=s_tot // tm,
                             ctx_blocks=c_len // tm)
    tok = pl.BlockSpec((tm, width), lambda i: (i, 0))
    plspec = pl.BlockSpec((nchunk, 1, width), lambda i: (i, 0, 0))
    tok_shape = jax.ShapeDtypeStruct((n, width), BF16)
    pl_shape = jax.ShapeDtypeStruct((n // SCAN_CHUNK, 1, width), F32)
    full = lambda a: pl.BlockSpec(a.shape, lambda i: (0,) * a.ndim)
    return pl.pallas_call(
        kern,
        grid=(n // tm,),
        in_specs=[pl.BlockSpec((tm, 3 * width), lambda i: (i, 0)),
                  pl.BlockSpec((halo, 3 * width), lambda i: (jnp.maximum(i * hb - 1, 0), 0)),
                  pl.BlockSpec((halo, 3 * width), lambda i: (jnp.minimum((i + 1) * hb, nhalo - 1), 0)),
                  pl.BlockSpec((tm, 2 * LANES), lambda i: (i, COL_WLOW // (2 * LANES))),
                  full(conv_w), full(w0), full(w2s), full(a0), full(a2s), full(kk0), full(kk1), full(rk)],
        out_specs=[tok, tok] + [tok] * 6 + [plspec] + [tok] * 6 + [plspec],
        out_shape=[tok_shape, tok_shape] + [tok_shape] * 6 + [pl_shape] + [tok_shape] * 6 + [pl_shape],
        compiler_params=_cparams("parallel"),
        name="rwkv_prepare",
    )(big, big, big, big, conv_w, w0, w2s, a0, a2s, kk0, kk1, rk)


def _scan_chunks(chains):
    L = SCAN_CHUNK
    lane = lax.broadcasted_iota(jnp.int32, (L, LANES), 1)
    m0 = _bf(jnp.where(lane < HEAD_DIM, 1.0, 0.0))
    m1 = _bf(jnp.where(lane < HEAD_DIM, 0.0, 1.0))

    def stack(x):
        blocks = [x[:, c:c + LANES] for c in range(0, x.shape[1], LANES)]
        top = [b * m0 for b in blocks]
        bot = [b * m1 for b in blocks]
        if len(blocks) == 1:
            return jnp.concatenate([top[0], bot[0]], axis=0)
        return jnp.concatenate([jnp.concatenate(top, axis=1), jnp.concatenate(bot, axis=1)], axis=0)

    trow = lax.broadcasted_iota(jnp.int32, (L, LANES), 0)
    tcol = lax.broadcasted_iota(jnp.int32, (L, LANES), 1) % L
    masks = {True: (tcol < trow, tcol <= trow), False: (tcol > trow, tcol >= trow)}
    eye = lax.broadcasted_iota(jnp.int32, (LANES, LANES), 0) == lax.broadcasted_iota(jnp.int32, (LANES, LANES), 1)
    fwd = [ch[9] for ch in chains]
    nc = range(len(chains))

    v_s = [stack(ch[6]) for ch in chains]
    big1 = [_dot_nt(jnp.concatenate([chains[i][0], chains[i][1]], axis=0),
                    jnp.concatenate([stack(chains[i][2]), stack(chains[i][3])], axis=0)) for i in nc]
    pm = [jnp.where(masks[fwd[i]][0], big1[i][:L, :LANES], 0.0) for i in nc]
    mak = [_bf(jnp.where(masks[fwd[i]][0], big1[i][:L, LANES:], 0.0)) for i in nc]
    lhs_top = [_bf(jnp.where(jnp.concatenate([masks[fwd[i]][1]] * 2, axis=1), big1[i][L:], 0.0)) for i in nc]
    mv = [_dot(mak[i], v_s[i]) for i in nc]
    px = [jnp.concatenate([chains[i][0].astype(F32), mv[i]], axis=1) for i in nc]
    steps = int(np.log2(L))
    for it in range(steps):
        if it < steps - 1:
            res = [_dot(_bf(pm[i]), stack(_bf(jnp.concatenate([pm[i], px[i]], axis=1)))) for i in nc]
            px = [px[i] + res[i][:, LANES:] for i in nc]
            pm = [res[i][:, :LANES] for i in nc]
        else:
            res = [_dot(_bf(pm[i]), stack(_bf(px[i]))) for i in nc]
            px = [px[i] + res[i] for i in nc]
    rhs2 = [jnp.concatenate([stack(_bf(px[i])), jnp.concatenate([jnp.zeros_like(v_s[i]), v_s[i]], axis=1)], axis=0)
            for i in nc]
    lhs_bot = [_bf(jnp.concatenate([stack(chains[i][4]), stack(chains[i][5])], axis=0).astype(F32).T)
               for i in nc]
    res2 = [_dot(jnp.concatenate([lhs_top[i], lhs_bot[i]], axis=0), rhs2[i]) for i in nc]
    lhs3 = [_bf(jnp.concatenate(
        [chains[i][1].astype(F32) + res2[i][:L, :LANES],
         res2[i][L:, :LANES] + jnp.where(eye, jnp.broadcast_to(chains[i][7], (LANES, LANES)), 0.0)], axis=0))
        for i in nc]
    res3 = [_dot(lhs3[i], _bf(chains[i][8])) for i in nc]
    return [(res3[i][:L] + res2[i][:L, LANES:], res3[i][L:] + res2[i][L:, LANES:]) for i in nc]


def _rwkv_scan_kernel(v_f, at_f, rt_f, bt_f, kt_f, bb_f, kb_f, pl_f,
                      v_b, at_b, rt_b, bt_b, kt_b, bb_b, kb_b, pl_b,
                      yf_ref, yb_ref, z_ref, *, npairs, nb):
    @pl.when(pl.program_id(1) == 0)
    def _():
        z_ref[...] = jnp.zeros_like(z_ref)

    dirs = ((v_f, at_f, rt_f, bt_f, kt_f, bb_f, kb_f, pl_f, yf_ref, True),
            (v_b, at_b, rt_b, bt_b, kt_b, bb_b, kb_b, pl_b, yb_ref, False))
    chains, dest = [], []
    for s in range(nb):
        for d, (v, at, rt, bt, kt, bb, kb, plr, y_ref, fwd) in enumerate(dirs):
            for p in range(npairs):
                cs = slice(p * LANES, (p + 1) * LANES)
                chains.append((at[s, :, cs], rt[s, :, cs], bt[s, :, cs], kt[s, :, cs], bb[s, :, cs], kb[s, :, cs],
                               v[s, :, cs], plr[s, 0][:, cs], z_ref[s, d, p], fwd))
                dest.append((y_ref, s, cs, d, p))
    for (y_ref, s, cs, d, p), (y, znew) in zip(dest, _scan_chunks(chains)):
        y_ref[s, :, cs] = y
        z_ref[s, d, p] = znew


def _rwkv_scan(prep, batch, s_tot, c_len):
    (v, _bonus, at_f, rt_f, bt_f, kt_f, bb_f, kb_f, pl_f, at_b, rt_b, bt_b, kt_b, bb_b, kb_b, pl_b) = prep
    n, width = v.shape
    L = SCAN_CHUNK
    nch = s_tot // L
    ncc = c_len // L
    npairs = width // LANES
    nb = 2 if batch % 2 == 0 else 1

    def fchunk(c):
        return c

    def bchunk(c):
        return jnp.where(c < ncc, ncc - 1 - c, nch - 1 - (c - ncc))

    def tok(chunk):
        return pl.BlockSpec((nb, L, width), lambda b, c: (b, chunk(c), 0))

    def pls(chunk):
        return pl.BlockSpec((nb, 1, 1, width), lambda b, c: (b, chunk(c), 0, 0))

    def tok3(a):
        return a.reshape(batch, s_tot, width)

    def pl4(a):
        return a.reshape(batch, nch, 1, width)

    kern = functools.partial(_rwkv_scan_kernel, npairs=npairs, nb=nb)
    fwd_in = [tok3(a) for a in (v, at_f, rt_f, bt_f, kt_f, bb_f, kb_f)] + [pl4(pl_f)]
    bwd_in = [tok3(a) for a in (v, at_b, rt_b, bt_b, kt_b, bb_b, kb_b)] + [pl4(pl_b)]
    yf, yb = pl.pallas_call(
        kern,
        grid=(batch // nb, nch),
        in_specs=[tok(fchunk)] * 7 + [pls(fchunk)] + [tok(bchunk)] * 7 + [pls(bchunk)],
        out_specs=[tok(fchunk), tok(bchunk)],
        out_shape=[jax.ShapeDtypeStruct((batch, s_tot, width), F32)] * 2,
        scratch_shapes=[pltpu.VMEM((nb, 2, npairs, LANES, LANES), F32)],
        compiler_params=_cparams("parallel", "arbitrary"),
        name="rwkv_scan",
    )(*fwd_in, *bwd_in)
    return yf.reshape(n, width), yb.reshape(n, width)


def _rwkv_readout_kernel(yf_ref, yb_ref, bonus_ref, gl_ref, g2_ref, lg_ref, lb_ref, o_ref):
    avg = _group_matrix(1.0 / HEAD_DIM)
    gate = _dot(_bf(jax.nn.sigmoid(gl_ref[...].astype(F32))), g2_ref[...])
    for c in range(o_ref.shape[1] // LANES):
        cs = slice(c * LANES, (c + 1) * LANES)
        y = yf_ref[:, cs] + yb_ref[:, cs]
        mu = _dot_split(y, avg, 2)
        dy = y - mu
        var = _dot_split(dy * dy, avg, 2)
        yn = dy * lax.rsqrt(var + LNX_EPS) * lg_ref[:, cs] + lb_ref[:, cs]
        o_ref[:, cs] = _bf((yn + bonus_ref[:, cs].astype(F32)) * gate[:, cs])


def _rwkv_readout(yf, yb, bonus, big, g2, lnx_g, lnx_b):
    n, width = yf.shape
    tm = _pick_tile(n, (512, 256))
    tok = pl.BlockSpec((tm, width), lambda i: (i, 0))
    full = lambda a: pl.BlockSpec(a.shape, lambda i: (0,) * a.ndim)
    return pl.pallas_call(
        _rwkv_readout_kernel,
        grid=(n // tm,),
        in_specs=[tok, tok, tok, pl.BlockSpec((tm, LANES), lambda i: (i, COL_GLOW // LANES)),
                  full(g2), full(lnx_g), full(lnx_b)],
        out_specs=tok,
        out_shape=jax.ShapeDtypeStruct((n, width), BF16),
        compiler_params=_cparams("parallel"),
        name="rwkv_readout",
    )(yf, yb, bonus, big, g2, lnx_g, lnx_b)


def _merge_kernel(oa_ref, ob_ref, oc_ref, gt_ref, x_ref, wb_ref, wo_ref, ng_ref, mod_ref, o_ref, *, nsub, d):
    y = None
    for br, ref in enumerate((oa_ref, ob_ref, oc_ref)):
        g = jax.nn.sigmoid(gt_ref[:, br * d:(br + 1) * d].astype(F32))
        t = g * _dot(ref[...], wb_ref[br])
        y = t if y is None else y + t
    o = _dot(_bf(y), wo_ref[...])
    o_ref[...] = x_ref[...] + _mod_rows(mod_ref, nsub, 2, d) * _rms(o, ng_ref[...])


def _merge(oa, ob, oc, big, x, wb, wo, ng, modblk):
    n, d = x.shape
    width = oa.shape[1]
    tm = _pick_tile(n, (512, 256))
    nsub = tm // MOD_ROWS
    kern = functools.partial(_merge_kernel, nsub=nsub, d=d)
    br = pl.BlockSpec((tm, width), lambda i: (i, 0))
    return pl.pallas_call(
        kern,
        grid=(n // tm,),
        in_specs=[br, br, br,
                  pl.BlockSpec((tm, 3 * d), lambda i: (i, COL_GATES // (3 * d))),
                  pl.BlockSpec((tm, d), lambda i: (i, 0)),
                  pl.BlockSpec(wb.shape, lambda i: (0, 0, 0)),
                  pl.BlockSpec(wo.shape, lambda i: (0, 0)),
                  pl.BlockSpec((1, d), lambda i: (0, 0)),
                  pl.BlockSpec((nsub, 1, 6 * d), lambda i: (i, 0, 0))],
        out_specs=pl.BlockSpec((tm, d), lambda i: (i, 0)),
        out_shape=jax.ShapeDtypeStruct((n, d), F32),
        compiler_params=_cparams("parallel"),
        name="merge_out_proj",
    )(oa, ob, oc, big, x, wb, wo, ng, modblk)


def _ffn_kernel(x_ref, g_ref, mod_ref, wg_ref, wu_ref, wd_ref, ng_ref, o_ref, h_ref, acc_ref, *, nsub, d):
    j = pl.program_id(1)

    @pl.when(j == 0)
    def _():
        y = _rms(x_ref[...], g_ref[...])
        h_ref[...] = _bf(y * (1.0 + _mod_rows(mod_ref, nsub, 4, d)) + _mod_rows(mod_ref, nsub, 3, d))
        acc_ref[...] = jnp.zeros_like(acc_ref)

    h = h_ref[...]
    g = _dot(h, wg_ref[...])
    u = _dot(h, wu_ref[...])
    acc_ref[...] += _dot(_bf(g * jax.nn.sigmoid(g) * u), wd_ref[...])

    @pl.when(j == pl.num_programs(1) - 1)
    def _():
        o_ref[...] = x_ref[...] + _mod_rows(mod_ref, nsub, 5, d) * _rms(acc_ref[...], ng_ref[...])


def _dense_ffn(x, gain_in, gain_out, modblk, w_gu, w_down):
    n, d = x.shape
    f = w_down.shape[0]
    tm = _pick_tile(n, (1024, 512, 256))
    tf = _pick_tile(f, (1408, 1024, 512, 256, 128))
    nf = f // tf
    nsub = tm // MOD_ROWS
    kern = functools.partial(_ffn_kernel, nsub=nsub, d=d)
    return pl.pallas_call(
        kern,
        grid=(n // tm, nf),
        in_specs=[pl.BlockSpec((tm, d), lambda i, j: (i, 0)),
                  pl.BlockSpec((1, d), lambda i, j: (0, 0)),
                  pl.BlockSpec((nsub, 1, 6 * d), lambda i, j: (i, 0, 0)),
                  pl.BlockSpec((d, tf), lambda i, j: (0, j)),
                  pl.BlockSpec((d, tf), lambda i, j: (0, j + nf)),
                  pl.BlockSpec((tf, d), lambda i, j: (j, 0)),
                  pl.BlockSpec((1, d), lambda i, j: (0, 0))],
        out_specs=pl.BlockSpec((tm, d), lambda i, j: (i, 0)),
        out_shape=jax.ShapeDtypeStruct((n, d), F32),
        scratch_shapes=[pltpu.VMEM((tm, d), BF16), pltpu.VMEM((tm, d), F32)],
        compiler_params=_cparams("parallel", "arbitrary"),
        name="dense_swiglu_ffn",
    )(x, gain_in, modblk, w_gu, w_gu, w_down, gain_out)


def _router_kernel(x_ref, g_ref, mod_ref, wr_ref, h_ref, comb_ref, rank_ref, combt_ref, rankt_ref, cnt_ref,
                   *, nsub, d):
    lane = lax.broadcasted_iota(jnp.int32, (MOD_ROWS, LANES), 1)
    r2 = lax.broadcasted_iota(jnp.int32, (MOD_ROWS, MOD_ROWS), 0)
    c2 = lax.broadcasted_iota(jnp.int32, (MOD_ROWS, MOD_ROWS), 1)
    tri = jnp.where(c2 < r2, 1.0, 0.0).astype(BF16)
    ninf = jnp.float32(-jnp.inf)
    running = jnp.zeros((1, LANES), F32)
    for s in range(nsub):
        rows = slice(s * MOD_ROWS, (s + 1) * MOD_ROWS)
        m = mod_ref[s]
        h = _rms(x_ref[rows, :], g_ref[...]) * (1.0 + m[:, 4 * d:5 * d]) + m[:, 3 * d:4 * d]
        hh = _bf(h)
        h_ref[rows, :] = hh
        hl = _bf(h - hh.astype(F32))
        both = _dot(hh, wr_ref[...])
        logits = both[:, :LANES] + both[:, LANES:] + _dot(hl, wr_ref[:, :LANES])
        logits = jnp.where(lane < N_EXPERTS, logits, ninf)
        m1 = jnp.max(logits, axis=-1, keepdims=True)
        i1 = jnp.min(jnp.where(logits == m1, lane, LANES), axis=-1, keepdims=True)
        rest = jnp.where(lane == i1, ninf, logits)
        m2 = jnp.max(rest, axis=-1, keepdims=True)
        i2 = jnp.min(jnp.where(rest == m2, lane, LANES), axis=-1, keepdims=True)
        e2 = jnp.exp(m2 - m1)
        w1 = 1.0 / (1.0 + e2)
        comb = jnp.where(lane == i1, w1, 0.0) + jnp.where(lane == i2, e2 * w1, 0.0)
        ind = jnp.where(comb > 0.0, 1.0, 0.0)
        rank = _dot(tri, _bf(ind)) + running
        running = running + jnp.sum(ind, axis=0, keepdims=True)
        comb_ref[rows, :] = comb
        rank_ref[rows, :] = rank
        combt_ref[:, rows] = comb.T[:N_EXPERTS, :]
        rankt_ref[:, rows] = rank.T[:N_EXPERTS, :]
    cnt_ref[0] = running


def _router(x, gain_in, modblk, w_router_pad, tm):
    n, d = x.shape
    nsub = tm // MOD_ROWS
    nt = n // tm
    kern = functools.partial(_router_kernel, nsub=nsub, d=d)
    tokm = pl.BlockSpec((tm, LANES), lambda i: (i, 0))
    expm = pl.BlockSpec((N_EXPERTS, tm), lambda i: (0, i))
    return pl.pallas_call(
        kern,
        grid=(nt,),
        in_specs=[pl.BlockSpec((tm, d), lambda i: (i, 0)),
                  pl.BlockSpec((1, d), lambda i: (0, 0)),
                  pl.BlockSpec((nsub, 1, 6 * d), lambda i: (i, 0, 0)),
                  pl.BlockSpec((d, 2 * LANES), lambda i: (0, 0))],
        out_specs=[pl.BlockSpec((tm, d), lambda i: (i, 0)), tokm, tokm, expm, expm,
                   pl.BlockSpec((1, 1, LANES), lambda i: (i, 0, 0))],
        out_shape=[jax.ShapeDtypeStruct((n, d), BF16),
                   jax.ShapeDtypeStruct((n, LANES), F32), jax.ShapeDtypeStruct((n, LANES), F32),
                   jax.ShapeDtypeStruct((N_EXPERTS, n), F32), jax.ShapeDtypeStruct((N_EXPERTS, n), F32),
                   jax.ShapeDtypeStruct((nt, 1, LANES), F32)],
        compiler_params=_cparams("parallel"),
        name="moe_router",
    )(x, gain_in, modblk, w_router_pad)


def _moe_kernel(cnt_ref, h_ref, comb_ref, rank_ref, combt_ref, rankt_ref, wg_ref, wu_ref, wd_ref, o_ref,
                xe_ref, y_ref, *, tm, rb):
    i = pl.program_id(0)
    e = pl.program_id(1)
    j = pl.program_id(2)
    nf = pl.num_programs(2)
    cnt = cnt_ref[i * N_EXPERTS + e]
    half, quarter = rb // 2, rb // 4
    nblk = cnt // rb
    tail0 = pl.multiple_of(nblk * rb, rb)
    rem = cnt - tail0

    def for_blocks(fn):
        lax.fori_loop(0, nblk, lambda b, carry: fn(pl.multiple_of(b * rb, rb), rb) or carry, 0)

        @pl.when(rem > half + quarter)
        def _():
            fn(tail0, rb)

        @pl.when(jnp.logical_and(rem > quarter, rem <= half + quarter))
        def _():
            fn(tail0, half)

        @pl.when(jnp.logical_or(jnp.logical_and(rem > 0, rem <= quarter),
                                jnp.logical_and(rem > half, rem <= half + quarter)))
        def _():
            fn(pl.multiple_of(tail0 + jnp.where(rem > half, half, 0), quarter), quarter)

    @pl.when(jnp.logical_and(e == 0, j == 0))
    def _():
        o_ref[...] = jnp.zeros_like(o_ref)

    @pl.when(j == 0)
    def _():
        key = jnp.where(combt_ref[pl.ds(e, 1), :] > 0.0, rankt_ref[pl.ds(e, 1), :], -1.0)

        def gather(r0, nr):
            want = (r0 + lax.broadcasted_iota(jnp.int32, (nr, tm), 0)).astype(F32)
            sel = jnp.where(key == want, 1.0, 0.0).astype(BF16)
            xe_ref[pl.ds(r0, nr), :] = _bf(_dot(sel, h_ref[...]))

        for_blocks(gather)

    def expert(r0, nr):
        rows = pl.ds(r0, nr)
        xb = xe_ref[rows, :]
        g = _dot(xb, wg_ref[0])
        u = _dot(xb, wu_ref[0])
        part = _dot(_bf(g * jax.nn.sigmoid(g) * u), wd_ref[0])

        @pl.when(j == 0)
        def _():
            y_ref[rows, :] = _bf(part)

        @pl.when(j > 0)
        def _():
            y_ref[rows, :] = _bf(y_ref[rows, :].astype(F32) + part)

    for_blocks(expert)

    @pl.when(j == nf - 1)
    def _():
        lane = lax.broadcasted_iota(jnp.int32, (tm, LANES), 1)
        rank_col = jnp.sum(jnp.where(lane == e, rank_ref[...], 0.0), axis=1, keepdims=True)
        w_col = jnp.sum(jnp.where(lane == e, comb_ref[...], 0.0), axis=1, keepdims=True)

        def scatter(r0, nr):
            want = (r0 + lax.broadcasted_iota(jnp.int32, (tm, nr), 1)).astype(F32)
            selw = _bf(jnp.where(rank_col == want, w_col, 0.0))
            o_ref[...] += _dot(selw, y_ref[pl.ds(r0, nr), :])

        for_blocks(scatter)


def _moe_ffn(hb, comb, rank, combt, rankt, counts, w_gu, w_down, tm):
    n, d = hb.shape
    ne, f, _ = w_down.shape
    tf = _pick_tile(f, (1792, 512, 256, 128))
    nf = f // tf
    rb = MOE_ROW_BLOCK
    kern = functools.partial(_moe_kernel, tm=tm, rb=rb)
    once = pl.Buffered(1)
    tokm = pl.BlockSpec((tm, LANES), lambda i, e, j, c: (i, 0), pipeline_mode=once)
    expm = pl.BlockSpec((N_EXPERTS, tm), lambda i, e, j, c: (0, i))
    grid_spec = pltpu.PrefetchScalarGridSpec(
        num_scalar_prefetch=1,
        grid=(n // tm, ne, nf),
        in_specs=[pl.BlockSpec((tm, d), lambda i, e, j, c: (i, 0), pipeline_mode=once), tokm, tokm, expm, expm,
                  pl.BlockSpec((1, d, tf), lambda i, e, j, c: (e, 0, j)),
                  pl.BlockSpec((1, d, tf), lambda i, e, j, c: (e, 0, j + nf)),
                  pl.BlockSpec((1, tf, d), lambda i, e, j, c: (e, j, 0))],
        out_specs=pl.BlockSpec((tm, d), lambda i, e, j, c: (i, 0), pipeline_mode=once),
        scratch_shapes=[pltpu.VMEM((tm, d), BF16), pltpu.VMEM((tm, d), BF16)])
    return pl.pallas_call(
        kern,
        grid_spec=grid_spec,
        out_shape=jax.ShapeDtypeStruct((n, d), F32),
        compiler_params=_cparams("parallel", "arbitrary", "arbitrary"),
        name="moe_swiglu_ffn",
    )(counts, hb, comb, rank, combt, rankt, w_gu, w_gu, w_down)


def _residual_kernel(x_ref, y_ref, ng_ref, mod_ref, o_ref, *, nsub, d):
    o_ref[...] = x_ref[...] + _mod_rows(mod_ref, nsub, 5, d) * _rms(y_ref[...].astype(F32), ng_ref[...])


def _gated_residual(x, y, gain_out, modblk):
    n, d = x.shape
    tm = _pick_tile(n, (512, 256))
    nsub = tm // MOD_ROWS
    tok = pl.BlockSpec((tm, d), lambda i: (i, 0))
    return pl.pallas_call(
        functools.partial(_residual_kernel, nsub=nsub, d=d),
        grid=(n // tm,),
        in_specs=[tok, tok, pl.BlockSpec((1, d), lambda i: (0, 0)),
                  pl.BlockSpec((nsub, 1, 6 * d), lambda i: (i, 0, 0))],
        out_specs=tok,
        out_shape=jax.ShapeDtypeStruct((n, d), F32),
        compiler_params=_cparams("parallel"),
        name="moe_gated_residual",
    )(x, y, gain_out, modblk)


def _rope_tables(t_len, c_len):
    pairs = HEAD_DIM // 4
    rows = t_len // GRID_W
    row = jnp.repeat(jnp.arange(rows, dtype=F32), GRID_W)
    col = jnp.tile(jnp.arange(GRID_W, dtype=F32), rows)
    freqs = ROPE_BASE ** (-jnp.arange(pairs, dtype=F32) / pairs)
    ar = row[:, None] * freqs
    ac = col[:, None] * freqs
    cos = jnp.concatenate([jnp.cos(ar), jnp.cos(ar), jnp.cos(ac), jnp.cos(ac)], axis=1)
    sin = jnp.concatenate([-jnp.sin(ar), jnp.sin(ar), -jnp.sin(ac), jnp.sin(ac)], axis=1)
    cos = jnp.concatenate([jnp.ones((c_len, HEAD_DIM), F32), cos], axis=0)
    sin = jnp.concatenate([jnp.zeros((c_len, HEAD_DIM), F32), sin], axis=0)
    return jnp.tile(cos, (1, 2)), jnp.tile(sin, (1, 2))


def _block_diag2(w):
    z = jnp.zeros_like(w[0])
    return jnp.concatenate([jnp.concatenate([w[0], z], axis=1), jnp.concatenate([z, w[1]], axis=1)], axis=0)


def kernel(x, c, ctx, c_ctx, w_mod, b_mod, norm_gain, w_in, qk_gain, rwkv_conv, decay_w0, decay_w2, iclr_a0, iclr_a2, key_k, bonus_rk, gate_g2, lnx_gain, lnx_bias, cmlp_ln_gain, cmlp_ln_bias, cmlp_ws, cmlp_bs, w_branch, w_out, ffn_w_gu, ffn_w_down, moe_router, moe_w_gu, moe_w_down):
    batch, t_len, d = x.shape
    c_len = ctx.shape[1]
    depth = w_mod.shape[0]
    s_tot = c_len + t_len
    n = batch * s_tot
    assert c_len % MOD_ROWS == 0 and t_len % MOD_ROWS == 0 and d % LANES == 0
    width = bonus_rk.shape[1] * bonus_rk.shape[2]
    nheads = width // HEAD_DIM

    xs = jnp.concatenate([ctx, x], axis=1).reshape(n, d)

    mod_rows = 8 * ((batch + 1 + 7) // 8)
    cvec = jnp.zeros((mod_rows, d), F32).at[0].set(c_ctx).at[1:batch + 1].set(c)
    mods = _modulation(cvec, w_mod, b_mod)
    mod_ctx = jnp.broadcast_to(mods[:, 0:1, None, :], (depth, batch, c_len // MOD_ROWS, 6 * d))
    mod_lat = jnp.broadcast_to(mods[:, 1:batch + 1, None, :], (depth, batch, t_len // MOD_ROWS, 6 * d))
    modblk_all = jnp.concatenate([mod_ctx, mod_lat], axis=2).reshape(depth, n // MOD_ROWS, 1, 6 * d)

    order = np.array(Q_HEAD_ORDER)
    nl = depth
    wq = w_in[:, :, 2048:2560].reshape(nl, d, nheads, HEAD_DIM)[:, :, order].reshape(nl, d, width)
    w_in_p = jnp.concatenate([
        w_in[:, :, 256:1792], wq, w_in[:, :, 2688:3712], w_in[:, :, 3712:6784],
        w_in[:, :, 0:128], w_in[:, :, 128:256], w_in[:, :, 1792:1920], w_in[:, :, 1920:2048],
        w_in[:, :, 2560:2688], jnp.zeros((nl, d, IN_PAD - 6784), F32)], axis=2).astype(BF16)
    wb = w_branch.astype(BF16)
    wb0 = wb[:, 0].reshape(nl, nheads, HEAD_DIM, d)[:, order].reshape(nl, width, d)
    wb = jnp.concatenate([wb0[:, None], wb[:, 1:]], axis=1)
    wo = w_out.astype(BF16)
    cos, sin = _rope_tables(t_len, c_len)
    qg = jnp.tile(qk_gain[:, 0], (1, 2))[:, None, :]
    kg = jnp.tile(qk_gain[:, 1], (1, 2))[:, None, :]
    ws_b = cmlp_ws.astype(BF16)
    bs_b = jnp.broadcast_to(cmlp_bs[..., None], cmlp_bs.shape + (CMLP_CHUNK,))
    w2s = jnp.stack([_block_diag2(decay_w2[l]) for l in range(nl)]).astype(BF16)
    a2s = jnp.stack([_block_diag2(iclr_a2[l]) for l in range(nl)]).astype(BF16)
    w0 = decay_w0.reshape(nl, 1, 2 * width)
    a0 = iclr_a0.reshape(nl, 1, 2 * width)
    rk = bonus_rk.reshape(nl, 1, width)
    g2 = gate_g2.astype(BF16)
    ffn_gu = ffn_w_gu.astype(BF16)
    ffn_dn = ffn_w_down.astype(BF16)
    moe_gu = moe_w_gu.astype(BF16)
    moe_dn = moe_w_down.astype(BF16)
    router_pad = jnp.pad(moe_router, ((0, 0), (0, 0), (0, LANES - moe_router.shape[2])))
    router_hi = router_pad.astype(BF16)
    router_pad = jnp.concatenate([router_hi, (router_pad - router_hi.astype(F32)).astype(BF16)], axis=2)

    for l in range(depth):
        modblk = modblk_all[l]
        ng = norm_gain[l]
        big = _norm_mod_matmul(xs, ng[0:1], modblk, w_in_p[l], 0, 1)
        qh, kbd, vt = _qk_prep(big, qg[l], kg[l], cos, sin, s_tot)
        oa = _attention(qh, kbd, vt, batch, s_tot, c_len)
        prep = _rwkv_prep(big, rwkv_conv[l], w0[l], w2s[l], a0[l], a2s[l],
                          key_k[l, 0:1], key_k[l, 1:2], rk[l], s_tot, c_len)
        yf, yb = _rwkv_scan(prep, batch, s_tot, c_len)
        ob = _rwkv_readout(yf, yb, prep[1], big, g2[l], lnx_gain[l][None], lnx_bias[l][None])
        oc = _chunk_mlp(big, cmlp_ln_gain[l][None], cmlp_ln_bias[l][None], ws_b[l], bs_b[l])
        xs = _merge(oa, ob, oc, big, xs, wb[l], wo[l], ng[1:2], modblk)
        if l % 2 == 0:
            xs = _dense_ffn(xs, ng[2:3], ng[3:4], modblk, ffn_gu[l // 2], ffn_dn[l // 2])
        else:
            tmoe = _pick_tile(n, (2048, 1024, 512, 256))
            hb, comb, rank, combt, rankt, cnt = _router(xs, ng[2:3], modblk, router_pad[l // 2], tmoe)
            counts = cnt[:, 0, :N_EXPERTS].astype(jnp.int32).reshape(-1)
            y = _moe_ffn(hb, comb, rank, combt, rankt, counts, moe_gu[l // 2], moe_dn[l // 2], tmoe)
            xs = _gated_residual(xs, y, ng[3:4], modblk)
    return xs.reshape(batch, s_tot, d)[:, c_len:, :]
```

```python
import functools

import jax
import jax.numpy as jnp
import numpy as np
from jax import lax
from jax.experimental import pallas as pl
from jax.experimental.pallas import tpu as pltpu

F32 = jnp.float32
BF16 = jnp.bfloat16
HIGHEST = lax.Precision.HIGHEST

EPS = 1e-6
LNX_EPS = 64e-5
HEAD_DIM = 64
ROPE_BASE = 10000.0
GRID_W = 64
LANES = 128
MOD_ROWS = 256
SCAN_CHUNK = 64
CMLP_CHUNK = 128
ATT_TK = 256
ATT_VROWS = HEAD_DIM + 16
ATT_GROUP = 1
N_EXPERTS = 8
MOE_ROW_BLOCK = 512
VMEM_LIMIT = 56 * 1024 * 1024

COL_RKV, COL_Q, COL_UV, COL_GATES = 0, 1536, 2048, 3072
COL_K, COL_V, COL_WLOW, COL_ALOW, COL_GLOW = 6144, 6272, 6400, 6528, 6656
IN_PAD = 7168
Q_HEAD_ORDER = (0, 4, 1, 5, 2, 6, 3, 7)


def _cparams(*sem):
    return pltpu.CompilerParams(dimension_semantics=sem, vmem_limit_bytes=VMEM_LIMIT)


def _dot(a, b):
    return jnp.dot(a, b, preferred_element_type=F32)


def _dot_nt(a, b):
    return lax.dot_general(a, b, (((1,), (1,)), ((), ())), preferred_element_type=F32)


def _bf(x):
    return x.astype(BF16)


def _dot_split(a, b_exact, terms):
    acc = None
    rem = a
    for _ in range(terms):
        piece = _bf(rem)
        rem = rem - piece.astype(F32)
        part = _dot(piece, b_exact)
        acc = part if acc is None else acc + part
    return acc


def _dot_split_left(a_exact, b, terms):
    acc = None
    rem = b
    for _ in range(terms):
        piece = _bf(rem)
        rem = rem - piece.astype(F32)
        part = _dot(a_exact, piece)
        acc = part if acc is None else acc + part
    return acc


def _group_matrix(scale):
    r = lax.broadcasted_iota(jnp.int32, (LANES, LANES), 0) // HEAD_DIM
    c = lax.broadcasted_iota(jnp.int32, (LANES, LANES), 1) // HEAD_DIM
    return jnp.where(r == c, scale, 0.0).astype(BF16)


def _pick_tile(n, candidates):
    for t in candidates:
        if n % t == 0:
            return t
    raise ValueError(f"no tile in {candidates} divides {n}")


def _mod_rows(mod_ref, nsub, idx, d):
    parts = [jnp.broadcast_to(mod_ref[s][:, idx * d:(idx + 1) * d], (MOD_ROWS, d)) for s in range(nsub)]
    return parts[0] if nsub == 1 else jnp.concatenate(parts, axis=0)


def _rms(x, g):
    return x * lax.rsqrt(jnp.mean(x * x, axis=-1, keepdims=True) + EPS) * g


def _mod_kernel(c_ref, w_ref, b_ref, o_ref):
    cv = c_ref[...]
    s = cv * jax.nn.sigmoid(cv)
    o_ref[0] = jnp.dot(s, w_ref[0], precision=HIGHEST, preferred_element_type=F32) + b_ref[0]


def _modulation(cvec, w_mod, b_mod):
    nl, d, d6 = w_mod.shape
    rows = cvec.shape[0]
    tn = 1024
    return pl.pallas_call(
        _mod_kernel,
        grid=(nl, d6 // tn),
        in_specs=[pl.BlockSpec((rows, d), lambda l, j: (0, 0)),
                  pl.BlockSpec((1, d, tn), lambda l, j: (l, 0, j)),
                  pl.BlockSpec((1, 1, tn), lambda l, j: (l, 0, j))],
        out_specs=pl.BlockSpec((1, rows, tn), lambda l, j: (l, 0, j)),
        out_shape=jax.ShapeDtypeStruct((nl, rows, d6), F32),
        compiler_params=_cparams("parallel", "parallel"),
        name="modulation",
    )(cvec, w_mod, b_mod.reshape(nl, 1, d6))


def _nmm_kernel(x_ref, g_ref, mod_ref, w_ref, o_ref, h_ref, *, nsub, d, shift_idx, scale_idx):
    @pl.when(pl.program_id(1) == 0)
    def _():
        y = _rms(x_ref[...], g_ref[...])
        sc = _mod_rows(mod_ref, nsub, scale_idx, d)
        sh = _mod_rows(mod_ref, nsub, shift_idx, d)
        h_ref[...] = _bf(y * (1.0 + sc) + sh)

    o_ref[...] = _bf(_dot(h_ref[...], w_ref[...]))


def _norm_mod_matmul(x, gain, modblk, w, shift_idx, scale_idx):
    n, d = x.shape
    nout = w.shape[1]
    tm = _pick_tile(n, (1024, 512, 256))
    tn = _pick_tile(nout, (1792, 1024))
    nsub = tm // MOD_ROWS
    kern = functools.partial(_nmm_kernel, nsub=nsub, d=d, shift_idx=shift_idx, scale_idx=scale_idx)
    return pl.pallas_call(
        kern,
        grid=(n // tm, nout // tn),
        in_specs=[pl.BlockSpec((tm, d), lambda i, j: (i, 0)),
                  pl.BlockSpec((1, d), lambda i, j: (0, 0)),
                  pl.BlockSpec((nsub, 1, 6 * d), lambda i, j: (i, 0, 0)),
                  pl.BlockSpec((d, tn), lambda i, j: (0, j))],
        out_specs=pl.BlockSpec((tm, tn), lambda i, j: (i, j)),
        out_shape=jax.ShapeDtypeStruct((n, nout), BF16),
        scratch_shapes=[pltpu.VMEM((tm, d), BF16)],
        compiler_params=_cparams("parallel", "arbitrary"),
        name="norm_mod_in_proj",
    )(x, gain, modblk, w)


def _qkprep_kernel(q_ref, k_ref, v_ref, qg_ref, kg_ref, cos_ref, sin_ref, qo_ref, ko_ref, vo_ref):
    cos = cos_ref[...]
    sin = sin_ref[...]
    avg = _group_matrix(1.0 / HEAD_DIM)
    lane = lax.broadcasted_iota(jnp.int32, cos.shape, 1)
    first = (lane % 32) < 16
    left = lane < HEAD_DIM

    def norm_rope(x, g):
        ms = _dot_split(x * x, avg, 2)
        xn = x * lax.rsqrt(ms + EPS) * g
        partner = jnp.where(first, pltpu.roll(xn, LANES - 16, 1), pltpu.roll(xn, 16, 1))
        return xn * cos + partner * sin

    qscale = (HEAD_DIM ** -0.5) * float(np.log2(np.e))
    for j in range(q_ref.shape[1] // LANES):
        q = q_ref[:, j * LANES:(j + 1) * LANES].astype(F32)
        qo_ref[:, j * LANES:(j + 1) * LANES] = _bf(norm_rope(q, qg_ref[...]) * qscale)
    k = norm_rope(k_ref[...].astype(F32), kg_ref[...])
    zero = jnp.zeros_like(k)
    k0 = _bf(jnp.where(left, k, zero))
    k1 = _bf(jnp.where(left, zero, k))
    v = v_ref[...].astype(F32)
    ones = jnp.ones((ATT_VROWS - HEAD_DIM, ATT_TK), BF16)
    for c in range(vo_ref.shape[0]):
        rows = slice(c * ATT_TK, (c + 1) * ATT_TK)
        ko_ref[c, :ATT_TK, :] = k0[rows]
        ko_ref[c, ATT_TK:, :] = k1[rows]
        vt = _bf(v[rows, :].T)
        for t in range(2):
            vo_ref[c, t, :HEAD_DIM, :] = vt[t * HEAD_DIM:(t + 1) * HEAD_DIM]
            vo_ref[c, t, HEAD_DIM:, :] = ones


def _qk_prep(big, qg, kg, cos, sin, s_tot):
    n = big.shape[0]
    tm = MOD_ROWS
    npos = s_tot // tm
    qw = 512
    vchunks = tm // ATT_TK
    return pl.pallas_call(
        _qkprep_kernel,
        grid=(n // tm,),
        in_specs=[pl.BlockSpec((tm, qw), lambda i: (i, COL_Q // qw)),
                  pl.BlockSpec((tm, LANES), lambda i: (i, COL_K // LANES)),
                  pl.BlockSpec((tm, LANES), lambda i: (i, COL_V // LANES)),
                  pl.BlockSpec((1, LANES), lambda i: (0, 0)),
                  pl.BlockSpec((1, LANES), lambda i: (0, 0)),
                  pl.BlockSpec((tm, LANES), lambda i: (i % npos, 0)),
                  pl.BlockSpec((tm, LANES), lambda i: (i % npos, 0))],
        out_specs=[pl.BlockSpec((tm, qw), lambda i: (i, 0)),
                   pl.BlockSpec((vchunks, 2 * ATT_TK, LANES), lambda i: (i, 0, 0)),
                   pl.BlockSpec((vchunks, 2, ATT_VROWS, ATT_TK), lambda i: (i, 0, 0, 0))],
        out_shape=[jax.ShapeDtypeStruct((n, qw), BF16),
                   jax.ShapeDtypeStruct((n // ATT_TK, 2 * ATT_TK, LANES), BF16),
                   jax.ShapeDtypeStruct((n // ATT_TK, 2, ATT_VROWS, ATT_TK), BF16)],
        compiler_params=_cparams("parallel"),
        name="qk_norm_rope",
    )(big, big, big, qg, kg, cos, sin)


def _attn_kernel(q_ref, k_ref, vt_ref, o_ref, acc_ref, sa_ref, sb_ref, *, tq, tk, n_ctx_q, n_ctx_kv, n_kv):
    i = pl.program_id(1)
    nkv = jnp.where(i < n_ctx_q, n_ctx_kv, n_kv)
    hd = HEAD_DIM
    npair = q_ref.shape[1] // LANES
    nh = 2 * npair
    qs = [q_ref[:, j * LANES:(j + 1) * LANES] for j in range(npair)]
    vr = ATT_VROWS
    acc_ref[...] = jnp.zeros_like(acc_ref)

    def scores_to(dst_ref, kb, h):
        s = _dot_nt(kb[(h % 2) * tk:(h % 2 + 1) * tk], qs[h // 2])
        dst_ref[h, :tk, :] = s
        dst_ref[h, tk:, :] = jnp.broadcast_to(jnp.max(s, axis=0, keepdims=True), (8, tq))

    def consume(src_ref, c, h, mh):
        n = jnp.maximum(mh, src_ref[h, tk:tk + 1, :])
        p = _bf(jnp.exp2(src_ref[h, :tk, :] - n))
        rows = slice(h * vr, (h + 1) * vr)
        acc_ref[rows, :] = acc_ref[rows, :] * jnp.exp2(mh - n) + _dot(vt_ref[c, h % 2], p)
        return n

    def step(src_ref, dst_ref, c, m, prefetch):
        kb = k_ref[c + 1] if prefetch else None
        new_m = []
        for h in range(nh):
            if prefetch:
                scores_to(dst_ref, kb, h)
            new_m.append(consume(src_ref, c, h, m[h]))
        return tuple(new_m)

    def body(u, m):
        c = 2 * u
        m = step(sa_ref, sb_ref, c, m, True)
        return step(sb_ref, sa_ref, c + 1, m, True)

    kb0 = k_ref[0]
    for h in range(nh):
        scores_to(sa_ref, kb0, h)
    m = lax.fori_loop(0, (nkv - 1) // 2, body, (jnp.full((1, tq), -1e30, F32),) * nh)
    step(sa_ref, sb_ref, nkv - 1, m, False)
    for j in range(npair):
        o = [acc_ref[h * vr:h * vr + hd, :] * (1.0 / acc_ref[h * vr + hd:h * vr + hd + 1, :]) for h in (2 * j, 2 * j + 1)]
        o_ref[:, j * LANES:(j + 1) * LANES] = _bf(jnp.concatenate(o, axis=0).T)


def _attention(qh, kbd, vt, batch, s_tot, c_len):
    n, qw = qh.shape
    tq = 256
    tk = ATT_TK
    nq = s_tot // tq
    assert (c_len // tk) % 2 == 1 and (s_tot // tk) % 2 == 1, "the key-chunk loop is unrolled by two plus a tail"
    kern = functools.partial(_attn_kernel, tq=tq, tk=tk, n_ctx_q=c_len // tq,
                             n_ctx_kv=c_len // tk, n_kv=s_tot // tk)
    return pl.pallas_call(
        kern,
        grid=(batch, nq),
        in_specs=[pl.BlockSpec((tq, qw), lambda b, i: (b * nq + i, 0)),
                  pl.BlockSpec((s_tot // tk, 2 * tk, LANES), lambda b, i: (b, 0, 0)),
                  pl.BlockSpec((s_tot // tk, 2, ATT_VROWS, tk), lambda b, i: (b, 0, 0, 0))],
        out_specs=pl.BlockSpec((tq, qw), lambda b, i: (b * nq + i, 0)),
        out_shape=jax.ShapeDtypeStruct((n, qw), BF16),
        scratch_shapes=[pltpu.VMEM((2 * (qw // LANES) * ATT_VROWS, tq), F32),
                        pltpu.VMEM((2 * (qw // LANES), tk + 8, tq), F32),
                        pltpu.VMEM((2 * (qw // LANES), tk + 8, tq), F32)],
        compiler_params=_cparams("parallel", "parallel"),
        name="gqa_attention",
    )(qh, kbd, vt)


def _cmlp_kernel(uv_ref, lng_ref, lnb_ref, ws_ref, bs_ref, o_ref, *, nchunk, width):
    x = uv_ref[...].astype(F32)
    g = 0.5 * x * (1.0 + jnp.tanh(0.7978845608028654 * (x + 0.044715 * (x * x * x))))
    u = g[:, :width]
    v = g[:, width:]
    mu = jnp.mean(v, axis=-1, keepdims=True)
    dv = v - mu
    var = jnp.mean(dv * dv, axis=-1, keepdims=True)
    vn = _bf(dv * lax.rsqrt(var + EPS) * lng_ref[...] + lnb_ref[...])
    ngroups = width // CMLP_CHUNK
    for c in range(nchunk):
        r0 = c * CMLP_CHUNK
        for gi in range(ngroups):
            c0 = gi * CMLP_CHUNK
            s = _dot(ws_ref[gi], vn[r0:r0 + CMLP_CHUNK, c0:c0 + CMLP_CHUNK]) + bs_ref[gi]
            o_ref[r0:r0 + CMLP_CHUNK, c0:c0 + CMLP_CHUNK] = _bf(u[r0:r0 + CMLP_CHUNK, c0:c0 + CMLP_CHUNK] * s)


def _chunk_mlp(big, ln_g, ln_b, ws, bs_b):
    n = big.shape[0]
    width = ln_g.shape[1]
    tr = _pick_tile(n, (512, 256, 128))
    kern = functools.partial(_cmlp_kernel, nchunk=tr // CMLP_CHUNK, width=width)
    ng = ws.shape[0]
    return pl.pallas_call(
        kern,
        grid=(n // tr,),
        in_specs=[pl.BlockSpec((tr, 2 * width), lambda i: (i, COL_UV // (2 * width))),
                  pl.BlockSpec((1, width), lambda i: (0, 0)),
                  pl.BlockSpec((1, width), lambda i: (0, 0)),
                  pl.BlockSpec((ng, CMLP_CHUNK, CMLP_CHUNK), lambda i: (0, 0, 0)),
                  pl.BlockSpec((ng, CMLP_CHUNK, CMLP_CHUNK), lambda i: (0, 0, 0))],
        out_specs=pl.BlockSpec((tr, width), lambda i: (i, 0)),
        out_shape=jax.ShapeDtypeStruct((n, width), BF16),
        compiler_params=_cparams("parallel"),
        name="chunk_gmlp",
    )(big, ln_g, ln_b, ws, bs_b)


def _rwkv_prep_kernel(x_ref, xp_ref, xn_ref, lo_ref, conv_ref, w0_ref, w2_ref, a0_ref, a2_ref,
                      kk0_ref, kk1_ref, rk_ref,
                      v_o, bonus_o, at_f, rt_f, bt_f, kt_f, bb_f, kb_f, pl_f,
                      at_b, rt_b, bt_b, kt_b, bb_b, kb_b, pl_b, *, tm, width, blocks_per_seq, ctx_blocks):
    i = pl.program_id(0)
    j = i % blocks_per_seq
    is_first = jnp.logical_or(j == 0, j == ctx_blocks)
    is_last = jnp.logical_or(j == ctx_blocks - 1, j == blocks_per_seq - 1)
    row = lax.broadcasted_iota(jnp.int32, (tm, width), 0)
    gsum = _group_matrix(1.0)
    halo = xp_ref.shape[0]

    def conv(c):
        cs = slice(c * width, (c + 1) * width)
        x = x_ref[:, cs].astype(F32)
        prev_row = jnp.where(is_first, 0.0, xp_ref[halo - 1:halo, cs].astype(F32))
        next_row = jnp.where(is_last, 0.0, xn_ref[0:1, cs].astype(F32))
        xprev = jnp.where(row == 0, prev_row, pltpu.roll(x, 1, 0))
        xnext = jnp.where(row == tm - 1, next_row, pltpu.roll(x, tm - 1, 0))
        return xprev * conv_ref[0:1, cs] + x * conv_ref[1:2, cs] + xnext * conv_ref[2:3, cs]

    r = conv(0)
    k = conv(1)
    v = conv(2)
    v_o[...] = _bf(v)

    def group_sum(x):
        parts = [_dot_split(x[:, c * LANES:(c + 1) * LANES], gsum, 2) for c in range(width // LANES)]
        return jnp.concatenate(parts, axis=1)

    kk = k * kk0_ref[...]
    kk = kk * lax.rsqrt(group_sum(kk * kk) + 1e-12)
    bonus_o[...] = _bf(group_sum(r * k * rk_ref[...]) * v)

    lo = lo_ref[...].astype(F32)
    wd = w0_ref[...] + _dot(_bf(jnp.tanh(lo[:, :LANES])), w2_ref[...])
    ad = jax.nn.sigmoid(a0_ref[...] + _dot(_bf(lo[:, LANES:]), a2_ref[...]))
    lw = -float(np.exp(-0.5) * np.log2(np.e)) * jax.nn.sigmoid(wd)

    r2 = lax.broadcasted_iota(jnp.int32, (tm, tm), 0)
    c2 = lax.broadcasted_iota(jnp.int32, (tm, tm), 1)
    same = (r2 // SCAN_CHUNK) == (c2 // SCAN_CHUNK)
    tri_pre = jnp.where(jnp.logical_and(same, c2 <= r2), 1.0, 0.0).astype(BF16)
    tri_suf = jnp.where(jnp.logical_and(same, c2 >= r2), 1.0, 0.0).astype(BF16)
    nchunk = tm // SCAN_CHUNK

    outs = ((at_f, rt_f, bt_f, kt_f, bb_f, kb_f, pl_f), (at_b, rt_b, bt_b, kt_b, bb_b, kb_b, pl_b))
    for d in range(2):
        ds_ = slice(d * width, (d + 1) * width)
        lwd = lw[:, ds_]
        pre = _dot_split_left(tri_pre, lwd, 3)
        suf = _dot_split_left(tri_suf, lwd, 3)
        cin, rem = (pre, suf - lwd) if d == 0 else (suf, pre - lwd)
        cex = cin - lwd
        a_d = ad[:, ds_]
        b = kk * a_d
        kd = k * (1.0 + (a_d - 1.0) * kk1_ref[...])
        at_o, rt_o, bt_o, kt_o, bb_o, kb_o, pl_o = outs[d]
        at_o[...] = _bf(-kk * jnp.exp2(cex))
        rt_o[...] = _bf(r * jnp.exp2(cin))
        pinv = jnp.exp2(-cin)
        bt_o[...] = _bf(b * pinv)
        kt_o[...] = _bf(kd * pinv)
        pend = jnp.exp2(rem)
        bb_o[...] = _bf(b * pend)
        kb_o[...] = _bf(kd * pend)
        for c in range(nchunk):
            last = (c + 1) * SCAN_CHUNK - 1
            pl_o[c] = jnp.exp2(pre[last:last + 1, :])


def _rwkv_prep(big, conv_w, w0, w2s, a0, a2s, kk0, kk1, rk, s_tot, c_len):
    n = big.shape[0]
    width = rk.shape[1]
    tm = MOD_ROWS
    halo = 16
    hb = tm // halo
    nhalo = n // halo
    nchunk = tm // SCAN_CHUNK
    kern = functools.partial(_rwkv_prep_kernel, tm=tm, width=width, blocks_per_seq=s_tot // tm,
                             ctx_blocks=c_len // tm)
    tok = pl.BlockSpec((tm, width), lambda i: (i, 0))
    plspec = pl.BlockSpec((nchunk, 1, width), lambda i: (i, 0, 0))
    tok_shape = jax.ShapeDtypeStruct((n, width), BF16)
    pl_shape = jax.ShapeDtypeStruct((n // SCAN_CHUNK, 1, width), F32)
    full = lambda a: pl.BlockSpec(a.shape, lambda i: (0,) * a.ndim)
    return pl.pallas_call(
        kern,
        grid=(n // tm,),
        in_specs=[pl.BlockSpec((tm, 3 * width), lambda i: (i, 0)),
                  pl.BlockSpec((halo, 3 * width), lambda i: (jnp.maximum(i * hb - 1, 0), 0)),
                  pl.BlockSpec((halo, 3 * width), lambda i: (jnp.minimum((i + 1) * hb, nhalo - 1), 0)),
                  pl.BlockSpec((tm, 2 * LANES), lambda i: (i, COL_WLOW // (2 * LANES))),
                  full(conv_w), full(w0), full(w2s), full(a0), full(a2s), full(kk0), full(kk1), full(rk)],
        out_specs=[tok, tok] + [tok] * 6 + [plspec] + [tok] * 6 + [plspec],
        out_shape=[tok_shape, tok_shape] + [tok_shape] * 6 + [pl_shape] + [tok_shape] * 6 + [pl_shape],
        compiler_params=_cparams("parallel"),
        name="rwkv_prepare",
    )(big, big, big, big, conv_w, w0, w2s, a0, a2s, kk0, kk1, rk)


def _scan_chunks(chains):
    L = SCAN_CHUNK
    lane = lax.broadcasted_iota(jnp.int32, (L, LANES), 1)
    m0 = _bf(jnp.where(lane < HEAD_DIM, 1.0, 0.0))
    m1 = _bf(jnp.where(lane < HEAD_DIM, 0.0, 1.0))

    def stack(x):
        blocks = [x[:, c:c + LANES] for c in range(0, x.shape[1], LANES)]
        top = [b * m0 for b in blocks]
        bot = [b * m1 for b in blocks]
        if len(blocks) == 1:
            return jnp.concatenate([top[0], bot[0]], axis=0)
        return jnp.concatenate([jnp.concatenate(top, axis=1), jnp.concatenate(bot, axis=1)], axis=0)

    trow = lax.broadcasted_iota(jnp.int32, (L, LANES), 0)
    tcol = lax.broadcasted_iota(jnp.int32, (L, LANES), 1) % L
    masks = {True: (tcol < trow, tcol <= trow), False: (tcol > trow, tcol >= trow)}
    eye = lax.broadcasted_iota(jnp.int32, (LANES, LANES), 0) == lax.broadcasted_iota(jnp.int32, (LANES, LANES), 1)
    fwd = [ch[9] for ch in chains]
    nc = range(len(chains))

    v_s = [stack(ch[6]) for ch in chains]
    big1 = [_dot_nt(jnp.concatenate([chains[i][0], chains[i][1]], axis=0),
                    jnp.concatenate([stack(chains[i][2]), stack(chains[i][3])], axis=0)) for i in nc]
    pm = [jnp.where(masks[fwd[i]][0], big1[i][:L, :LANES], 0.0) for i in nc]
    mak = [_bf(jnp.where(masks[fwd[i]][0], big1[i][:L, LANES:], 0.0)) for i in nc]
    lhs_top = [_bf(jnp.where(jnp.concatenate([masks[fwd[i]][1]] * 2, axis=1), big1[i][L:], 0.0)) for i in nc]
    mv = [_dot(mak[i], v_s[i]) for i in nc]
    px = [jnp.concatenate([chains[i][0].astype(F32), mv[i]], axis=1) for i in nc]
    steps = int(np.log2(L))
    for it in range(steps):
        if it < steps - 1:
            res = [_dot(_bf(pm[i]), stack(_bf(jnp.concatenate([pm[i], px[i]], axis=1)))) for i in nc]
            px = [px[i] + res[i][:, LANES:] for i in nc]
            pm = [res[i][:, :LANES] for i in nc]
        else:
            res = [_dot(_bf(pm[i]), stack(_bf(px[i]))) for i in nc]
            px = [px[i] + res[i] for i in nc]
    rhs2 = [jnp.concatenate([stack(_bf(px[i])), jnp.concatenate([jnp.zeros_like(v_s[i]), v_s[i]], axis=1)], axis=0)
            for i in nc]
    lhs_bot = [_bf(jnp.concatenate([stack(chains[i][4]), stack(chains[i][5])], axis=0).astype(F32).T)
               for i in nc]
    res2 = [_dot(jnp.concatenate([lhs_top[i], lhs_bot[i]], axis=0), rhs2[i]) for i in nc]
    lhs3 = [_bf(jnp.concatenate(
        [chains[i][1].astype(F32) + res2[i][:L, :LANES],
         res2[i][L:, :LANES] + jnp.where(eye, jnp.broadcast_to(chains[i][7], (LANES, LANES)), 0.0)], axis=0))
        for i in nc]
    res3 = [_dot(lhs3[i], _bf(chains[i][8])) for i in nc]
    return [(res3[i][:L] + res2[i][:L, LANES:], res3[i][L:] + res2[i][L:, LANES:]) for i in nc]


def _rwkv_scan_kernel(v_f, at_f, rt_f, bt_f, kt_f, bb_f, kb_f, pl_f,
                      v_b, at_b, rt_b, bt_b, kt_b, bb_b, kb_b, pl_b,
                      yf_ref, yb_ref, z_ref, *, npairs, nb):
    @pl.when(pl.program_id(1) == 0)
    def _():
        z_ref[...] = jnp.zeros_like(z_ref)

    dirs = ((v_f, at_f, rt_f, bt_f, kt_f, bb_f, kb_f, pl_f, yf_ref, True),
            (v_b, at_b, rt_b, bt_b, kt_b, bb_b, kb_b, pl_b, yb_ref, False))
    chains, dest = [], []
    for s in range(nb):
        for d, (v, at, rt, bt, kt, bb, kb, plr, y_ref, fwd) in enumerate(dirs):
            for p in range(npairs):
                cs = slice(p * LANES, (p + 1) * LANES)
                chains.append((at[s, :, cs], rt[s, :, cs], bt[s, :, cs], kt[s, :, cs], bb[s, :, cs], kb[s, :, cs],
                               v[s, :, cs], plr[s, 0][:, cs], z_ref[s, d, p], fwd))
                dest.append((y_ref, s, cs, d, p))
    for (y_ref, s, cs, d, p), (y, znew) in zip(dest, _scan_chunks(chains)):
        y_ref[s, :, cs] = y
        z_ref[s, d, p] = znew


def _rwkv_scan(prep, batch, s_tot, c_len):
    (v, _bonus, at_f, rt_f, bt_f, kt_f, bb_f, kb_f, pl_f, at_b, rt_b, bt_b, kt_b, bb_b, kb_b, pl_b) = prep
    n, width = v.shape
    L = SCAN_CHUNK
    nch = s_tot // L
    ncc = c_len // L
    npairs = width // LANES
    nb = _pick_tile(batch, (4, 2, 1))

    def fchunk(c):
        return c

    def bchunk(c):
        return jnp.where(c < ncc, ncc - 1 - c, nch - 1 - (c - ncc))

    def tok(chunk):
        return pl.BlockSpec((nb, L, width), lambda b, c: (b, chunk(c), 0))

    def pls(chunk):
        return pl.BlockSpec((nb, 1, 1, width), lambda b, c: (b, chunk(c), 0, 0))

    def tok3(a):
        return a.reshape(batch, s_tot, width)

    def pl4(a):
        return a.reshape(batch, nch, 1, width)

    kern = functools.partial(_rwkv_scan_kernel, npairs=npairs, nb=nb)
    fwd_in = [tok3(a) for a in (v, at_f, rt_f, bt_f, kt_f, bb_f, kb_f)] + [pl4(pl_f)]
    bwd_in = [tok3(a) for a in (v, at_b, rt_b, bt_b, kt_b, bb_b, kb_b)] + [pl4(pl_b)]
    yf, yb = pl.pallas_call(
        kern,
        grid=(batch // nb, nch),
        in_specs=[tok(fchunk)] * 7 + [pls(fchunk)] + [tok(bchunk)] * 7 + [pls(bchunk)],
        out_specs=[tok(fchunk), tok(bchunk)],
        out_shape=[jax.ShapeDtypeStruct((batch, s_tot, width), F32)] * 2,
        scratch_shapes=[pltpu.VMEM((nb, 2, npairs, LANES, LANES), F32)],
        compiler_params=_cparams("parallel", "arbitrary"),
        name="rwkv_scan",
    )(*fwd_in, *bwd_in)
    return yf.reshape(n, width), yb.reshape(n, width)


def _rwkv_readout_kernel(yf_ref, yb_ref, bonus_ref, gl_ref, g2_ref, lg_ref, lb_ref, o_ref):
    avg = _group_matrix(1.0 / HEAD_DIM)
    gate = _dot(_bf(jax.nn.sigmoid(gl_ref[...].astype(F32))), g2_ref[...])
    for c in range(o_ref.shape[1] // LANES):
        cs = slice(c * LANES, (c + 1) * LANES)
        y = yf_ref[:, cs] + yb_ref[:, cs]
        mu = _dot_split(y, avg, 2)
        dy = y - mu
        var = _dot_split(dy * dy, avg, 2)
        yn = dy * lax.rsqrt(var + LNX_EPS) * lg_ref[:, cs] + lb_ref[:, cs]
        o_ref[:, cs] = _bf((yn + bonus_ref[:, cs].astype(F32)) * gate[:, cs])


def _rwkv_readout(yf, yb, bonus, big, g2, lnx_g, lnx_b):
    n, width = yf.shape
    tm = _pick_tile(n, (512, 256))
    tok = pl.BlockSpec((tm, width), lambda i: (i, 0))
    full = lambda a: pl.BlockSpec(a.shape, lambda i: (0,) * a.ndim)
    return pl.pallas_call(
        _rwkv_readout_kernel,
        grid=(n // tm,),
        in_specs=[tok, tok, tok, pl.BlockSpec((tm, LANES), lambda i: (i, COL_GLOW // LANES)),
                  full(g2), full(lnx_g), full(lnx_b)],
        out_specs=tok,
        out_shape=jax.ShapeDtypeStruct((n, width), BF16),
        compiler_params=_cparams("parallel"),
        name="rwkv_readout",
    )(yf, yb, bonus, big, g2, lnx_g, lnx_b)


def _merge_kernel(oa_ref, ob_ref, oc_ref, gt_ref, x_ref, wb_ref, wo_ref, ng_ref, mod_ref, o_ref, *, nsub, d):
    y = None
    for br, ref in enumerate((oa_ref, ob_ref, oc_ref)):
        g = jax.nn.sigmoid(gt_ref[:, br * d:(br + 1) * d].astype(F32))
        t = g * _dot(ref[...], wb_ref[br])
        y = t if y is None else y + t
    o = _dot(_bf(y), wo_ref[...])
    o_ref[...] = x_ref[...] + _mod_rows(mod_ref, nsub, 2, d) * _rms(o, ng_ref[...])


def _merge(oa, ob, oc, big, x, wb, wo, ng, modblk):
    n, d = x.shape
    width = oa.shape[1]
    tm = _pick_tile(n, (512, 256))
    nsub = tm // MOD_ROWS
    kern = functools.partial(_merge_kernel, nsub=nsub, d=d)
    br = pl.BlockSpec((tm, width), lambda i: (i, 0))
    return pl.pallas_call(
        kern,
        grid=(n // tm,),
        in_specs=[br, br, br,
                  pl.BlockSpec((tm, 3 * d), lambda i: (i, COL_GATES // (3 * d))),
                  pl.BlockSpec((tm, d), lambda i: (i, 0)),
                  pl.BlockSpec(wb.shape, lambda i: (0, 0, 0)),
                  pl.BlockSpec(wo.shape, lambda i: (0, 0)),
                  pl.BlockSpec((1, d), lambda i: (0, 0)),
                  pl.BlockSpec((nsub, 1, 6 * d), lambda i: (i, 0, 0))],
        out_specs=pl.BlockSpec((tm, d), lambda i: (i, 0)),
        out_shape=jax.ShapeDtypeStruct((n, d), F32),
        compiler_params=_cparams("parallel"),
        name="merge_out_proj",
    )(oa, ob, oc, big, x, wb, wo, ng, modblk)


def _ffn_kernel(x_ref, g_ref, mod_ref, wg_ref, wu_ref, wd_ref, ng_ref, o_ref, h_ref, acc_ref, *, nsub, d):
    j = pl.program_id(1)

    @pl.when(j == 0)
    def _():
        y = _rms(x_ref[...], g_ref[...])
        h_ref[...] = _bf(y * (1.0 + _mod_rows(mod_ref, nsub, 4, d)) + _mod_rows(mod_ref, nsub, 3, d))
        acc_ref[...] = jnp.zeros_like(acc_ref)

    h = h_ref[...]
    g = _dot(h, wg_ref[...])
    u = _dot(h, wu_ref[...])
    acc_ref[...] += _dot(_bf(g * jax.nn.sigmoid(g) * u), wd_ref[...])

    @pl.when(j == pl.num_programs(1) - 1)
    def _():
        o_ref[...] = x_ref[...] + _mod_rows(mod_ref, nsub, 5, d) * _rms(acc_ref[...], ng_ref[...])


def _dense_ffn(x, gain_in, gain_out, modblk, w_gu, w_down):
    n, d = x.shape
    f = w_down.shape[0]
    tm = _pick_tile(n, (1024, 512, 256))
    tf = _pick_tile(f, (1408, 1024, 512, 256, 128))
    nf = f // tf
    nsub = tm // MOD_ROWS
    kern = functools.partial(_ffn_kernel, nsub=nsub, d=d)
    return pl.pallas_call(
        kern,
        grid=(n // tm, nf),
        in_specs=[pl.BlockSpec((tm, d), lambda i, j: (i, 0)),
                  pl.BlockSpec((1, d), lambda i, j: (0, 0)),
                  pl.BlockSpec((nsub, 1, 6 * d), lambda i, j: (i, 0, 0)),
                  pl.BlockSpec((d, tf), lambda i, j: (0, j)),
                  pl.BlockSpec((d, tf), lambda i, j: (0, j + nf)),
                  pl.BlockSpec((tf, d), lambda i, j: (j, 0)),
                  pl.BlockSpec((1, d), lambda i, j: (0, 0))],
        out_specs=pl.BlockSpec((tm, d), lambda i, j: (i, 0)),
        out_shape=jax.ShapeDtypeStruct((n, d), F32),
        scratch_shapes=[pltpu.VMEM((tm, d), BF16), pltpu.VMEM((tm, d), F32)],
        compiler_params=_cparams("parallel", "arbitrary"),
        name="dense_swiglu_ffn",
    )(x, gain_in, modblk, w_gu, w_gu, w_down, gain_out)


def _router_kernel(x_ref, g_ref, mod_ref, wr_ref, h_ref, comb_ref, rank_ref, combt_ref, rankt_ref, cnt_ref,
                   *, nsub, d):
    lane = lax.broadcasted_iota(jnp.int32, (MOD_ROWS, LANES), 1)
    r2 = lax.broadcasted_iota(jnp.int32, (MOD_ROWS, MOD_ROWS), 0)
    c2 = lax.broadcasted_iota(jnp.int32, (MOD_ROWS, MOD_ROWS), 1)
    tri = jnp.where(c2 < r2, 1.0, 0.0).astype(BF16)
    ninf = jnp.float32(-jnp.inf)
    running = jnp.zeros((1, LANES), F32)
    for s in range(nsub):
        rows = slice(s * MOD_ROWS, (s + 1) * MOD_ROWS)
        m = mod_ref[s]
        h = _rms(x_ref[rows, :], g_ref[...]) * (1.0 + m[:, 4 * d:5 * d]) + m[:, 3 * d:4 * d]
        hh = _bf(h)
        h_ref[rows, :] = hh
        hl = _bf(h - hh.astype(F32))
        both = _dot(hh, wr_ref[...])
        logits = both[:, :LANES] + both[:, LANES:] + _dot(hl, wr_ref[:, :LANES])
        logits = jnp.where(lane < N_EXPERTS, logits, ninf)
        m1 = jnp.max(logits, axis=-1, keepdims=True)
        i1 = jnp.min(jnp.where(logits == m1, lane, LANES), axis=-1, keepdims=True)
        rest = jnp.where(lane == i1, ninf, logits)
        m2 = jnp.max(rest, axis=-1, keepdims=True)
        i2 = jnp.min(jnp.where(rest == m2, lane, LANES), axis=-1, keepdims=True)
        e2 = jnp.exp(m2 - m1)
        w1 = 1.0 / (1.0 + e2)
        comb = jnp.where(lane == i1, w1, 0.0) + jnp.where(lane == i2, e2 * w1, 0.0)
        ind = jnp.where(comb > 0.0, 1.0, 0.0)
        rank = _dot(tri, _bf(ind)) + running
        running = running + jnp.sum(ind, axis=0, keepdims=True)
        comb_ref[rows, :] = comb
        rank_ref[rows, :] = rank
        combt_ref[:, rows] = comb.T[:N_EXPERTS, :]
        rankt_ref[:, rows] = rank.T[:N_EXPERTS, :]
    cnt_ref[0] = running


def _router(x, gain_in, modblk, w_router_pad, tm):
    n, d = x.shape
    nsub = tm // MOD_ROWS
    nt = n // tm
    kern = functools.partial(_router_kernel, nsub=nsub, d=d)
    tokm = pl.BlockSpec((tm, LANES), lambda i: (i, 0))
    expm = pl.BlockSpec((N_EXPERTS, tm), lambda i: (0, i))
    return pl.pallas_call(
        kern,
        grid=(nt,),
        in_specs=[pl.BlockSpec((tm, d), lambda i: (i, 0)),
                  pl.BlockSpec((1, d), lambda i: (0, 0)),
                  pl.BlockSpec((nsub, 1, 6 * d), lambda i: (i, 0, 0)),
                  pl.BlockSpec((d, 2 * LANES), lambda i: (0, 0))],
        out_specs=[pl.BlockSpec((tm, d), lambda i: (i, 0)), tokm, tokm, expm, expm,
                   pl.BlockSpec((1, 1, LANES), lambda i: (i, 0, 0))],
        out_shape=[jax.ShapeDtypeStruct((n, d), BF16),
                   jax.ShapeDtypeStruct((n, LANES), F32), jax.ShapeDtypeStruct((n, LANES), F32),
                   jax.ShapeDtypeStruct((N_EXPERTS, n), F32), jax.ShapeDtypeStruct((N_EXPERTS, n), F32),
                   jax.ShapeDtypeStruct((nt, 1, LANES), F32)],
        compiler_params=_cparams("parallel"),
        name="moe_router",
    )(x, gain_in, modblk, w_router_pad)


def _moe_kernel(cnt_ref, h_ref, comb_ref, rank_ref, combt_ref, rankt_ref, wg_ref, wu_ref, wd_ref, o_ref,
                xe_ref, y_ref, *, tm, rb):
    i = pl.program_id(0)
    e = pl.program_id(1)
    j = pl.program_id(2)
    nf = pl.num_programs(2)
    cnt = cnt_ref[i * N_EXPERTS + e]
    half, quarter = rb // 2, rb // 4
    nblk = cnt // rb
    tail0 = pl.multiple_of(nblk * rb, rb)
    rem = cnt - tail0

    def for_blocks(fn):
        lax.fori_loop(0, nblk, lambda b, carry: fn(pl.multiple_of(b * rb, rb), rb) or carry, 0)

        @pl.when(rem > half + quarter)
        def _():
            fn(tail0, rb)

        @pl.when(jnp.logical_and(rem > quarter, rem <= half + quarter))
        def _():
            fn(tail0, half)

        @pl.when(jnp.logical_or(jnp.logical_and(rem > 0, rem <= quarter),
                                jnp.logical_and(rem > half, rem <= half + quarter)))
        def _():
            fn(pl.multiple_of(tail0 + jnp.where(rem > half, half, 0), quarter), quarter)

    @pl.when(jnp.logical_and(e == 0, j == 0))
    def _():
        o_ref[...] = jnp.zeros_like(o_ref)

    @pl.when(j == 0)
    def _():
        key = jnp.where(combt_ref[pl.ds(e, 1), :] > 0.0, rankt_ref[pl.ds(e, 1), :], -1.0)

        def gather(r0, nr):
            want = (r0 + lax.broadcasted_iota(jnp.int32, (nr, tm), 0)).astype(F32)
            sel = jnp.where(key == want, 1.0, 0.0).astype(BF16)
            xe_ref[pl.ds(r0, nr), :] = _bf(_dot(sel, h_ref[...]))

        for_blocks(gather)

    def expert(r0, nr):
        rows = pl.ds(r0, nr)
        xb = xe_ref[rows, :]
        g = _dot(xb, wg_ref[0])
        u = _dot(xb, wu_ref[0])
        part = _dot(_bf(g * jax.nn.sigmoid(g) * u), wd_ref[0])

        @pl.when(j == 0)
        def _():
            y_ref[rows, :] = _bf(part)

        @pl.when(j > 0)
        def _():
            y_ref[rows, :] = _bf(y_ref[rows, :].astype(F32) + part)

    for_blocks(expert)

    @pl.when(j == nf - 1)
    def _():
        lane = lax.broadcasted_iota(jnp.int32, (tm, LANES), 1)
        rank_col = jnp.sum(jnp.where(lane == e, rank_ref[...], 0.0), axis=1, keepdims=True)
        w_col = jnp.sum(jnp.where(lane == e, comb_ref[...], 0.0), axis=1, keepdims=True)

        def scatter(r0, nr):
            want = (r0 + lax.broadcasted_iota(jnp.int32, (tm, nr), 1)).astype(F32)
            selw = _bf(jnp.where(rank_col == want, w_col, 0.0))
            o_ref[...] += _dot(selw, y_ref[pl.ds(r0, nr), :])

        for_blocks(scatter)


def _moe_ffn(hb, comb, rank, combt, rankt, counts, w_gu, w_down, tm):
    n, d = hb.shape
    ne, f, _ = w_down.shape
    tf = _pick_tile(f, (1792, 512, 256, 128))
    nf = f // tf
    rb = MOE_ROW_BLOCK
    kern = functools.partial(_moe_kernel, tm=tm, rb=rb)
    once = pl.Buffered(1)
    tokm = pl.BlockSpec((tm, LANES), lambda i, e, j, c: (i, 0), pipeline_mode=once)
    expm = pl.BlockSpec((N_EXPERTS, tm), lambda i, e, j, c: (0, i))
    grid_spec = pltpu.PrefetchScalarGridSpec(
        num_scalar_prefetch=1,
        grid=(n // tm, ne, nf),
        in_specs=[pl.BlockSpec((tm, d), lambda i, e, j, c: (i, 0), pipeline_mode=once), tokm, tokm, expm, expm,
                  pl.BlockSpec((1, d, tf), lambda i, e, j, c: (e, 0, j)),
                  pl.BlockSpec((1, d, tf), lambda i, e, j, c: (e, 0, j + nf)),
                  pl.BlockSpec((1, tf, d), lambda i, e, j, c: (e, j, 0))],
        out_specs=pl.BlockSpec((tm, d), lambda i, e, j, c: (i, 0), pipeline_mode=once),
        scratch_shapes=[pltpu.VMEM((tm, d), BF16), pltpu.VMEM((tm, d), BF16)])
    return pl.pallas_call(
        kern,
        grid_spec=grid_spec,
        out_shape=jax.ShapeDtypeStruct((n, d), F32),
        compiler_params=_cparams("parallel", "arbitrary", "arbitrary"),
        name="moe_swiglu_ffn",
    )(counts, hb, comb, rank, combt, rankt, w_gu, w_gu, w_down)


def _residual_kernel(x_ref, y_ref, ng_ref, mod_ref, o_ref, *, nsub, d):
    o_ref[...] = x_ref[...] + _mod_rows(mod_ref, nsub, 5, d) * _rms(y_ref[...].astype(F32), ng_ref[...])


def _gated_residual(x, y, gain_out, modblk):
    n, d = x.shape
    tm = _pick_tile(n, (512, 256))
    nsub = tm // MOD_ROWS
    tok = pl.BlockSpec((tm, d), lambda i: (i, 0))
    return pl.pallas_call(
        functools.partial(_residual_kernel, nsub=nsub, d=d),
        grid=(n // tm,),
        in_specs=[tok, tok, pl.BlockSpec((1, d), lambda i: (0, 0)),
                  pl.BlockSpec((nsub, 1, 6 * d), lambda i: (i, 0, 0))],
        out_specs=tok,
        out_shape=jax.ShapeDtypeStruct((n, d), F32),
        compiler_params=_cparams("parallel"),
        name="moe_gated_residual",
    )(x, y, gain_out, modblk)


def _rope_tables(t_len, c_len):
    pairs = HEAD_DIM // 4
    rows = t_len // GRID_W
    row = jnp.repeat(jnp.arange(rows, dtype=F32), GRID_W)
    col = jnp.tile(jnp.arange(GRID_W, dtype=F32), rows)
    freqs = ROPE_BASE ** (-jnp.arange(pairs, dtype=F32) / pairs)
    ar = row[:, None] * freqs
    ac = col[:, None] * freqs
    cos = jnp.concatenate([jnp.cos(ar), jnp.cos(ar), jnp.cos(ac), jnp.cos(ac)], axis=1)
    sin = jnp.concatenate([-jnp.sin(ar), jnp.sin(ar), -jnp.sin(ac), jnp.sin(ac)], axis=1)
    cos = jnp.concatenate([jnp.ones((c_len, HEAD_DIM), F32), cos], axis=0)
    sin = jnp.concatenate([jnp.zeros((c_len, HEAD_DIM), F32), sin], axis=0)
    return jnp.tile(cos, (1, 2)), jnp.tile(sin, (1, 2))


def _block_diag2(w):
    z = jnp.zeros_like(w[0])
    return jnp.concatenate([jnp.concatenate([w[0], z], axis=1), jnp.concatenate([z, w[1]], axis=1)], axis=0)


def kernel(x, c, ctx, c_ctx, w_mod, b_mod, norm_gain, w_in, qk_gain, rwkv_conv, decay_w0, decay_w2, iclr_a0, iclr_a2, key_k, bonus_rk, gate_g2, lnx_gain, lnx_bias, cmlp_ln_gain, cmlp_ln_bias, cmlp_ws, cmlp_bs, w_branch, w_out, ffn_w_gu, ffn_w_down, moe_router, moe_w_gu, moe_w_down):
    batch, t_len, d = x.shape
    c_len = ctx.shape[1]
    depth = w_mod.shape[0]
    s_tot = c_len + t_len
    n = batch * s_tot
    assert c_len % MOD_ROWS == 0 and t_len % MOD_ROWS == 0 and d % LANES == 0
    width = bonus_rk.shape[1] * bonus_rk.shape[2]
    nheads = width // HEAD_DIM

    xs = jnp.concatenate([ctx, x], axis=1).reshape(n, d)

    mod_rows = 8 * ((batch + 1 + 7) // 8)
    cvec = jnp.zeros((mod_rows, d), F32).at[0].set(c_ctx).at[1:batch + 1].set(c)
    mods = _modulation(cvec, w_mod, b_mod)
    mod_ctx = jnp.broadcast_to(mods[:, 0:1, None, :], (depth, batch, c_len // MOD_ROWS, 6 * d))
    mod_lat = jnp.broadcast_to(mods[:, 1:batch + 1, None, :], (depth, batch, t_len // MOD_ROWS, 6 * d))
    modblk_all = jnp.concatenate([mod_ctx, mod_lat], axis=2).reshape(depth, n // MOD_ROWS, 1, 6 * d)

    order = np.array(Q_HEAD_ORDER)
    nl = depth
    wq = w_in[:, :, 2048:2560].reshape(nl, d, nheads, HEAD_DIM)[:, :, order].reshape(nl, d, width)
    w_in_p = jnp.concatenate([
        w_in[:, :, 256:1792], wq, w_in[:, :, 2688:3712], w_in[:, :, 3712:6784],
        w_in[:, :, 0:128], w_in[:, :, 128:256], w_in[:, :, 1792:1920], w_in[:, :, 1920:2048],
        w_in[:, :, 2560:2688], jnp.zeros((nl, d, IN_PAD - 6784), F32)], axis=2).astype(BF16)
    wb = w_branch.astype(BF16)
    wb0 = wb[:, 0].reshape(nl, nheads, HEAD_DIM, d)[:, order].reshape(nl, width, d)
    wb = jnp.concatenate([wb0[:, None], wb[:, 1:]], axis=1)
    wo = w_out.astype(BF16)
    cos, sin = _rope_tables(t_len, c_len)
    qg = jnp.tile(qk_gain[:, 0], (1, 2))[:, None, :]
    kg = jnp.tile(qk_gain[:, 1], (1, 2))[:, None, :]
    ws_b = cmlp_ws.astype(BF16)
    bs_b = jnp.broadcast_to(cmlp_bs[..., None], cmlp_bs.shape + (CMLP_CHUNK,))
    w2s = jnp.stack([_block_diag2(decay_w2[l]) for l in range(nl)]).astype(BF16)
    a2s = jnp.stack([_block_diag2(iclr_a2[l]) for l in range(nl)]).astype(BF16)
    w0 = decay_w0.reshape(nl, 1, 2 * width)
    a0 = iclr_a0.reshape(nl, 1, 2 * width)
    rk = bonus_rk.reshape(nl, 1, width)
    g2 = gate_g2.astype(BF16)
    ffn_gu = ffn_w_gu.astype(BF16)
    ffn_dn = ffn_w_down.astype(BF16)
    moe_gu = moe_w_gu.astype(BF16)
    moe_dn = moe_w_down.astype(BF16)
    router_pad = jnp.pad(moe_router, ((0, 0), (0, 0), (0, LANES - moe_router.shape[2])))
    router_hi = router_pad.astype(BF16)
    router_pad = jnp.concatenate([router_hi, (router_pad - router_hi.astype(F32)).astype(BF16)], axis=2)

    for l in range(depth):
        modblk = modblk_all[l]
        ng = norm_gain[l]
        big = _norm_mod_matmul(xs, ng[0:1], modblk, w_in_p[l], 0, 1)
        qh, kbd, vt = _qk_prep(big, qg[l], kg[l], cos, sin, s_tot)
        oa = _attention(qh, kbd, vt, batch, s_tot, c_len)
        prep = _rwkv_prep(big, rwkv_conv[l], w0[l], w2s[l], a0[l], a2s[l],
                          key_k[l, 0:1], key_k[l, 1:2], rk[l], s_tot, c_len)
        yf, yb = _rwkv_scan(prep, batch, s_tot, c_len)
        ob = _rwkv_readout(yf, yb, prep[1], big, g2[l], lnx_gain[l][None], lnx_bias[l][None])
        oc = _chunk_mlp(big, cmlp_ln_gain[l][None], cmlp_ln_bias[l][None], ws_b[l], bs_b[l])
        xs = _merge(oa, ob, oc, big, xs, wb[l], wo[l], ng[1:2], modblk)
        if l % 2 == 0:
            xs = _dense_ffn(xs, ng[2:3], ng[3:4], modblk, ffn_gu[l // 2], ffn_dn[l // 2])
        else:
            tmoe = _pick_tile(n, (2048, 1024, 512, 256))
            hb, comb, rank, combt, rankt, cnt = _router(xs, ng[2:3], modblk, router_pad[l // 2], tmoe)
            counts = cnt[:, 0, :N_EXPERTS].astype(jnp.int32).reshape(-1)
            y = _moe_ffn(hb, comb, rank, combt, rankt, counts, moe_gu[l // 2], moe_dn[l // 2], tmoe)
            xs = _gated_residual(xs, y, ng[3:4], modblk)
    return xs.reshape(batch, s_tot, d)[:, c_len:, :]
```

```python
import functools

import jax
import jax.numpy as jnp
import numpy as np
from jax import lax
from jax.experimental import pallas as pl
from jax.experimental.pallas import tpu as pltpu

F32 = jnp.float32
BF16 = jnp.bfloat16
HIGHEST = lax.Precision.HIGHEST

EPS = 1e-6
LNX_EPS = 64e-5
HEAD_DIM = 64
ROPE_BASE = 10000.0
GRID_W = 64
LANES = 128
MXU_TILE = 256
MOD_ROWS = 256
SCAN_CHUNK = 64
CMLP_CHUNK = 128
ATT_TK = 256
ATT_VROWS = HEAD_DIM + 16
ATT_GROUP = 1
N_EXPERTS = 8
MOE_ROW_BLOCK = 512
VMEM_LIMIT = 56 * 1024 * 1024

COL_RKV, COL_Q, COL_UV, COL_GATES = 0, 1536, 2048, 3072
COL_K, COL_V, COL_WLOW, COL_ALOW, COL_GLOW = 6144, 6272, 6400, 6528, 6656
IN_PAD = 7168
Q_HEAD_ORDER = (0, 4, 1, 5, 2, 6, 3, 7)


def _cparams(*sem):
    return pltpu.CompilerParams(dimension_semantics=sem, vmem_limit_bytes=VMEM_LIMIT)


def _dot(a, b):
    return jnp.dot(a, b, preferred_element_type=F32)


def _dot_nt(a, b):
    return lax.dot_general(a, b, (((1,), (1,)), ((), ())), preferred_element_type=F32)


def _bf(x):
    return x.astype(BF16)


def _dot_split(a, b_exact, terms):
    acc = None
    rem = a
    for _ in range(terms):
        piece = _bf(rem)
        rem = rem - piece.astype(F32)
        part = _dot(piece, b_exact)
        acc = part if acc is None else acc + part
    return acc


def _dot_split_left(a_exact, b, terms):
    acc = None
    rem = b
    for _ in range(terms):
        piece = _bf(rem)
        rem = rem - piece.astype(F32)
        part = _dot(a_exact, piece)
        acc = part if acc is None else acc + part
    return acc


def _group_matrix(scale):
    r = lax.broadcasted_iota(jnp.int32, (LANES, LANES), 0) // HEAD_DIM
    c = lax.broadcasted_iota(jnp.int32, (LANES, LANES), 1) // HEAD_DIM
    return jnp.where(r == c, scale, 0.0).astype(BF16)


def _pick_tile(n, candidates):
    for t in candidates:
        if n % t == 0:
            return t
    raise ValueError(f"no tile in {candidates} divides {n}")


def _mod_rows(mod_ref, nsub, idx, d):
    parts = [jnp.broadcast_to(mod_ref[s][:, idx * d:(idx + 1) * d], (MOD_ROWS, d)) for s in range(nsub)]
    return parts[0] if nsub == 1 else jnp.concatenate(parts, axis=0)


def _rms(x, g):
    return x * lax.rsqrt(jnp.mean(x * x, axis=-1, keepdims=True) + EPS) * g


def _mod_kernel(c_ref, w_ref, b_ref, o_ref):
    cv = c_ref[...]
    s = cv * jax.nn.sigmoid(cv)
    o_ref[0] = jnp.dot(s, w_ref[0], precision=HIGHEST, preferred_element_type=F32) + b_ref[0]


def _modulation(cvec, w_mod, b_mod):
    nl, d, d6 = w_mod.shape
    rows = cvec.shape[0]
    tn = 1024
    return pl.pallas_call(
        _mod_kernel,
        grid=(nl, d6 // tn),
        in_specs=[pl.BlockSpec((rows, d), lambda l, j: (0, 0)),
                  pl.BlockSpec((1, d, tn), lambda l, j: (l, 0, j)),
                  pl.BlockSpec((1, 1, tn), lambda l, j: (l, 0, j))],
        out_specs=pl.BlockSpec((1, rows, tn), lambda l, j: (l, 0, j)),
        out_shape=jax.ShapeDtypeStruct((nl, rows, d6), F32),
        compiler_params=_cparams("parallel", "parallel"),
        name="modulation",
    )(cvec, w_mod, b_mod.reshape(nl, 1, d6))


def _nmm_kernel(x_ref, g_ref, mod_ref, w_ref, o_ref, h_ref, *, nsub, d, shift_idx, scale_idx):
    @pl.when(pl.program_id(1) == 0)
    def _():
        y = _rms(x_ref[...], g_ref[...])
        sc = _mod_rows(mod_ref, nsub, scale_idx, d)
        sh = _mod_rows(mod_ref, nsub, shift_idx, d)
        h_ref[...] = _bf(y * (1.0 + sc) + sh)

    o_ref[...] = _bf(_dot(h_ref[...], w_ref[...]))


def _norm_mod_matmul(x, gain, modblk, w, shift_idx, scale_idx):
    n, d = x.shape
    nout = w.shape[1]
    tm = _pick_tile(n, (1024, 512, 256))
    tn = _pick_tile(nout, (1792, 1024))
    nsub = tm // MOD_ROWS
    kern = functools.partial(_nmm_kernel, nsub=nsub, d=d, shift_idx=shift_idx, scale_idx=scale_idx)
    return pl.pallas_call(
        kern,
        grid=(n // tm, nout // tn),
        in_specs=[pl.BlockSpec((tm, d), lambda i, j: (i, 0)),
                  pl.BlockSpec((1, d), lambda i, j: (0, 0)),
                  pl.BlockSpec((nsub, 1, 6 * d), lambda i, j: (i, 0, 0)),
                  pl.BlockSpec((d, tn), lambda i, j: (0, j))],
        out_specs=pl.BlockSpec((tm, tn), lambda i, j: (i, j)),
        out_shape=jax.ShapeDtypeStruct((n, nout), BF16),
        scratch_shapes=[pltpu.VMEM((tm, d), BF16)],
        compiler_params=_cparams("parallel", "arbitrary"),
        name="norm_mod_in_proj",
    )(x, gain, modblk, w)


def _qkprep_kernel(q_ref, k_ref, v_ref, qg_ref, kg_ref, cos_ref, sin_ref, qo_ref, ko_ref, vo_ref):
    cos = cos_ref[...]
    sin = sin_ref[...]
    avg = _group_matrix(1.0 / HEAD_DIM)
    lane = lax.broadcasted_iota(jnp.int32, cos.shape, 1)
    first = (lane % 32) < 16
    left = lane < HEAD_DIM

    def norm_rope(x, g):
        ms = _dot_split(x * x, avg, 2)
        xn = x * lax.rsqrt(ms + EPS) * g
        partner = jnp.where(first, pltpu.roll(xn, LANES - 16, 1), pltpu.roll(xn, 16, 1))
        return xn * cos + partner * sin

    qscale = (HEAD_DIM ** -0.5) * float(np.log2(np.e))
    for j in range(q_ref.shape[1] // LANES):
        q = q_ref[:, j * LANES:(j + 1) * LANES].astype(F32)
        qo_ref[:, j * LANES:(j + 1) * LANES] = _bf(norm_rope(q, qg_ref[...]) * qscale)
    k = norm_rope(k_ref[...].astype(F32), kg_ref[...])
    zero = jnp.zeros_like(k)
    k0 = _bf(jnp.where(left, k, zero))
    k1 = _bf(jnp.where(left, zero, k))
    v = v_ref[...].astype(F32)
    ones = jnp.ones((ATT_VROWS - HEAD_DIM, ATT_TK), BF16)
    for c in range(vo_ref.shape[0]):
        rows = slice(c * ATT_TK, (c + 1) * ATT_TK)
        ko_ref[c, :ATT_TK, :] = k0[rows]
        ko_ref[c, ATT_TK:, :] = k1[rows]
        vt = _bf(v[rows, :].T)
        for t in range(2):
            vo_ref[c, t, :HEAD_DIM, :] = vt[t * HEAD_DIM:(t + 1) * HEAD_DIM]
            vo_ref[c, t, HEAD_DIM:, :] = ones


def _qk_prep(big, qg, kg, cos, sin, s_tot):
    n = big.shape[0]
    tm = MOD_ROWS
    npos = s_tot // tm
    qw = 512
    vchunks = tm // ATT_TK
    return pl.pallas_call(
        _qkprep_kernel,
        grid=(n // tm,),
        in_specs=[pl.BlockSpec((tm, qw), lambda i: (i, COL_Q // qw)),
                  pl.BlockSpec((tm, LANES), lambda i: (i, COL_K // LANES)),
                  pl.BlockSpec((tm, LANES), lambda i: (i, COL_V // LANES)),
                  pl.BlockSpec((1, LANES), lambda i: (0, 0)),
                  pl.BlockSpec((1, LANES), lambda i: (0, 0)),
                  pl.BlockSpec((tm, LANES), lambda i: (i % npos, 0)),
                  pl.BlockSpec((tm, LANES), lambda i: (i % npos, 0))],
        out_specs=[pl.BlockSpec((tm, qw), lambda i: (i, 0)),
                   pl.BlockSpec((vchunks, 2 * ATT_TK, LANES), lambda i: (i, 0, 0)),
                   pl.BlockSpec((vchunks, 2, ATT_VROWS, ATT_TK), lambda i: (i, 0, 0, 0))],
        out_shape=[jax.ShapeDtypeStruct((n, qw), BF16),
                   jax.ShapeDtypeStruct((n // ATT_TK, 2 * ATT_TK, LANES), BF16),
                   jax.ShapeDtypeStruct((n // ATT_TK, 2, ATT_VROWS, ATT_TK), BF16)],
        compiler_params=_cparams("parallel"),
        name="qk_norm_rope",
    )(big, big, big, qg, kg, cos, sin)


def _attn_kernel(q_ref, k_ref, vt_ref, o_ref, acc_ref, sa_ref, sb_ref, *, tq, tk, n_ctx_q, n_ctx_kv, n_kv):
    i = pl.program_id(1)
    nkv = jnp.where(i < n_ctx_q, n_ctx_kv, n_kv)
    hd = HEAD_DIM
    npair = q_ref.shape[1] // LANES
    nh = 2 * npair
    qs = [q_ref[:, j * LANES:(j + 1) * LANES] for j in range(npair)]
    vr = ATT_VROWS
    acc_ref[...] = jnp.zeros_like(acc_ref)

    def scores_to(dst_ref, kb, h):
        s = _dot_nt(kb[(h % 2) * tk:(h % 2 + 1) * tk], qs[h // 2])
        dst_ref[h, :tk, :] = s
        dst_ref[h, tk:, :] = jnp.broadcast_to(jnp.max(s, axis=0, keepdims=True), (8, tq))

    def consume(src_ref, c, h, mh):
        n = jnp.maximum(mh, src_ref[h, tk:tk + 1, :])
        p = _bf(jnp.exp2(src_ref[h, :tk, :] - n))
        rows = slice(h * vr, (h + 1) * vr)
        acc_ref[rows, :] = acc_ref[rows, :] * jnp.exp2(mh - n) + _dot(vt_ref[c, h % 2], p)
        return n

    def step(src_ref, dst_ref, c, m, prefetch):
        kb = k_ref[c + 1] if prefetch else None
        new_m = []
        for h in range(nh):
            if prefetch:
                scores_to(dst_ref, kb, h)
            new_m.append(consume(src_ref, c, h, m[h]))
        return tuple(new_m)

    def body(u, m):
        c = 2 * u
        m = step(sa_ref, sb_ref, c, m, True)
        return step(sb_ref, sa_ref, c + 1, m, True)

    kb0 = k_ref[0]
    for h in range(nh):
        scores_to(sa_ref, kb0, h)
    m = lax.fori_loop(0, (nkv - 1) // 2, body, (jnp.full((1, tq), -1e30, F32),) * nh)
    step(sa_ref, sb_ref, nkv - 1, m, False)
    for j in range(npair):
        o = [acc_ref[h * vr:h * vr + hd, :] * (1.0 / acc_ref[h * vr + hd:h * vr + hd + 1, :]) for h in (2 * j, 2 * j + 1)]
        o_ref[:, j * LANES:(j + 1) * LANES] = _bf(jnp.concatenate(o, axis=0).T)


def _attention(qh, kbd, vt, batch, s_tot, c_len):
    n, qw = qh.shape
    tq = 256
    tk = ATT_TK
    nq = s_tot // tq
    assert (c_len // tk) % 2 == 1 and (s_tot // tk) % 2 == 1, "the key-chunk loop is unrolled by two plus a tail"
    kern = functools.partial(_attn_kernel, tq=tq, tk=tk, n_ctx_q=c_len // tq,
                             n_ctx_kv=c_len // tk, n_kv=s_tot // tk)
    return pl.pallas_call(
        kern,
        grid=(batch, nq),
        in_specs=[pl.BlockSpec((tq, qw), lambda b, i: (b * nq + i, 0)),
                  pl.BlockSpec((s_tot // tk, 2 * tk, LANES), lambda b, i: (b, 0, 0)),
                  pl.BlockSpec((s_tot // tk, 2, ATT_VROWS, tk), lambda b, i: (b, 0, 0, 0))],
        out_specs=pl.BlockSpec((tq, qw), lambda b, i: (b * nq + i, 0)),
        out_shape=jax.ShapeDtypeStruct((n, qw), BF16),
        scratch_shapes=[pltpu.VMEM((2 * (qw // LANES) * ATT_VROWS, tq), F32),
                        pltpu.VMEM((2 * (qw // LANES), tk + 8, tq), F32),
                        pltpu.VMEM((2 * (qw // LANES), tk + 8, tq), F32)],
        compiler_params=_cparams("parallel", "parallel"),
        name="gqa_attention",
    )(qh, kbd, vt)


def _cmlp_kernel(uv_ref, lng_ref, lnb_ref, ws_ref, bs_ref, o_ref, *, nchunk, width):
    x = uv_ref[...].astype(F32)
    g = 0.5 * x * (1.0 + jnp.tanh(0.7978845608028654 * (x + 0.044715 * (x * x * x))))
    u = g[:, :width]
    v = g[:, width:]
    mu = jnp.mean(v, axis=-1, keepdims=True)
    dv = v - mu
    var = jnp.mean(dv * dv, axis=-1, keepdims=True)
    vn = _bf(dv * lax.rsqrt(var + EPS) * lng_ref[...] + lnb_ref[...])
    ngroups = width // CMLP_CHUNK
    for c in range(nchunk):
        r0 = c * CMLP_CHUNK
        for gi in range(ngroups):
            c0 = gi * CMLP_CHUNK
            s = _dot(ws_ref[gi], vn[r0:r0 + CMLP_CHUNK, c0:c0 + CMLP_CHUNK]) + bs_ref[gi]
            o_ref[r0:r0 + CMLP_CHUNK, c0:c0 + CMLP_CHUNK] = _bf(u[r0:r0 + CMLP_CHUNK, c0:c0 + CMLP_CHUNK] * s)


def _chunk_mlp(big, ln_g, ln_b, ws, bs_b):
    n = big.shape[0]
    width = ln_g.shape[1]
    tr = _pick_tile(n, (512, 256, 128))
    kern = functools.partial(_cmlp_kernel, nchunk=tr // CMLP_CHUNK, width=width)
    ng = ws.shape[0]
    return pl.pallas_call(
        kern,
        grid=(n // tr,),
        in_specs=[pl.BlockSpec((tr, 2 * width), lambda i: (i, COL_UV // (2 * width))),
                  pl.BlockSpec((1, width), lambda i: (0, 0)),
                  pl.BlockSpec((1, width), lambda i: (0, 0)),
                  pl.BlockSpec((ng, CMLP_CHUNK, CMLP_CHUNK), lambda i: (0, 0, 0)),
                  pl.BlockSpec((ng, CMLP_CHUNK, CMLP_CHUNK), lambda i: (0, 0, 0))],
        out_specs=pl.BlockSpec((tr, width), lambda i: (i, 0)),
        out_shape=jax.ShapeDtypeStruct((n, width), BF16),
        compiler_params=_cparams("parallel"),
        name="chunk_gmlp",
    )(big, ln_g, ln_b, ws, bs_b)


def _rwkv_prep_kernel(x_ref, xp_ref, xn_ref, lo_ref, conv_ref, w0_ref, w2_ref, a0_ref, a2_ref,
                      kk0_ref, kk1_ref, rk_ref,
                      v_o, bonus_o, at_f, rt_f, bt_f, kt_f, bb_f, kb_f, pl_f,
                      at_b, rt_b, bt_b, kt_b, bb_b, kb_b, pl_b, *, tm, width, blocks_per_seq, ctx_blocks):
    i = pl.program_id(0)
    j = i % blocks_per_seq
    is_first = jnp.logical_or(j == 0, j == ctx_blocks)
    is_last = jnp.logical_or(j == ctx_blocks - 1, j == blocks_per_seq - 1)
    row = lax.broadcasted_iota(jnp.int32, (tm, width), 0)
    gsum = _group_matrix(1.0)
    halo = xp_ref.shape[0]

    def conv(c):
        cs = slice(c * width, (c + 1) * width)
        x = x_ref[:, cs].astype(F32)
        prev_row = jnp.where(is_first, 0.0, xp_ref[halo - 1:halo, cs].astype(F32))
        next_row = jnp.where(is_last, 0.0, xn_ref[0:1, cs].astype(F32))
        xprev = jnp.where(row == 0, prev_row, pltpu.roll(x, 1, 0))
        xnext = jnp.where(row == tm - 1, next_row, pltpu.roll(x, tm - 1, 0))
        return xprev * conv_ref[0:1, cs] + x * conv_ref[1:2, cs] + xnext * conv_ref[2:3, cs]

    r = conv(0)
    k = conv(1)
    v = conv(2)
    v_o[...] = _bf(v)

    def group_sum(x):
        parts = [_dot_split(x[:, c * LANES:(c + 1) * LANES], gsum, 2) for c in range(width // LANES)]
        return jnp.concatenate(parts, axis=1)

    kk = k * kk0_ref[...]
    kk = kk * lax.rsqrt(group_sum(kk * kk) + 1e-12)
    bonus_o[...] = _bf(group_sum(r * k * rk_ref[...]) * v)

    lo = lo_ref[...].astype(F32)
    wd = w0_ref[...] + _dot(_bf(jnp.tanh(lo[:, :LANES])), w2_ref[...])
    ad = jax.nn.sigmoid(a0_ref[...] + _dot(_bf(lo[:, LANES:]), a2_ref[...]))
    lw = -float(np.exp(-0.5) * np.log2(np.e)) * jax.nn.sigmoid(wd)

    r2 = lax.broadcasted_iota(jnp.int32, (tm, tm), 0)
    c2 = lax.broadcasted_iota(jnp.int32, (tm, tm), 1)
    same = (r2 // SCAN_CHUNK) == (c2 // SCAN_CHUNK)
    tri_pre = jnp.where(jnp.logical_and(same, c2 <= r2), 1.0, 0.0).astype(BF16)
    tri_suf = jnp.where(jnp.logical_and(same, c2 >= r2), 1.0, 0.0).astype(BF16)
    nchunk = tm // SCAN_CHUNK

    outs = ((at_f, rt_f, bt_f, kt_f, bb_f, kb_f, pl_f), (at_b, rt_b, bt_b, kt_b, bb_b, kb_b, pl_b))
    for d in range(2):
        ds_ = slice(d * width, (d + 1) * width)
        lwd = lw[:, ds_]
        pre = _dot_split_left(tri_pre, lwd, 3)
        suf = _dot_split_left(tri_suf, lwd, 3)
        cin, rem = (pre, suf - lwd) if d == 0 else (suf, pre - lwd)
        cex = cin - lwd
        a_d = ad[:, ds_]
        b = kk * a_d
        kd = k * (1.0 + (a_d - 1.0) * kk1_ref[...])
        at_o, rt_o, bt_o, kt_o, bb_o, kb_o, pl_o = outs[d]
        at_o[...] = _bf(-kk * jnp.exp2(cex))
        rt_o[...] = _bf(r * jnp.exp2(cin))
        pinv = jnp.exp2(-cin)
        bt_o[...] = _bf(b * pinv)
        kt_o[...] = _bf(kd * pinv)
        pend = jnp.exp2(rem)
        bb_o[...] = _bf(b * pend)
        kb_o[...] = _bf(kd * pend)
        for c in range(nchunk):
            last = (c + 1) * SCAN_CHUNK - 1
            pl_o[c] = jnp.exp2(pre[last:last + 1, :])


def _rwkv_prep(big, conv_w, w0, w2s, a0, a2s, kk0, kk1, rk, s_tot, c_len):
    n = big.shape[0]
    width = rk.shape[1]
    tm = MOD_ROWS
    halo = 16
    hb = tm // halo
    nhalo = n // halo
    nchunk = tm // SCAN_CHUNK
    kern = functools.partial(_rwkv_prep_kernel, tm=tm, width=width, blocks_per_seq=s_tot // tm,
                             ctx_blocks=c_len // tm)
    tok = pl.BlockSpec((tm, width), lambda i: (i, 0))
    plspec = pl.BlockSpec((nchunk, 1, width), lambda i: (i, 0, 0))
    tok_shape = jax.ShapeDtypeStruct((n, width), BF16)
    pl_shape = jax.ShapeDtypeStruct((n // SCAN_CHUNK, 1, width), F32)
    full = lambda a: pl.BlockSpec(a.shape, lambda i: (0,) * a.ndim)
    return pl.pallas_call(
        kern,
        grid=(n // tm,),
        in_specs=[pl.BlockSpec((tm, 3 * width), lambda i: (i, 0)),
                  pl.BlockSpec((halo, 3 * width), lambda i: (jnp.maximum(i * hb - 1, 0), 0)),
                  pl.BlockSpec((halo, 3 * width), lambda i: (jnp.minimum((i + 1) * hb, nhalo - 1), 0)),
                  pl.BlockSpec((tm, 2 * LANES), lambda i: (i, COL_WLOW // (2 * LANES))),
                  full(conv_w), full(w0), full(w2s), full(a0), full(a2s), full(kk0), full(kk1), full(rk)],
        out_specs=[tok, tok] + [tok] * 6 + [plspec] + [tok] * 6 + [plspec],
        out_shape=[tok_shape, tok_shape] + [tok_shape] * 6 + [pl_shape] + [tok_shape] * 6 + [pl_shape],
        compiler_params=_cparams("parallel"),
        name="rwkv_prepare",
    )(big, big, big, big, conv_w, w0, w2s, a0, a2s, kk0, kk1, rk)


def _scan_chunks(chains):
    L = SCAN_CHUNK
    lane = lax.broadcasted_iota(jnp.int32, (L, LANES), 1)
    m0 = _bf(jnp.where(lane < HEAD_DIM, 1.0, 0.0))
    m1 = _bf(jnp.where(lane < HEAD_DIM, 0.0, 1.0))

    def stack(x):
        blocks = [x[:, c:c + LANES] for c in range(0, x.shape[1], LANES)]
        top = [b * m0 for b in blocks]
        bot = [b * m1 for b in blocks]
        if len(blocks) == 1:
            return jnp.concatenate([top[0], bot[0]], axis=0)
        return jnp.concatenate([jnp.concatenate(top, axis=1), jnp.concatenate(bot, axis=1)], axis=0)

    trow = lax.broadcasted_iota(jnp.int32, (L, LANES), 0)
    tcol = lax.broadcasted_iota(jnp.int32, (L, LANES), 1) % L
    masks = {True: (tcol < trow, tcol <= trow), False: (tcol > trow, tcol >= trow)}
    eye = lax.broadcasted_iota(jnp.int32, (LANES, LANES), 0) == lax.broadcasted_iota(jnp.int32, (LANES, LANES), 1)
    fwd = [ch[9] for ch in chains]
    nc = range(len(chains))

    v_s = [stack(ch[6]) for ch in chains]
    big1 = [_dot_nt(jnp.concatenate([chains[i][0], chains[i][1]], axis=0),
                    jnp.concatenate([stack(chains[i][2]), stack(chains[i][3])], axis=0)) for i in nc]
    pm = [jnp.where(masks[fwd[i]][0], big1[i][:L, :LANES], 0.0) for i in nc]
    mak = [_bf(jnp.where(masks[fwd[i]][0], big1[i][:L, LANES:], 0.0)) for i in nc]
    lhs_top = [_bf(jnp.where(jnp.concatenate([masks[fwd[i]][1]] * 2, axis=1), big1[i][L:], 0.0)) for i in nc]
    mv = [_dot(mak[i], v_s[i]) for i in nc]
    px = [jnp.concatenate([chains[i][0].astype(F32), mv[i]], axis=1) for i in nc]
    steps = int(np.log2(L))
    for it in range(steps):
        if it < steps - 1:
            res = [_dot(_bf(pm[i]), stack(_bf(jnp.concatenate([pm[i], px[i]], axis=1)))) for i in nc]
            px = [px[i] + res[i][:, LANES:] for i in nc]
            pm = [res[i][:, :LANES] for i in nc]
        else:
            res = [_dot(_bf(pm[i]), stack(_bf(px[i]))) for i in nc]
            px = [px[i] + res[i] for i in nc]
    rhs2 = [jnp.concatenate([stack(_bf(px[i])), jnp.concatenate([jnp.zeros_like(v_s[i]), v_s[i]], axis=1)], axis=0)
            for i in nc]
    lhs_bot = [_bf(jnp.concatenate([stack(chains[i][4]), stack(chains[i][5])], axis=0).astype(F32).T)
               for i in nc]
    res2 = [_dot(jnp.concatenate([lhs_top[i], lhs_bot[i]], axis=0), rhs2[i]) for i in nc]
    lhs3 = [_bf(jnp.concatenate(
        [chains[i][1].astype(F32) + res2[i][:L, :LANES],
         res2[i][L:, :LANES] + jnp.where(eye, jnp.broadcast_to(chains[i][7], (LANES, LANES)), 0.0)], axis=0))
        for i in nc]
    res3 = [_dot(lhs3[i], _bf(chains[i][8])) for i in nc]
    return [(res3[i][:L] + res2[i][:L, LANES:], res3[i][L:] + res2[i][L:, LANES:]) for i in nc]


def _rwkv_scan_kernel(v_f, at_f, rt_f, bt_f, kt_f, bb_f, kb_f, pl_f,
                      v_b, at_b, rt_b, bt_b, kt_b, bb_b, kb_b, pl_b,
                      yf_ref, yb_ref, z_ref, *, npairs, nb):
    @pl.when(pl.program_id(1) == 0)
    def _():
        z_ref[...] = jnp.zeros_like(z_ref)

    dirs = ((v_f, at_f, rt_f, bt_f, kt_f, bb_f, kb_f, pl_f, yf_ref, True),
            (v_b, at_b, rt_b, bt_b, kt_b, bb_b, kb_b, pl_b, yb_ref, False))
    chains, dest = [], []
    for s in range(nb):
        for d, (v, at, rt, bt, kt, bb, kb, plr, y_ref, fwd) in enumerate(dirs):
            for p in range(npairs):
                cs = slice(p * LANES, (p + 1) * LANES)
                chains.append((at[s, :, cs], rt[s, :, cs], bt[s, :, cs], kt[s, :, cs], bb[s, :, cs], kb[s, :, cs],
                               v[s, :, cs], plr[s, 0][:, cs], z_ref[s, d, p], fwd))
                dest.append((y_ref, s, cs, d, p))
    for (y_ref, s, cs, d, p), (y, znew) in zip(dest, _scan_chunks(chains)):
        y_ref[s, :, cs] = y
        z_ref[s, d, p] = znew


def _rwkv_scan(prep, batch, s_tot, c_len):
    (v, _bonus, at_f, rt_f, bt_f, kt_f, bb_f, kb_f, pl_f, at_b, rt_b, bt_b, kt_b, bb_b, kb_b, pl_b) = prep
    n, width = v.shape
    L = SCAN_CHUNK
    nch = s_tot // L
    ncc = c_len // L
    npairs = width // LANES
    nb = _pick_tile(batch, (4, 2, 1))

    def fchunk(c):
        return c

    def bchunk(c):
        return jnp.where(c < ncc, ncc - 1 - c, nch - 1 - (c - ncc))

    def tok(chunk):
        return pl.BlockSpec((nb, L, width), lambda b, c: (b, chunk(c), 0))

    def pls(chunk):
        return pl.BlockSpec((nb, 1, 1, width), lambda b, c: (b, chunk(c), 0, 0))

    def tok3(a):
        return a.reshape(batch, s_tot, width)

    def pl4(a):
        return a.reshape(batch, nch, 1, width)

    kern = functools.partial(_rwkv_scan_kernel, npairs=npairs, nb=nb)
    fwd_in = [tok3(a) for a in (v, at_f, rt_f, bt_f, kt_f, bb_f, kb_f)] + [pl4(pl_f)]
    bwd_in = [tok3(a) for a in (v, at_b, rt_b, bt_b, kt_b, bb_b, kb_b)] + [pl4(pl_b)]
    yf, yb = pl.pallas_call(
        kern,
        grid=(batch // nb, nch),
        in_specs=[tok(fchunk)] * 7 + [pls(fchunk)] + [tok(bchunk)] * 7 + [pls(bchunk)],
        out_specs=[tok(fchunk), tok(bchunk)],
        out_shape=[jax.ShapeDtypeStruct((batch, s_tot, width), F32)] * 2,
        scratch_shapes=[pltpu.VMEM((nb, 2, npairs, LANES, LANES), F32)],
        compiler_params=_cparams("parallel", "arbitrary"),
        name="rwkv_scan",
    )(*fwd_in, *bwd_in)
    return yf.reshape(n, width), yb.reshape(n, width)


def _rwkv_readout_kernel(yf_ref, yb_ref, bonus_ref, gl_ref, g2_ref, lg_ref, lb_ref, o_ref):
    avg = _group_matrix(1.0 / HEAD_DIM)
    gate = _dot(_bf(jax.nn.sigmoid(gl_ref[...].astype(F32))), g2_ref[...])
    for c in range(o_ref.shape[1] // LANES):
        cs = slice(c * LANES, (c + 1) * LANES)
        y = yf_ref[:, cs] + yb_ref[:, cs]
        mu = _dot_split(y, avg, 2)
        dy = y - mu
        var = _dot_split(dy * dy, avg, 2)
        yn = dy * lax.rsqrt(var + LNX_EPS) * lg_ref[:, cs] + lb_ref[:, cs]
        o_ref[:, cs] = _bf((yn + bonus_ref[:, cs].astype(F32)) * gate[:, cs])


def _rwkv_readout(yf, yb, bonus, big, g2, lnx_g, lnx_b):
    n, width = yf.shape
    tm = _pick_tile(n, (512, 256))
    tok = pl.BlockSpec((tm, width), lambda i: (i, 0))
    full = lambda a: pl.BlockSpec(a.shape, lambda i: (0,) * a.ndim)
    return pl.pallas_call(
        _rwkv_readout_kernel,
        grid=(n // tm,),
        in_specs=[tok, tok, tok, pl.BlockSpec((tm, LANES), lambda i: (i, COL_GLOW // LANES)),
                  full(g2), full(lnx_g), full(lnx_b)],
        out_specs=tok,
        out_shape=jax.ShapeDtypeStruct((n, width), BF16),
        compiler_params=_cparams("parallel"),
        name="rwkv_readout",
    )(yf, yb, bonus, big, g2, lnx_g, lnx_b)


def _merge_kernel(oa_ref, ob_ref, oc_ref, gt_ref, x_ref, wb_ref, wo_ref, ng_ref, mod_ref, o_ref, *, nsub, d):
    y = None
    for br, ref in enumerate((oa_ref, ob_ref, oc_ref)):
        g = jax.nn.sigmoid(gt_ref[:, br * d:(br + 1) * d].astype(F32))
        t = g * _dot(ref[...], wb_ref[br])
        y = t if y is None else y + t
    o = _dot(_bf(y), wo_ref[...])
    o_ref[...] = x_ref[...] + _mod_rows(mod_ref, nsub, 2, d) * _rms(o, ng_ref[...])


def _merge(oa, ob, oc, big, x, wb, wo, ng, modblk):
    n, d = x.shape
    width = oa.shape[1]
    tm = _pick_tile(n, (512, 256))
    nsub = tm // MOD_ROWS
    kern = functools.partial(_merge_kernel, nsub=nsub, d=d)
    br = pl.BlockSpec((tm, width), lambda i: (i, 0))
    return pl.pallas_call(
        kern,
        grid=(n // tm,),
        in_specs=[br, br, br,
                  pl.BlockSpec((tm, 3 * d), lambda i: (i, COL_GATES // (3 * d))),
                  pl.BlockSpec((tm, d), lambda i: (i, 0)),
                  pl.BlockSpec(wb.shape, lambda i: (0, 0, 0)),
                  pl.BlockSpec(wo.shape, lambda i: (0, 0)),
                  pl.BlockSpec((1, d), lambda i: (0, 0)),
                  pl.BlockSpec((nsub, 1, 6 * d), lambda i: (i, 0, 0))],
        out_specs=pl.BlockSpec((tm, d), lambda i: (i, 0)),
        out_shape=jax.ShapeDtypeStruct((n, d), F32),
        compiler_params=_cparams("parallel"),
        name="merge_out_proj",
    )(oa, ob, oc, big, x, wb, wo, ng, modblk)


def _ffn_kernel(x_ref, g_ref, mod_ref, wgu_ref, wd_ref, ng_ref, o_ref, h_ref, acc_ref, *, nsub, d, f, chunks):
    y = _rms(x_ref[...], g_ref[...])
    h_ref[...] = _bf(y * (1.0 + _mod_rows(mod_ref, nsub, 4, d)) + _mod_rows(mod_ref, nsub, 3, d))
    for c0, w in chunks:
        h = h_ref[...]
        g = _dot(h, wgu_ref[:, c0:c0 + w])
        u = _dot(h, wgu_ref[:, f + c0:f + c0 + w])
        part = _dot(_bf(g * jax.nn.sigmoid(g) * u), wd_ref[c0:c0 + w, :])
        if c0 == 0:
            acc_ref[...] = part
        else:
            acc_ref[...] += part
    o_ref[...] = x_ref[...] + _mod_rows(mod_ref, nsub, 5, d) * _rms(acc_ref[...], ng_ref[...])


def _dense_ffn(x, gain_in, gain_out, modblk, w_gu, w_down):
    n, d = x.shape
    f = w_down.shape[0]
    tm = _pick_tile(n, (1024, 512, 256))
    nsub = tm // MOD_ROWS
    cw = 2 * MXU_TILE
    chunks = tuple((c0, min(cw, f - c0)) for c0 in range(0, f, cw))
    assert f % MXU_TILE == 0
    kern = functools.partial(_ffn_kernel, nsub=nsub, d=d, f=f, chunks=chunks)
    once = pl.Buffered(1)
    return pl.pallas_call(
        kern,
        grid=(n // tm,),
        in_specs=[pl.BlockSpec((tm, d), lambda i: (i, 0)),
                  pl.BlockSpec((1, d), lambda i: (0, 0)),
                  pl.BlockSpec((nsub, 1, 6 * d), lambda i: (i, 0, 0)),
                  pl.BlockSpec((d, 2 * f), lambda i: (0, 0), pipeline_mode=once),
                  pl.BlockSpec((f, d), lambda i: (0, 0), pipeline_mode=once),
                  pl.BlockSpec((1, d), lambda i: (0, 0))],
        out_specs=pl.BlockSpec((tm, d), lambda i: (i, 0)),
        out_shape=jax.ShapeDtypeStruct((n, d), F32),
        scratch_shapes=[pltpu.VMEM((tm, d), BF16), pltpu.VMEM((tm, d), F32)],
        compiler_params=_cparams("parallel"),
        name="dense_swiglu_ffn",
    )(x, gain_in, modblk, w_gu, w_down, gain_out)


def _router_kernel(x_ref, g_ref, mod_ref, wr_ref, h_ref, comb_ref, rank_ref, combt_ref, rankt_ref, cnt_ref,
                   *, nsub, d):
    lane = lax.broadcasted_iota(jnp.int32, (MOD_ROWS, LANES), 1)
    r2 = lax.broadcasted_iota(jnp.int32, (MOD_ROWS, MOD_ROWS), 0)
    c2 = lax.broadcasted_iota(jnp.int32, (MOD_ROWS, MOD_ROWS), 1)
    tri = jnp.where(c2 < r2, 1.0, 0.0).astype(BF16)
    ninf = jnp.float32(-jnp.inf)
    running = jnp.zeros((1, LANES), F32)
    for s in range(nsub):
        rows = slice(s * MOD_ROWS, (s + 1) * MOD_ROWS)
        m = mod_ref[s]
        h = _rms(x_ref[rows, :], g_ref[...]) * (1.0 + m[:, 4 * d:5 * d]) + m[:, 3 * d:4 * d]
        hh = _bf(h)
        h_ref[rows, :] = hh
        hl = _bf(h - hh.astype(F32))
        both = _dot(hh, wr_ref[...])
        logits = both[:, :LANES] + both[:, LANES:] + _dot(hl, wr_ref[:, :LANES])
        logits = jnp.where(lane < N_EXPERTS, logits, ninf)
        m1 = jnp.max(logits, axis=-1, keepdims=True)
        i1 = jnp.min(jnp.where(logits == m1, lane, LANES), axis=-1, keepdims=True)
        rest = jnp.where(lane == i1, ninf, logits)
        m2 = jnp.max(rest, axis=-1, keepdims=True)
        i2 = jnp.min(jnp.where(rest == m2, lane, LANES), axis=-1, keepdims=True)
        e2 = jnp.exp(m2 - m1)
        w1 = 1.0 / (1.0 + e2)
        comb = jnp.where(lane == i1, w1, 0.0) + jnp.where(lane == i2, e2 * w1, 0.0)
        ind = jnp.where(comb > 0.0, 1.0, 0.0)
        rank = _dot(tri, _bf(ind)) + running
        running = running + jnp.sum(ind, axis=0, keepdims=True)
        comb_ref[rows, :] = comb
        rank_ref[rows, :] = rank
        combt_ref[:, rows] = comb.T[:N_EXPERTS, :]
        rankt_ref[:, rows] = rank.T[:N_EXPERTS, :]
    cnt_ref[0] = running


def _router(x, gain_in, modblk, w_router_pad, tm):
    n, d = x.shape
    nsub = tm // MOD_ROWS
    nt = n // tm
    kern = functools.partial(_router_kernel, nsub=nsub, d=d)
    tokm = pl.BlockSpec((tm, LANES), lambda i: (i, 0))
    expm = pl.BlockSpec((N_EXPERTS, tm), lambda i: (0, i))
    return pl.pallas_call(
        kern,
        grid=(nt,),
        in_specs=[pl.BlockSpec((tm, d), lambda i: (i, 0)),
                  pl.BlockSpec((1, d), lambda i: (0, 0)),
                  pl.BlockSpec((nsub, 1, 6 * d), lambda i: (i, 0, 0)),
                  pl.BlockSpec((d, 2 * LANES), lambda i: (0, 0))],
        out_specs=[pl.BlockSpec((tm, d), lambda i: (i, 0)), tokm, tokm, expm, expm,
                   pl.BlockSpec((1, 1, LANES), lambda i: (i, 0, 0))],
        out_shape=[jax.ShapeDtypeStruct((n, d), BF16),
                   jax.ShapeDtypeStruct((n, LANES), F32), jax.ShapeDtypeStruct((n, LANES), F32),
                   jax.ShapeDtypeStruct((N_EXPERTS, n), F32), jax.ShapeDtypeStruct((N_EXPERTS, n), F32),
                   jax.ShapeDtypeStruct((nt, 1, LANES), F32)],
        compiler_params=_cparams("parallel"),
        name="moe_router",
    )(x, gain_in, modblk, w_router_pad)


def _moe_kernel(cnt_ref, h_ref, comb_ref, rank_ref, combt_ref, rankt_ref, wg_ref, wu_ref, wd_ref, o_ref,
                xe_ref, y_ref, *, tm, rb):
    i = pl.program_id(0)
    e = pl.program_id(1)
    j = pl.program_id(2)
    nf = pl.num_programs(2)
    cnt = cnt_ref[i * N_EXPERTS + e]
    half, quarter = rb // 2, rb // 4
    nblk = cnt // rb
    tail0 = pl.multiple_of(nblk * rb, rb)
    rem = cnt - tail0

    def for_blocks(fn):
        lax.fori_loop(0, nblk, lambda b, carry: fn(pl.multiple_of(b * rb, rb), rb) or carry, 0)

        @pl.when(rem > half + quarter)
        def _():
            fn(tail0, rb)

        @pl.when(jnp.logical_and(rem > quarter, rem <= half + quarter))
        def _():
            fn(tail0, half)

        @pl.when(jnp.logical_or(jnp.logical_and(rem > 0, rem <= quarter),
                                jnp.logical_and(rem > half, rem <= half + quarter)))
        def _():
            fn(pl.multiple_of(tail0 + jnp.where(rem > half, half, 0), quarter), quarter)

    @pl.when(jnp.logical_and(e == 0, j == 0))
    def _():
        o_ref[...] = jnp.zeros_like(o_ref)

    @pl.when(j == 0)
    def _():
        key = jnp.where(combt_ref[pl.ds(e, 1), :] > 0.0, rankt_ref[pl.ds(e, 1), :], -1.0)

        def gather(r0, nr):
            want = (r0 + lax.broadcasted_iota(jnp.int32, (nr, tm), 0)).astype(F32)
            sel = jnp.where(key == want, 1.0, 0.0).astype(BF16)
            xe_ref[pl.ds(r0, nr), :] = _bf(_dot(sel, h_ref[...]))

        for_blocks(gather)

    def expert(r0, nr):
        rows = pl.ds(r0, nr)
        xb = xe_ref[rows, :]
        g = _dot(xb, wg_ref[0])
        u = _dot(xb, wu_ref[0])
        part = _dot(_bf(g * jax.nn.sigmoid(g) * u), wd_ref[0])

        @pl.when(j == 0)
        def _():
            y_ref[rows, :] = _bf(part)

        @pl.when(j > 0)
        def _():
            y_ref[rows, :] = _bf(y_ref[rows, :].astype(F32) + part)

    for_blocks(expert)

    @pl.when(j == nf - 1)
    def _():
        lane = lax.broadcasted_iota(jnp.int32, (tm, LANES), 1)
        rank_col = jnp.sum(jnp.where(lane == e, rank_ref[...], 0.0), axis=1, keepdims=True)
        w_col = jnp.sum(jnp.where(lane == e, comb_ref[...], 0.0), axis=1, keepdims=True)

        def scatter(r0, nr):
            want = (r0 + lax.broadcasted_iota(jnp.int32, (tm, nr), 1)).astype(F32)
            selw = _bf(jnp.where(rank_col == want, w_col, 0.0))
            o_ref[...] += _dot(selw, y_ref[pl.ds(r0, nr), :])

        for_blocks(scatter)


def _moe_ffn(hb, comb, rank, combt, rankt, counts, w_gu, w_down, tm):
    n, d = hb.shape
    ne, f, _ = w_down.shape
    tf = _pick_tile(f, (1792, 512, 256, 128))
    nf = f // tf
    rb = MOE_ROW_BLOCK
    kern = functools.partial(_moe_kernel, tm=tm, rb=rb)
    once = pl.Buffered(1)
    tokm = pl.BlockSpec((tm, LANES), lambda i, e, j, c: (i, 0), pipeline_mode=once)
    expm = pl.BlockSpec((N_EXPERTS, tm), lambda i, e, j, c: (0, i))
    grid_spec = pltpu.PrefetchScalarGridSpec(
        num_scalar_prefetch=1,
        grid=(n // tm, ne, nf),
        in_specs=[pl.BlockSpec((tm, d), lambda i, e, j, c: (i, 0), pipeline_mode=once), tokm, tokm, expm, expm,
                  pl.BlockSpec((1, d, tf), lambda i, e, j, c: (e, 0, j)),
                  pl.BlockSpec((1, d, tf), lambda i, e, j, c: (e, 0, j + nf)),
                  pl.BlockSpec((1, tf, d), lambda i, e, j, c: (e, j, 0))],
        out_specs=pl.BlockSpec((tm, d), lambda i, e, j, c: (i, 0), pipeline_mode=once),
        scratch_shapes=[pltpu.VMEM((tm, d), BF16), pltpu.VMEM((tm, d), BF16)])
    return pl.pallas_call(
        kern,
        grid_spec=grid_spec,
        out_shape=jax.ShapeDtypeStruct((n, d), F32),
        compiler_params=_cparams("parallel", "arbitrary", "arbitrary"),
        name="moe_swiglu_ffn",
    )(counts, hb, comb, rank, combt, rankt, w_gu, w_gu, w_down)


def _residual_kernel(x_ref, y_ref, ng_ref, mod_ref, o_ref, *, nsub, d):
    o_ref[...] = x_ref[...] + _mod_rows(mod_ref, nsub, 5, d) * _rms(y_ref[...].astype(F32), ng_ref[...])


def _gated_residual(x, y, gain_out, modblk):
    n, d = x.shape
    tm = _pick_tile(n, (512, 256))
    nsub = tm // MOD_ROWS
    tok = pl.BlockSpec((tm, d), lambda i: (i, 0))
    return pl.pallas_call(
        functools.partial(_residual_kernel, nsub=nsub, d=d),
        grid=(n // tm,),
        in_specs=[tok, tok, pl.BlockSpec((1, d), lambda i: (0, 0)),
                  pl.BlockSpec((nsub, 1, 6 * d), lambda i: (i, 0, 0))],
        out_specs=tok,
        out_shape=jax.ShapeDtypeStruct((n, d), F32),
        compiler_params=_cparams("parallel"),
        name="moe_gated_residual",
    )(x, y, gain_out, modblk)


def _rope_tables(t_len, c_len):
    pairs = HEAD_DIM // 4
    rows = t_len // GRID_W
    row = jnp.repeat(jnp.arange(rows, dtype=F32), GRID_W)
    col = jnp.tile(jnp.arange(GRID_W, dtype=F32), rows)
    freqs = ROPE_BASE ** (-jnp.arange(pairs, dtype=F32) / pairs)
    ar = row[:, None] * freqs
    ac = col[:, None] * freqs
    cos = jnp.concatenate([jnp.cos(ar), jnp.cos(ar), jnp.cos(ac), jnp.cos(ac)], axis=1)
    sin = jnp.concatenate([-jnp.sin(ar), jnp.sin(ar), -jnp.sin(ac), jnp.sin(ac)], axis=1)
    cos = jnp.concatenate([jnp.ones((c_len, HEAD_DIM), F32), cos], axis=0)
    sin = jnp.concatenate([jnp.zeros((c_len, HEAD_DIM), F32), sin], axis=0)
    return jnp.tile(cos, (1, 2)), jnp.tile(sin, (1, 2))


def _block_diag2(w):
    z = jnp.zeros_like(w[0])
    return jnp.concatenate([jnp.concatenate([w[0], z], axis=1), jnp.concatenate([z, w[1]], axis=1)], axis=0)


def kernel(x, c, ctx, c_ctx, w_mod, b_mod, norm_gain, w_in, qk_gain, rwkv_conv, decay_w0, decay_w2, iclr_a0, iclr_a2, key_k, bonus_rk, gate_g2, lnx_gain, lnx_bias, cmlp_ln_gain, cmlp_ln_bias, cmlp_ws, cmlp_bs, w_branch, w_out, ffn_w_gu, ffn_w_down, moe_router, moe_w_gu, moe_w_down):
    batch, t_len, d = x.shape
    c_len = ctx.shape[1]
    depth = w_mod.shape[0]
    s_tot = c_len + t_len
    n = batch * s_tot
    assert c_len % MOD_ROWS == 0 and t_len % MOD_ROWS == 0 and d % LANES == 0
    width = bonus_rk.shape[1] * bonus_rk.shape[2]
    nheads = width // HEAD_DIM

    xs = jnp.concatenate([ctx, x], axis=1).reshape(n, d)

    mod_rows = 8 * ((batch + 1 + 7) // 8)
    cvec = jnp.zeros((mod_rows, d), F32).at[0].set(c_ctx).at[1:batch + 1].set(c)
    mods = _modulation(cvec, w_mod, b_mod)
    mod_ctx = jnp.broadcast_to(mods[:, 0:1, None, :], (depth, batch, c_len // MOD_ROWS, 6 * d))
    mod_lat = jnp.broadcast_to(mods[:, 1:batch + 1, None, :], (depth, batch, t_len // MOD_ROWS, 6 * d))
    modblk_all = jnp.concatenate([mod_ctx, mod_lat], axis=2).reshape(depth, n // MOD_ROWS, 1, 6 * d)

    order = np.array(Q_HEAD_ORDER)
    nl = depth
    wq = w_in[:, :, 2048:2560].reshape(nl, d, nheads, HEAD_DIM)[:, :, order].reshape(nl, d, width)
    w_in_p = jnp.concatenate([
        w_in[:, :, 256:1792], wq, w_in[:, :, 2688:3712], w_in[:, :, 3712:6784],
        w_in[:, :, 0:128], w_in[:, :, 128:256], w_in[:, :, 1792:1920], w_in[:, :, 1920:2048],
        w_in[:, :, 2560:2688], jnp.zeros((nl, d, IN_PAD - 6784), F32)], axis=2).astype(BF16)
    wb = w_branch.astype(BF16)
    wb0 = wb[:, 0].reshape(nl, nheads, HEAD_DIM, d)[:, order].reshape(nl, width, d)
    wb = jnp.concatenate([wb0[:, None], wb[:, 1:]], axis=1)
    wo = w_out.astype(BF16)
    cos, sin = _rope_tables(t_len, c_len)
    qg = jnp.tile(qk_gain[:, 0], (1, 2))[:, None, :]
    kg = jnp.tile(qk_gain[:, 1], (1, 2))[:, None, :]
    ws_b = cmlp_ws.astype(BF16)
    bs_b = jnp.broadcast_to(cmlp_bs[..., None], cmlp_bs.shape + (CMLP_CHUNK,))
    w2s = jnp.stack([_block_diag2(decay_w2[l]) for l in range(nl)]).astype(BF16)
    a2s = jnp.stack([_block_diag2(iclr_a2[l]) for l in range(nl)]).astype(BF16)
    w0 = decay_w0.reshape(nl, 1, 2 * width)
    a0 = iclr_a0.reshape(nl, 1, 2 * width)
    rk = bonus_rk.reshape(nl, 1, width)
    g2 = gate_g2.astype(BF16)
    ffn_gu = ffn_w_gu.astype(BF16)
    ffn_dn = ffn_w_down.astype(BF16)
    moe_gu = moe_w_gu.astype(BF16)
    moe_dn = moe_w_down.astype(BF16)
    router_pad = jnp.pad(moe_router, ((0, 0), (0, 0), (0, LANES - moe_router.shape[2])))
    router_hi = router_pad.astype(BF16)
    router_pad = jnp.concatenate([router_hi, (router_pad - router_hi.astype(F32)).astype(BF16)], axis=2)

    for l in range(depth):
        modblk = modblk_all[l]
        ng = norm_gain[l]
        big = _norm_mod_matmul(xs, ng[0:1], modblk, w_in_p[l], 0, 1)
        qh, kbd, vt = _qk_prep(big, qg[l], kg[l], cos, sin, s_tot)
        oa = _attention(qh, kbd, vt, batch, s_tot, c_len)
        prep = _rwkv_prep(big, rwkv_conv[l], w0[l], w2s[l], a0[l], a2s[l],
                          key_k[l, 0:1], key_k[l, 1:2], rk[l], s_tot, c_len)
        yf, yb = _rwkv_scan(prep, batch, s_tot, c_len)
        ob = _rwkv_readout(yf, yb, prep[1], big, g2[l], lnx_gain[l][None], lnx_bias[l][None])
        oc = _chunk_mlp(big, cmlp_ln_gain[l][None], cmlp_ln_bias[l][None], ws_b[l], bs_b[l])
        xs = _merge(oa, ob, oc, big, xs, wb[l], wo[l], ng[1:2], modblk)
        if l % 2 == 0:
            xs = _dense_ffn(xs, ng[2:3], ng[3:4], modblk, ffn_gu[l // 2], ffn_dn[l // 2])
        else:
            tmoe = _pick_tile(n, (2048, 1024, 512, 256))
            hb, comb, rank, combt, rankt, cnt = _router(xs, ng[2:3], modblk, router_pad[l // 2], tmoe)
            counts = cnt[:, 0, :N_EXPERTS].astype(jnp.int32).reshape(-1)
            y = _moe_ffn(hb, comb, rank, combt, rankt, counts, moe_gu[l // 2], moe_dn[l // 2], tmoe)
            xs = _gated_residual(xs, y, ng[3:4], modblk)
    return xs.reshape(batch, s_tot, d)[:, c_len:, :]
```

```python
import functools

import jax
import jax.numpy as jnp
import numpy as np
from jax import lax
from jax.experimental import pallas as pl
from jax.experimental.pallas import tpu as pltpu

F32 = jnp.float32
BF16 = jnp.bfloat16
HIGHEST = lax.Precision.HIGHEST

EPS = 1e-6
LNX_EPS = 64e-5
HEAD_DIM = 64
ROPE_BASE = 10000.0
GRID_W = 64
LANES = 128
MXU_TILE = 256
MOD_ROWS = 256
SCAN_CHUNK = 64
CMLP_CHUNK = 128
ATT_TK = 256
ATT_VROWS = HEAD_DIM + 16
ATT_GROUP = 1
N_EXPERTS = 8
MOE_ROW_BLOCK = 512
VMEM_LIMIT = 56 * 1024 * 1024

COL_RKV, COL_Q, COL_UV, COL_GATES = 0, 1536, 2048, 3072
COL_K, COL_V, COL_WLOW, COL_ALOW, COL_GLOW = 6144, 6272, 6400, 6528, 6656
IN_PAD = 7168
Q_HEAD_ORDER = (0, 4, 1, 5, 2, 6, 3, 7)


def _cparams(*sem):
    return pltpu.CompilerParams(dimension_semantics=sem, vmem_limit_bytes=VMEM_LIMIT)


def _dot(a, b):
    return jnp.dot(a, b, preferred_element_type=F32)


def _dot_nt(a, b):
    return lax.dot_general(a, b, (((1,), (1,)), ((), ())), preferred_element_type=F32)


def _bf(x):
    return x.astype(BF16)


def _dot_split(a, b_exact, terms):
    acc = None
    rem = a
    for _ in range(terms):
        piece = _bf(rem)
        rem = rem - piece.astype(F32)
        part = _dot(piece, b_exact)
        acc = part if acc is None else acc + part
    return acc


def _dot_split_left(a_exact, b, terms):
    acc = None
    rem = b
    for _ in range(terms):
        piece = _bf(rem)
        rem = rem - piece.astype(F32)
        part = _dot(a_exact, piece)
        acc = part if acc is None else acc + part
    return acc


def _group_matrix(scale):
    r = lax.broadcasted_iota(jnp.int32, (LANES, LANES), 0) // HEAD_DIM
    c = lax.broadcasted_iota(jnp.int32, (LANES, LANES), 1) // HEAD_DIM
    return jnp.where(r == c, scale, 0.0).astype(BF16)


def _pick_tile(n, candidates):
    for t in candidates:
        if n % t == 0:
            return t
    raise ValueError(f"no tile in {candidates} divides {n}")


def _mod_rows(mod_ref, nsub, idx, d):
    parts = [jnp.broadcast_to(mod_ref[s][:, idx * d:(idx + 1) * d], (MOD_ROWS, d)) for s in range(nsub)]
    return parts[0] if nsub == 1 else jnp.concatenate(parts, axis=0)


def _rms(x, g):
    return x * lax.rsqrt(jnp.mean(x * x, axis=-1, keepdims=True) + EPS) * g


def _mod_kernel(c_ref, w_ref, b_ref, o_ref):
    cv = c_ref[...]
    s = cv * jax.nn.sigmoid(cv)
    o_ref[0] = jnp.dot(s, w_ref[0], precision=HIGHEST, preferred_element_type=F32) + b_ref[0]


def _modulation(cvec, w_mod, b_mod):
    nl, d, d6 = w_mod.shape
    rows = cvec.shape[0]
    tn = 1024
    return pl.pallas_call(
        _mod_kernel,
        grid=(nl, d6 // tn),
        in_specs=[pl.BlockSpec((rows, d), lambda l, j: (0, 0)),
                  pl.BlockSpec((1, d, tn), lambda l, j: (l, 0, j)),
                  pl.BlockSpec((1, 1, tn), lambda l, j: (l, 0, j))],
        out_specs=pl.BlockSpec((1, rows, tn), lambda l, j: (l, 0, j)),
        out_shape=jax.ShapeDtypeStruct((nl, rows, d6), F32),
        compiler_params=_cparams("parallel", "parallel"),
        name="modulation",
    )(cvec, w_mod, b_mod.reshape(nl, 1, d6))


def _nmm_kernel(x_ref, g_ref, mod_ref, w_ref, o_ref, h_ref, *, nsub, d, shift_idx, scale_idx, tn):
    y = _rms(x_ref[...], g_ref[...])
    sc = _mod_rows(mod_ref, nsub, scale_idx, d)
    sh = _mod_rows(mod_ref, nsub, shift_idx, d)
    h_ref[...] = _bf(y * (1.0 + sc) + sh)
    for c0 in range(0, o_ref.shape[1], tn):
        o_ref[:, c0:c0 + tn] = _bf(_dot(h_ref[...], w_ref[:, c0:c0 + tn]))


def _norm_mod_matmul(x, gain, modblk, w, shift_idx, scale_idx):
    n, d = x.shape
    nout = w.shape[1]
    tm = _pick_tile(n, (512, 256))
    tn = _pick_tile(nout, (1792, 1024))
    nsub = tm // MOD_ROWS
    kern = functools.partial(_nmm_kernel, nsub=nsub, d=d, shift_idx=shift_idx, scale_idx=scale_idx, tn=tn)
    return pl.pallas_call(
        kern,
        grid=(n // tm,),
        in_specs=[pl.BlockSpec((tm, d), lambda i: (i, 0)),
                  pl.BlockSpec((1, d), lambda i: (0, 0)),
                  pl.BlockSpec((nsub, 1, 6 * d), lambda i: (i, 0, 0)),
                  pl.BlockSpec((d, nout), lambda i: (0, 0), pipeline_mode=pl.Buffered(1))],
        out_specs=pl.BlockSpec((tm, nout), lambda i: (i, 0)),
        out_shape=jax.ShapeDtypeStruct((n, nout), BF16),
        scratch_shapes=[pltpu.VMEM((tm, d), BF16)],
        compiler_params=_cparams("parallel"),
        name="norm_mod_in_proj",
    )(x, gain, modblk, w)


def _qkprep_kernel(q_ref, k_ref, v_ref, qg_ref, kg_ref, cos_ref, sin_ref, qo_ref, ko_ref, vo_ref):
    cos = cos_ref[...]
    sin = sin_ref[...]
    avg = _group_matrix(1.0 / HEAD_DIM)
    lane = lax.broadcasted_iota(jnp.int32, cos.shape, 1)
    first = (lane % 32) < 16
    left = lane < HEAD_DIM

    def norm_rope(x, g):
        ms = _dot_split(x * x, avg, 2)
        xn = x * lax.rsqrt(ms + EPS) * g
        partner = jnp.where(first, pltpu.roll(xn, LANES - 16, 1), pltpu.roll(xn, 16, 1))
        return xn * cos + partner * sin

    qscale = (HEAD_DIM ** -0.5) * float(np.log2(np.e))
    for j in range(q_ref.shape[1] // LANES):
        q = q_ref[:, j * LANES:(j + 1) * LANES].astype(F32)
        qo_ref[:, j * LANES:(j + 1) * LANES] = _bf(norm_rope(q, qg_ref[...]) * qscale)
    k = norm_rope(k_ref[...].astype(F32), kg_ref[...])
    zero = jnp.zeros_like(k)
    k0 = _bf(jnp.where(left, k, zero))
    k1 = _bf(jnp.where(left, zero, k))
    v = v_ref[...].astype(F32)
    ones = jnp.ones((ATT_VROWS - HEAD_DIM, ATT_TK), BF16)
    for c in range(vo_ref.shape[0]):
        rows = slice(c * ATT_TK, (c + 1) * ATT_TK)
        ko_ref[c, :ATT_TK, :] = k0[rows]
        ko_ref[c, ATT_TK:, :] = k1[rows]
        vt = _bf(v[rows, :].T)
        for t in range(2):
            vo_ref[c, t, :HEAD_DIM, :] = vt[t * HEAD_DIM:(t + 1) * HEAD_DIM]
            vo_ref[c, t, HEAD_DIM:, :] = ones


def _qk_prep(big, qg, kg, cos, sin, s_tot):
    n = big.shape[0]
    tm = MOD_ROWS
    npos = s_tot // tm
    qw = 512
    vchunks = tm // ATT_TK
    return pl.pallas_call(
        _qkprep_kernel,
        grid=(n // tm,),
        in_specs=[pl.BlockSpec((tm, qw), lambda i: (i, COL_Q // qw)),
                  pl.BlockSpec((tm, LANES), lambda i: (i, COL_K // LANES)),
                  pl.BlockSpec((tm, LANES), lambda i: (i, COL_V // LANES)),
                  pl.BlockSpec((1, LANES), lambda i: (0, 0)),
                  pl.BlockSpec((1, LANES), lambda i: (0, 0)),
                  pl.BlockSpec((tm, LANES), lambda i: (i % npos, 0)),
                  pl.BlockSpec((tm, LANES), lambda i: (i % npos, 0))],
        out_specs=[pl.BlockSpec((tm, qw), lambda i: (i, 0)),
                   pl.BlockSpec((vchunks, 2 * ATT_TK, LANES), lambda i: (i, 0, 0)),
                   pl.BlockSpec((vchunks, 2, ATT_VROWS, ATT_TK), lambda i: (i, 0, 0, 0))],
        out_shape=[jax.ShapeDtypeStruct((n, qw), BF16),
                   jax.ShapeDtypeStruct((n // ATT_TK, 2 * ATT_TK, LANES), BF16),
                   jax.ShapeDtypeStruct((n // ATT_TK, 2, ATT_VROWS, ATT_TK), BF16)],
        compiler_params=_cparams("parallel"),
        name="qk_norm_rope",
    )(big, big, big, qg, kg, cos, sin)


def _attn_kernel(q_ref, k_ref, vt_ref, o_ref, acc_ref, sa_ref, sb_ref, *, tq, tk, n_ctx_q, n_ctx_kv, n_kv):
    i = pl.program_id(1)
    nkv = jnp.where(i < n_ctx_q, n_ctx_kv, n_kv)
    hd = HEAD_DIM
    npair = q_ref.shape[1] // LANES
    nh = 2 * npair
    qs = [q_ref[:, j * LANES:(j + 1) * LANES] for j in range(npair)]
    vr = ATT_VROWS
    acc_ref[...] = jnp.zeros_like(acc_ref)

    def scores_to(dst_ref, kb, h):
        s = _dot_nt(kb[(h % 2) * tk:(h % 2 + 1) * tk], qs[h // 2])
        dst_ref[h, :tk, :] = s
        dst_ref[h, tk:, :] = jnp.broadcast_to(jnp.max(s, axis=0, keepdims=True), (8, tq))

    def consume(src_ref, c, h, mh):
        n = jnp.maximum(mh, src_ref[h, tk:tk + 1, :])
        p = _bf(jnp.exp2(src_ref[h, :tk, :] - n))
        rows = slice(h * vr, (h + 1) * vr)
        acc_ref[rows, :] = acc_ref[rows, :] * jnp.exp2(mh - n) + _dot(vt_ref[c, h % 2], p)
        return n

    def step(src_ref, dst_ref, c, m, prefetch):
        kb = k_ref[c + 1] if prefetch else None
        new_m = []
        for h in range(nh):
            if prefetch:
                scores_to(dst_ref, kb, h)
            new_m.append(consume(src_ref, c, h, m[h]))
        return tuple(new_m)

    def body(u, m):
        c = 2 * u
        m = step(sa_ref, sb_ref, c, m, True)
        return step(sb_ref, sa_ref, c + 1, m, True)

    kb0 = k_ref[0]
    for h in range(nh):
        scores_to(sa_ref, kb0, h)
    m = lax.fori_loop(0, (nkv - 1) // 2, body, (jnp.full((1, tq), -1e30, F32),) * nh)
    step(sa_ref, sb_ref, nkv - 1, m, False)
    for j in range(npair):
        o = [acc_ref[h * vr:h * vr + hd, :] * (1.0 / acc_ref[h * vr + hd:h * vr + hd + 1, :]) for h in (2 * j, 2 * j + 1)]
        o_ref[:, j * LANES:(j + 1) * LANES] = _bf(jnp.concatenate(o, axis=0).T)


def _attention(qh, kbd, vt, batch, s_tot, c_len):
    n, qw = qh.shape
    tq = 256
    tk = ATT_TK
    nq = s_tot // tq
    assert (c_len // tk) % 2 == 1 and (s_tot // tk) % 2 == 1, "the key-chunk loop is unrolled by two plus a tail"
    kern = functools.partial(_attn_kernel, tq=tq, tk=tk, n_ctx_q=c_len // tq,
                             n_ctx_kv=c_len // tk, n_kv=s_tot // tk)
    return pl.pallas_call(
        kern,
        grid=(batch, nq),
        in_specs=[pl.BlockSpec((tq, qw), lambda b, i: (b * nq + i, 0)),
                  pl.BlockSpec((s_tot // tk, 2 * tk, LANES), lambda b, i: (b, 0, 0)),
                  pl.BlockSpec((s_tot // tk, 2, ATT_VROWS, tk), lambda b, i: (b, 0, 0, 0))],
        out_specs=pl.BlockSpec((tq, qw), lambda b, i: (b * nq + i, 0)),
        out_shape=jax.ShapeDtypeStruct((n, qw), BF16),
        scratch_shapes=[pltpu.VMEM((2 * (qw // LANES) * ATT_VROWS, tq), F32),
                        pltpu.VMEM((2 * (qw // LANES), tk + 8, tq), F32),
                        pltpu.VMEM((2 * (qw // LANES), tk + 8, tq), F32)],
        compiler_params=_cparams("parallel", "parallel"),
        name="gqa_attention",
    )(qh, kbd, vt)


def _cmlp_kernel(uv_ref, lng_ref, lnb_ref, ws_ref, bs_ref, o_ref, *, nchunk, width):
    x = uv_ref[...].astype(F32)
    g = 0.5 * x * (1.0 + jnp.tanh(0.7978845608028654 * (x + 0.044715 * (x * x * x))))
    u = g[:, :width]
    v = g[:, width:]
    mu = jnp.mean(v, axis=-1, keepdims=True)
    dv = v - mu
    var = jnp.mean(dv * dv, axis=-1, keepdims=True)
    vn = _bf(dv * lax.rsqrt(var + EPS) * lng_ref[...] + lnb_ref[...])
    ngroups = width // CMLP_CHUNK
    for c in range(nchunk):
        r0 = c * CMLP_CHUNK
        for gi in range(ngroups):
            c0 = gi * CMLP_CHUNK
            s = _dot(ws_ref[gi], vn[r0:r0 + CMLP_CHUNK, c0:c0 + CMLP_CHUNK]) + bs_ref[gi]
            o_ref[r0:r0 + CMLP_CHUNK, c0:c0 + CMLP_CHUNK] = _bf(u[r0:r0 + CMLP_CHUNK, c0:c0 + CMLP_CHUNK] * s)


def _chunk_mlp(big, ln_g, ln_b, ws, bs_b):
    n = big.shape[0]
    width = ln_g.shape[1]
    tr = _pick_tile(n, (512, 256, 128))
    kern = functools.partial(_cmlp_kernel, nchunk=tr // CMLP_CHUNK, width=width)
    ng = ws.shape[0]
    return pl.pallas_call(
        kern,
        grid=(n // tr,),
        in_specs=[pl.BlockSpec((tr, 2 * width), lambda i: (i, COL_UV // (2 * width))),
                  pl.BlockSpec((1, width), lambda i: (0, 0)),
                  pl.BlockSpec((1, width), lambda i: (0, 0)),
                  pl.BlockSpec((ng, CMLP_CHUNK, CMLP_CHUNK), lambda i: (0, 0, 0)),
                  pl.BlockSpec((ng, CMLP_CHUNK, CMLP_CHUNK), lambda i: (0, 0, 0))],
        out_specs=pl.BlockSpec((tr, width), lambda i: (i, 0)),
        out_shape=jax.ShapeDtypeStruct((n, width), BF16),
        compiler_params=_cparams("parallel"),
        name="chunk_gmlp",
    )(big, ln_g, ln_b, ws, bs_b)


def _rwkv_prep_kernel(x_ref, xp_ref, xn_ref, lo_ref, conv_ref, w0_ref, w2_ref, a0_ref, a2_ref,
                      kk0_ref, kk1_ref, rk_ref,
                      v_o, bonus_o, at_f, rt_f, bt_f, kt_f, bb_f, kb_f, pl_f,
                      at_b, rt_b, bt_b, kt_b, bb_b, kb_b, pl_b, *, tm, width, blocks_per_seq, ctx_blocks):
    i = pl.program_id(0)
    j = i % blocks_per_seq
    is_first = jnp.logical_or(j == 0, j == ctx_blocks)
    is_last = jnp.logical_or(j == ctx_blocks - 1, j == blocks_per_seq - 1)
    row = lax.broadcasted_iota(jnp.int32, (tm, width), 0)
    gsum = _group_matrix(1.0)
    halo = xp_ref.shape[0]

    def conv(c):
        cs = slice(c * width, (c + 1) * width)
        x = x_ref[:, cs].astype(F32)
        prev_row = jnp.where(is_first, 0.0, xp_ref[halo - 1:halo, cs].astype(F32))
        next_row = jnp.where(is_last, 0.0, xn_ref[0:1, cs].astype(F32))
        xprev = jnp.where(row == 0, prev_row, pltpu.roll(x, 1, 0))
        xnext = jnp.where(row == tm - 1, next_row, pltpu.roll(x, tm - 1, 0))
        return xprev * conv_ref[0:1, cs] + x * conv_ref[1:2, cs] + xnext * conv_ref[2:3, cs]

    r = conv(0)
    k = conv(1)
    v = conv(2)
    v_o[...] = _bf(v)

    def group_sum(x):
        parts = [_dot_split(x[:, c * LANES:(c + 1) * LANES], gsum, 2) for c in range(width // LANES)]
        return jnp.concatenate(parts, axis=1)

    kk = k * kk0_ref[...]
    kk = kk * lax.rsqrt(group_sum(kk * kk) + 1e-12)
    bonus_o[...] = _bf(group_sum(r * k * rk_ref[...]) * v)

    lo = lo_ref[...].astype(F32)
    wd = w0_ref[...] + _dot(_bf(jnp.tanh(lo[:, :LANES])), w2_ref[...])
    ad = jax.nn.sigmoid(a0_ref[...] + _dot(_bf(lo[:, LANES:]), a2_ref[...]))
    lw = -float(np.exp(-0.5) * np.log2(np.e)) * jax.nn.sigmoid(wd)

    r2 = lax.broadcasted_iota(jnp.int32, (tm, tm), 0)
    c2 = lax.broadcasted_iota(jnp.int32, (tm, tm), 1)
    same = (r2 // SCAN_CHUNK) == (c2 // SCAN_CHUNK)
    tri_pre = jnp.where(jnp.logical_and(same, c2 <= r2), 1.0, 0.0).astype(BF16)
    tri_suf = jnp.where(jnp.logical_and(same, c2 >= r2), 1.0, 0.0).astype(BF16)
    nchunk = tm // SCAN_CHUNK

    outs = ((at_f, rt_f, bt_f, kt_f, bb_f, kb_f, pl_f), (at_b, rt_b, bt_b, kt_b, bb_b, kb_b, pl_b))
    for d in range(2):
        ds_ = slice(d * width, (d + 1) * width)
        lwd = lw[:, ds_]
        pre = _dot_split_left(tri_pre, lwd, 3)
        suf = _dot_split_left(tri_suf, lwd, 3)
        cin, rem = (pre, suf - lwd) if d == 0 else (suf, pre - lwd)
        cex = cin - lwd
        a_d = ad[:, ds_]
        b = kk * a_d
        kd = k * (1.0 + (a_d - 1.0) * kk1_ref[...])
        at_o, rt_o, bt_o, kt_o, bb_o, kb_o, pl_o = outs[d]
        at_o[...] = _bf(-kk * jnp.exp2(cex))
        rt_o[...] = _bf(r * jnp.exp2(cin))
        pinv = jnp.exp2(-cin)
        bt_o[...] = _bf(b * pinv)
        kt_o[...] = _bf(kd * pinv)
        pend = jnp.exp2(rem)
        bb_o[...] = _bf(b * pend)
        kb_o[...] = _bf(kd * pend)
        for c in range(nchunk):
            last = (c + 1) * SCAN_CHUNK - 1
            pl_o[c] = jnp.exp2(pre[last:last + 1, :])


def _rwkv_prep(big, conv_w, w0, w2s, a0, a2s, kk0, kk1, rk, s_tot, c_len):
    n = big.shape[0]
    width = rk.shape[1]
    tm = MOD_ROWS
    halo = 16
    hb = tm // halo
    nhalo = n // halo
    nchunk = tm // SCAN_CHUNK
    kern = functools.partial(_rwkv_prep_kernel, tm=tm, width=width, blocks_per_seq=s_tot // tm,
                             ctx_blocks=c_len // tm)
    tok = pl.BlockSpec((tm, width), lambda i: (i, 0))
    plspec = pl.BlockSpec((nchunk, 1, width), lambda i: (i, 0, 0))
    tok_shape = jax.ShapeDtypeStruct((n, width), BF16)
    pl_shape = jax.ShapeDtypeStruct((n // SCAN_CHUNK, 1, width), F32)
    full = lambda a: pl.BlockSpec(a.shape, lambda i: (0,) * a.ndim)
    return pl.pallas_call(
        kern,
        grid=(n // tm,),
        in_specs=[pl.BlockSpec((tm, 3 * width), lambda i: (i, 0)),
                  pl.BlockSpec((halo, 3 * width), lambda i: (jnp.maximum(i * hb - 1, 0), 0)),
                  pl.BlockSpec((halo, 3 * width), lambda i: (jnp.minimum((i + 1) * hb, nhalo - 1), 0)),
                  pl.BlockSpec((tm, 2 * LANES), lambda i: (i, COL_WLOW // (2 * LANES))),
                  full(conv_w), full(w0), full(w2s), full(a0), full(a2s), full(kk0), full(kk1), full(rk)],
        out_specs=[tok, tok] + [tok] * 6 + [plspec] + [tok] * 6 + [plspec],
        out_shape=[tok_shape, tok_shape] + [tok_shape] * 6 + [pl_shape] + [tok_shape] * 6 + [pl_shape],
        compiler_params=_cparams("parallel"),
        name="rwkv_prepare",
    )(big, big, big, big, conv_w, w0, w2s, a0, a2s, kk0, kk1, rk)


def _scan_chunks(chains):
    L = SCAN_CHUNK
    lane = lax.broadcasted_iota(jnp.int32, (L, LANES), 1)
    m0 = _bf(jnp.where(lane < HEAD_DIM, 1.0, 0.0))
    m1 = _bf(jnp.where(lane < HEAD_DIM, 0.0, 1.0))

    def stack(x):
        blocks = [x[:, c:c + LANES] for c in range(0, x.shape[1], LANES)]
        top = [b * m0 for b in blocks]
        bot = [b * m1 for b in blocks]
        if len(blocks) == 1:
            return jnp.concatenate([top[0], bot[0]], axis=0)
        return jnp.concatenate([jnp.concatenate(top, axis=1), jnp.concatenate(bot, axis=1)], axis=0)

    trow = lax.broadcasted_iota(jnp.int32, (L, LANES), 0)
    tcol = lax.broadcasted_iota(jnp.int32, (L, LANES), 1) % L
    masks = {True: (tcol < trow, tcol <= trow), False: (tcol > trow, tcol >= trow)}
    eye = lax.broadcasted_iota(jnp.int32, (LANES, LANES), 0) == lax.broadcasted_iota(jnp.int32, (LANES, LANES), 1)
    fwd = [ch[9] for ch in chains]
    nc = range(len(chains))

    v_s = [stack(ch[6]) for ch in chains]
    big1 = [_dot_nt(jnp.concatenate([chains[i][0], chains[i][1]], axis=0),
                    jnp.concatenate([stack(chains[i][2]), stack(chains[i][3])], axis=0)) for i in nc]
    pm = [jnp.where(masks[fwd[i]][0], big1[i][:L, :LANES], 0.0) for i in nc]
    mak = [_bf(jnp.where(masks[fwd[i]][0], big1[i][:L, LANES:], 0.0)) for i in nc]
    lhs_top = [_bf(jnp.where(jnp.concatenate([masks[fwd[i]][1]] * 2, axis=1), big1[i][L:], 0.0)) for i in nc]
    mv = [_dot(mak[i], v_s[i]) for i in nc]
    px = [jnp.concatenate([chains[i][0].astype(F32), mv[i]], axis=1) for i in nc]
    steps = int(np.log2(L))
    for it in range(steps):
        if it < steps - 1:
            res = [_dot(_bf(pm[i]), stack(_bf(jnp.concatenate([pm[i], px[i]], axis=1)))) for i in nc]
            px = [px[i] + res[i][:, LANES:] for i in nc]
            pm = [res[i][:, :LANES] for i in nc]
        else:
            res = [_dot(_bf(pm[i]), stack(_bf(px[i]))) for i in nc]
            px = [px[i] + res[i] for i in nc]
    rhs2 = [jnp.concatenate([stack(_bf(px[i])), jnp.concatenate([jnp.zeros_like(v_s[i]), v_s[i]], axis=1)], axis=0)
            for i in nc]
    lhs_bot = [_bf(jnp.concatenate([stack(chains[i][4]), stack(chains[i][5])], axis=0).astype(F32).T)
               for i in nc]
    res2 = [_dot(jnp.concatenate([lhs_top[i], lhs_bot[i]], axis=0), rhs2[i]) for i in nc]
    lhs3 = [_bf(jnp.concatenate(
        [chains[i][1].astype(F32) + res2[i][:L, :LANES],
         res2[i][L:, :LANES] + jnp.where(eye, jnp.broadcast_to(chains[i][7], (LANES, LANES)), 0.0)], axis=0))
        for i in nc]
    res3 = [_dot(lhs3[i], _bf(chains[i][8])) for i in nc]
    return [(res3[i][:L] + res2[i][:L, LANES:], res3[i][L:] + res2[i][L:, LANES:]) for i in nc]


def _rwkv_scan_kernel(v_f, at_f, rt_f, bt_f, kt_f, bb_f, kb_f, pl_f,
                      v_b, at_b, rt_b, bt_b, kt_b, bb_b, kb_b, pl_b,
                      yf_ref, yb_ref, z_ref, *, npairs, nb):
    @pl.when(pl.program_id(1) == 0)
    def _():
        z_ref[...] = jnp.zeros_like(z_ref)

    dirs = ((v_f, at_f, rt_f, bt_f, kt_f, bb_f, kb_f, pl_f, yf_ref, True),
            (v_b, at_b, rt_b, bt_b, kt_b, bb_b, kb_b, pl_b, yb_ref, False))
    chains, dest = [], []
    for s in range(nb):
        for d, (v, at, rt, bt, kt, bb, kb, plr, y_ref, fwd) in enumerate(dirs):
            for p in range(npairs):
                cs = slice(p * LANES, (p + 1) * LANES)
                chains.append((at[s, :, cs], rt[s, :, cs], bt[s, :, cs], kt[s, :, cs], bb[s, :, cs], kb[s, :, cs],
                               v[s, :, cs], plr[s, 0][:, cs], z_ref[s, d, p], fwd))
                dest.append((y_ref, s, cs, d, p))
    for (y_ref, s, cs, d, p), (y, znew) in zip(dest, _scan_chunks(chains)):
        y_ref[s, :, cs] = y
        z_ref[s, d, p] = znew


def _rwkv_scan(prep, batch, s_tot, c_len):
    (v, _bonus, at_f, rt_f, bt_f, kt_f, bb_f, kb_f, pl_f, at_b, rt_b, bt_b, kt_b, bb_b, kb_b, pl_b) = prep
    n, width = v.shape
    L = SCAN_CHUNK
    nch = s_tot // L
    ncc = c_len // L
    npairs = width // LANES
    nb = _pick_tile(batch, (4, 2, 1))

    def fchunk(c):
        return c

    def bchunk(c):
        return jnp.where(c < ncc, ncc - 1 - c, nch - 1 - (c - ncc))

    def tok(chunk):
        return pl.BlockSpec((nb, L, width), lambda b, c: (b, chunk(c), 0))

    def pls(chunk):
        return pl.BlockSpec((nb, 1, 1, width), lambda b, c: (b, chunk(c), 0, 0))

    def tok3(a):
        return a.reshape(batch, s_tot, width)

    def pl4(a):
        return a.reshape(batch, nch, 1, width)

    kern = functools.partial(_rwkv_scan_kernel, npairs=npairs, nb=nb)
    fwd_in = [tok3(a) for a in (v, at_f, rt_f, bt_f, kt_f, bb_f, kb_f)] + [pl4(pl_f)]
    bwd_in = [tok3(a) for a in (v, at_b, rt_b, bt_b, kt_b, bb_b, kb_b)] + [pl4(pl_b)]
    yf, yb = pl.pallas_call(
        kern,
        grid=(batch // nb, nch),
        in_specs=[tok(fchunk)] * 7 + [pls(fchunk)] + [tok(bchunk)] * 7 + [pls(bchunk)],
        out_specs=[tok(fchunk), tok(bchunk)],
        out_shape=[jax.ShapeDtypeStruct((batch, s_tot, width), F32)] * 2,
        scratch_shapes=[pltpu.VMEM((nb, 2, npairs, LANES, LANES), F32)],
        compiler_params=_cparams("parallel", "arbitrary"),
        name="rwkv_scan",
    )(*fwd_in, *bwd_in)
    return yf.reshape(n, width), yb.reshape(n, width)


def _rwkv_readout_kernel(yf_ref, yb_ref, bonus_ref, gl_ref, g2_ref, lg_ref, lb_ref, o_ref):
    avg = _group_matrix(1.0 / HEAD_DIM)
    gate = _dot(_bf(jax.nn.sigmoid(gl_ref[...].astype(F32))), g2_ref[...])
    for c in range(o_ref.shape[1] // LANES):
        cs = slice(c * LANES, (c + 1) * LANES)
        y = yf_ref[:, cs] + yb_ref[:, cs]
        mu = _dot_split(y, avg, 2)
        dy = y - mu
        var = _dot_split(dy * dy, avg, 2)
        yn = dy * lax.rsqrt(var + LNX_EPS) * lg_ref[:, cs] + lb_ref[:, cs]
        o_ref[:, cs] = _bf((yn + bonus_ref[:, cs].astype(F32)) * gate[:, cs])


def _rwkv_readout(yf, yb, bonus, big, g2, lnx_g, lnx_b):
    n, width = yf.shape
    tm = _pick_tile(n, (512, 256))
    tok = pl.BlockSpec((tm, width), lambda i: (i, 0))
    full = lambda a: pl.BlockSpec(a.shape, lambda i: (0,) * a.ndim)
    return pl.pallas_call(
        _rwkv_readout_kernel,
        grid=(n // tm,),
        in_specs=[tok, tok, tok, pl.BlockSpec((tm, LANES), lambda i: (i, COL_GLOW // LANES)),
                  full(g2), full(lnx_g), full(lnx_b)],
        out_specs=tok,
        out_shape=jax.ShapeDtypeStruct((n, width), BF16),
        compiler_params=_cparams("parallel"),
        name="rwkv_readout",
    )(yf, yb, bonus, big, g2, lnx_g, lnx_b)


def _merge_kernel(oa_ref, ob_ref, oc_ref, gt_ref, x_ref, wb_ref, wo_ref, ng_ref, mod_ref, o_ref, *, nsub, d):
    y = None
    for br, ref in enumerate((oa_ref, ob_ref, oc_ref)):
        g = jax.nn.sigmoid(gt_ref[:, br * d:(br + 1) * d].astype(F32))
        t = g * _dot(ref[...], wb_ref[br])
        y = t if y is None else y + t
    o = _dot(_bf(y), wo_ref[...])
    o_ref[...] = x_ref[...] + _mod_rows(mod_ref, nsub, 2, d) * _rms(o, ng_ref[...])


def _merge(oa, ob, oc, big, x, wb, wo, ng, modblk):
    n, d = x.shape
    width = oa.shape[1]
    tm = _pick_tile(n, (512, 256))
    nsub = tm // MOD_ROWS
    kern = functools.partial(_merge_kernel, nsub=nsub, d=d)
    br = pl.BlockSpec((tm, width), lambda i: (i, 0))
    return pl.pallas_call(
        kern,
        grid=(n // tm,),
        in_specs=[br, br, br,
                  pl.BlockSpec((tm, 3 * d), lambda i: (i, COL_GATES // (3 * d))),
                  pl.BlockSpec((tm, d), lambda i: (i, 0)),
                  pl.BlockSpec(wb.shape, lambda i: (0, 0, 0)),
                  pl.BlockSpec(wo.shape, lambda i: (0, 0)),
                  pl.BlockSpec((1, d), lambda i: (0, 0)),
                  pl.BlockSpec((nsub, 1, 6 * d), lambda i: (i, 0, 0))],
        out_specs=pl.BlockSpec((tm, d), lambda i: (i, 0)),
        out_shape=jax.ShapeDtypeStruct((n, d), F32),
        compiler_params=_cparams("parallel"),
        name="merge_out_proj",
    )(oa, ob, oc, big, x, wb, wo, ng, modblk)


def _ffn_kernel(x_ref, g_ref, mod_ref, wgu_ref, wd_ref, ng_ref, o_ref, h_ref, acc_ref, *, nsub, d, f, chunks):
    y = _rms(x_ref[...], g_ref[...])
    h_ref[...] = _bf(y * (1.0 + _mod_rows(mod_ref, nsub, 4, d)) + _mod_rows(mod_ref, nsub, 3, d))
    for c0, w in chunks:
        h = h_ref[...]
        g = _dot(h, wgu_ref[:, c0:c0 + w])
        u = _dot(h, wgu_ref[:, f + c0:f + c0 + w])
        part = _dot(_bf(g * jax.nn.sigmoid(g) * u), wd_ref[c0:c0 + w, :])
        if c0 == 0:
            acc_ref[...] = part
        else:
            acc_ref[...] += part
    o_ref[...] = x_ref[...] + _mod_rows(mod_ref, nsub, 5, d) * _rms(acc_ref[...], ng_ref[...])


def _dense_ffn(x, gain_in, gain_out, modblk, w_gu, w_down):
    n, d = x.shape
    f = w_down.shape[0]
    tm = _pick_tile(n, (1024, 512, 256))
    nsub = tm // MOD_ROWS
    cw = 2 * MXU_TILE
    chunks = tuple((c0, min(cw, f - c0)) for c0 in range(0, f, cw))
    assert f % MXU_TILE == 0
    kern = functools.partial(_ffn_kernel, nsub=nsub, d=d, f=f, chunks=chunks)
    once = pl.Buffered(1)
    return pl.pallas_call(
        kern,
        grid=(n // tm,),
        in_specs=[pl.BlockSpec((tm, d), lambda i: (i, 0)),
                  pl.BlockSpec((1, d), lambda i: (0, 0)),
                  pl.BlockSpec((nsub, 1, 6 * d), lambda i: (i, 0, 0)),
                  pl.BlockSpec((d, 2 * f), lambda i: (0, 0), pipeline_mode=once),
                  pl.BlockSpec((f, d), lambda i: (0, 0), pipeline_mode=once),
                  pl.BlockSpec((1, d), lambda i: (0, 0))],
        out_specs=pl.BlockSpec((tm, d), lambda i: (i, 0)),
        out_shape=jax.ShapeDtypeStruct((n, d), F32),
        scratch_shapes=[pltpu.VMEM((tm, d), BF16), pltpu.VMEM((tm, d), F32)],
        compiler_params=_cparams("parallel"),
        name="dense_swiglu_ffn",
    )(x, gain_in, modblk, w_gu, w_down, gain_out)


def _router_kernel(x_ref, g_ref, mod_ref, wr_ref, h_ref, comb_ref, rank_ref, combt_ref, rankt_ref, cnt_ref,
                   *, nsub, d):
    lane = lax.broadcasted_iota(jnp.int32, (MOD_ROWS, LANES), 1)
    r2 = lax.broadcasted_iota(jnp.int32, (MOD_ROWS, MOD_ROWS), 0)
    c2 = lax.broadcasted_iota(jnp.int32, (MOD_ROWS, MOD_ROWS), 1)
    tri = jnp.where(c2 < r2, 1.0, 0.0).astype(BF16)
    ninf = jnp.float32(-jnp.inf)
    running = jnp.zeros((1, LANES), F32)
    for s in range(nsub):
        rows = slice(s * MOD_ROWS, (s + 1) * MOD_ROWS)
        m = mod_ref[s]
        h = _rms(x_ref[rows, :], g_ref[...]) * (1.0 + m[:, 4 * d:5 * d]) + m[:, 3 * d:4 * d]
        hh = _bf(h)
        h_ref[rows, :] = hh
        hl = _bf(h - hh.astype(F32))
        both = _dot(hh, wr_ref[...])
        logits = both[:, :LANES] + both[:, LANES:] + _dot(hl, wr_ref[:, :LANES])
        logits = jnp.where(lane < N_EXPERTS, logits, ninf)
        m1 = jnp.max(logits, axis=-1, keepdims=True)
        i1 = jnp.min(jnp.where(logits == m1, lane, LANES), axis=-1, keepdims=True)
        rest = jnp.where(lane == i1, ninf, logits)
        m2 = jnp.max(rest, axis=-1, keepdims=True)
        i2 = jnp.min(jnp.where(rest == m2, lane, LANES), axis=-1, keepdims=True)
        e2 = jnp.exp(m2 - m1)
        w1 = 1.0 / (1.0 + e2)
        comb = jnp.where(lane == i1, w1, 0.0) + jnp.where(lane == i2, e2 * w1, 0.0)
        ind = jnp.where(comb > 0.0, 1.0, 0.0)
        rank = _dot(tri, _bf(ind)) + running
        running = running + jnp.sum(ind, axis=0, keepdims=True)
        comb_ref[rows, :] = comb
        rank_ref[rows, :] = rank
        combt_ref[:, rows] = comb.T[:N_EXPERTS, :]
        rankt_ref[:, rows] = rank.T[:N_EXPERTS, :]
    cnt_ref[0] = running


def _router(x, gain_in, modblk, w_router_pad, tm):
    n, d = x.shape
    nsub = tm // MOD_ROWS
    nt = n // tm
    kern = functools.partial(_router_kernel, nsub=nsub, d=d)
    tokm = pl.BlockSpec((tm, LANES), lambda i: (i, 0))
    expm = pl.BlockSpec((N_EXPERTS, tm), lambda i: (0, i))
    return pl.pallas_call(
        kern,
        grid=(nt,),
        in_specs=[pl.BlockSpec((tm, d), lambda i: (i, 0)),
                  pl.BlockSpec((1, d), lambda i: (0, 0)),
                  pl.BlockSpec((nsub, 1, 6 * d), lambda i: (i, 0, 0)),
                  pl.BlockSpec((d, 2 * LANES), lambda i: (0, 0))],
        out_specs=[pl.BlockSpec((tm, d), lambda i: (i, 0)), tokm, tokm, expm, expm,
                   pl.BlockSpec((1, 1, LANES), lambda i: (i, 0, 0))],
        out_shape=[jax.ShapeDtypeStruct((n, d), BF16),
                   jax.ShapeDtypeStruct((n, LANES), F32), jax.ShapeDtypeStruct((n, LANES), F32),
                   jax.ShapeDtypeStruct((N_EXPERTS, n), F32), jax.ShapeDtypeStruct((N_EXPERTS, n), F32),
                   jax.ShapeDtypeStruct((nt, 1, LANES), F32)],
        compiler_params=_cparams("parallel"),
        name="moe_router",
    )(x, gain_in, modblk, w_router_pad)


def _moe_kernel(cnt_ref, h_ref, comb_ref, rank_ref, combt_ref, rankt_ref, wg_ref, wu_ref, wd_ref, o_ref,
                xe_ref, y_ref, *, tm, rb):
    i = pl.program_id(0)
    e = pl.program_id(1)
    j = pl.program_id(2)
    nf = pl.num_programs(2)
    cnt = cnt_ref[i * N_EXPERTS + e]
    half, quarter = rb // 2, rb // 4
    nblk = cnt // rb
    tail0 = pl.multiple_of(nblk * rb, rb)
    rem = cnt - tail0

    def for_blocks(fn):
        lax.fori_loop(0, nblk, lambda b, carry: fn(pl.multiple_of(b * rb, rb), rb) or carry, 0)

        @pl.when(rem > half + quarter)
        def _():
            fn(tail0, rb)

        @pl.when(jnp.logical_and(rem > quarter, rem <= half + quarter))
        def _():
            fn(tail0, half)

        @pl.when(jnp.logical_or(jnp.logical_and(rem > 0, rem <= quarter),
                                jnp.logical_and(rem > half, rem <= half + quarter)))
        def _():
            fn(pl.multiple_of(tail0 + jnp.where(rem > half, half, 0), quarter), quarter)

    @pl.when(jnp.logical_and(e == 0, j == 0))
    def _():
        o_ref[...] = jnp.zeros_like(o_ref)

    @pl.when(j == 0)
    def _():
        key = jnp.where(combt_ref[pl.ds(e, 1), :] > 0.0, rankt_ref[pl.ds(e, 1), :], -1.0)

        def gather(r0, nr):
            want = (r0 + lax.broadcasted_iota(jnp.int32, (nr, tm), 0)).astype(F32)
            sel = jnp.where(key == want, 1.0, 0.0).astype(BF16)
            xe_ref[pl.ds(r0, nr), :] = _bf(_dot(sel, h_ref[...]))

        for_blocks(gather)

    def expert(r0, nr):
        rows = pl.ds(r0, nr)
        xb = xe_ref[rows, :]
        g = _dot(xb, wg_ref[0])
        u = _dot(xb, wu_ref[0])
        part = _dot(_bf(g * jax.nn.sigmoid(g) * u), wd_ref[0])

        @pl.when(j == 0)
        def _():
            y_ref[rows, :] = _bf(part)

        @pl.when(j > 0)
        def _():
            y_ref[rows, :] = _bf(y_ref[rows, :].astype(F32) + part)

    for_blocks(expert)

    @pl.when(j == nf - 1)
    def _():
        lane = lax.broadcasted_iota(jnp.int32, (tm, LANES), 1)
        rank_col = jnp.sum(jnp.where(lane == e, rank_ref[...], 0.0), axis=1, keepdims=True)
        w_col = jnp.sum(jnp.where(lane == e, comb_ref[...], 0.0), axis=1, keepdims=True)

        def scatter(r0, nr):
            want = (r0 + lax.broadcasted_iota(jnp.int32, (tm, nr), 1)).astype(F32)
            selw = _bf(jnp.where(rank_col == want, w_col, 0.0))
            o_ref[...] += _dot(selw, y_ref[pl.ds(r0, nr), :])

        for_blocks(scatter)


def _moe_ffn(hb, comb, rank, combt, rankt, counts, w_gu, w_down, tm):
    n, d = hb.shape
    ne, f, _ = w_down.shape
    tf = _pick_tile(f, (1792, 512, 256, 128))
    nf = f // tf
    rb = MOE_ROW_BLOCK
    kern = functools.partial(_moe_kernel, tm=tm, rb=rb)
    once = pl.Buffered(1)
    tokm = pl.BlockSpec((tm, LANES), lambda i, e, j, c: (i, 0), pipeline_mode=once)
    expm = pl.BlockSpec((N_EXPERTS, tm), lambda i, e, j, c: (0, i))
    grid_spec = pltpu.PrefetchScalarGridSpec(
        num_scalar_prefetch=1,
        grid=(n // tm, ne, nf),
        in_specs=[pl.BlockSpec((tm, d), lambda i, e, j, c: (i, 0), pipeline_mode=once), tokm, tokm, expm, expm,
                  pl.BlockSpec((1, d, tf), lambda i, e, j, c: (e, 0, j)),
                  pl.BlockSpec((1, d, tf), lambda i, e, j, c: (e, 0, j + nf)),
                  pl.BlockSpec((1, tf, d), lambda i, e, j, c: (e, j, 0))],
        out_specs=pl.BlockSpec((tm, d), lambda i, e, j, c: (i, 0), pipeline_mode=once),
        scratch_shapes=[pltpu.VMEM((tm, d), BF16), pltpu.VMEM((tm, d), BF16)])
    return pl.pallas_call(
        kern,
        grid_spec=grid_spec,
        out_shape=jax.ShapeDtypeStruct((n, d), F32),
        compiler_params=_cparams("parallel", "arbitrary", "arbitrary"),
        name="moe_swiglu_ffn",
    )(counts, hb, comb, rank, combt, rankt, w_gu, w_gu, w_down)


def _residual_kernel(x_ref, y_ref, ng_ref, mod_ref, o_ref, *, nsub, d):
    o_ref[...] = x_ref[...] + _mod_rows(mod_ref, nsub, 5, d) * _rms(y_ref[...].astype(F32), ng_ref[...])


def _gated_residual(x, y, gain_out, modblk):
    n, d = x.shape
    tm = _pick_tile(n, (512, 256))
    nsub = tm // MOD_ROWS
    tok = pl.BlockSpec((tm, d), lambda i: (i, 0))
    return pl.pallas_call(
        functools.partial(_residual_kernel, nsub=nsub, d=d),
        grid=(n // tm,),
        in_specs=[tok, tok, pl.BlockSpec((1, d), lambda i: (0, 0)),
                  pl.BlockSpec((nsub, 1, 6 * d), lambda i: (i, 0, 0))],
        out_specs=tok,
        out_shape=jax.ShapeDtypeStruct((n, d), F32),
        compiler_params=_cparams("parallel"),
        name="moe_gated_residual",
    )(x, y, gain_out, modblk)


def _rope_tables(t_len, c_len):
    pairs = HEAD_DIM // 4
    rows = t_len // GRID_W
    row = jnp.repeat(jnp.arange(rows, dtype=F32), GRID_W)
    col = jnp.tile(jnp.arange(GRID_W, dtype=F32), rows)
    freqs = ROPE_BASE ** (-jnp.arange(pairs, dtype=F32) / pairs)
    ar = row[:, None] * freqs
    ac = col[:, None] * freqs
    cos = jnp.concatenate([jnp.cos(ar), jnp.cos(ar), jnp.cos(ac), jnp.cos(ac)], axis=1)
    sin = jnp.concatenate([-jnp.sin(ar), jnp.sin(ar), -jnp.sin(ac), jnp.sin(ac)], axis=1)
    cos = jnp.concatenate([jnp.ones((c_len, HEAD_DIM), F32), cos], axis=0)
    sin = jnp.concatenate([jnp.zeros((c_len, HEAD_DIM), F32), sin], axis=0)
    return jnp.tile(cos, (1, 2)), jnp.tile(sin, (1, 2))


def _block_diag2(w):
    z = jnp.zeros_like(w[0])
    return jnp.concatenate([jnp.concatenate([w[0], z], axis=1), jnp.concatenate([z, w[1]], axis=1)], axis=0)


def kernel(x, c, ctx, c_ctx, w_mod, b_mod, norm_gain, w_in, qk_gain, rwkv_conv, decay_w0, decay_w2, iclr_a0, iclr_a2, key_k, bonus_rk, gate_g2, lnx_gain, lnx_bias, cmlp_ln_gain, cmlp_ln_bias, cmlp_ws, cmlp_bs, w_branch, w_out, ffn_w_gu, ffn_w_down, moe_router, moe_w_gu, moe_w_down):
    batch, t_len, d = x.shape
    c_len = ctx.shape[1]
    depth = w_mod.shape[0]
    s_tot = c_len + t_len
    n = batch * s_tot
    assert c_len % MOD_ROWS == 0 and t_len % MOD_ROWS == 0 and d % LANES == 0
    width = bonus_rk.shape[1] * bonus_rk.shape[2]
    nheads = width // HEAD_DIM

    xs = jnp.concatenate([ctx, x], axis=1).reshape(n, d)

    mod_rows = 8 * ((batch + 1 + 7) // 8)
    cvec = jnp.zeros((mod_rows, d), F32).at[0].set(c_ctx).at[1:batch + 1].set(c)
    mods = _modulation(cvec, w_mod, b_mod)
    mod_ctx = jnp.broadcast_to(mods[:, 0:1, None, :], (depth, batch, c_len // MOD_ROWS, 6 * d))
    mod_lat = jnp.broadcast_to(mods[:, 1:batch + 1, None, :], (depth, batch, t_len // MOD_ROWS, 6 * d))
    modblk_all = jnp.concatenate([mod_ctx, mod_lat], axis=2).reshape(depth, n // MOD_ROWS, 1, 6 * d)

    order = np.array(Q_HEAD_ORDER)
    nl = depth
    wq = w_in[:, :, 2048:2560].reshape(nl, d, nheads, HEAD_DIM)[:, :, order].reshape(nl, d, width)
    w_in_p = jnp.concatenate([
        w_in[:, :, 256:1792], wq, w_in[:, :, 2688:3712], w_in[:, :, 3712:6784],
        w_in[:, :, 0:128], w_in[:, :, 128:256], w_in[:, :, 1792:1920], w_in[:, :, 1920:2048],
        w_in[:, :, 2560:2688], jnp.zeros((nl, d, IN_PAD - 6784), F32)], axis=2).astype(BF16)
    wb = w_branch.astype(BF16)
    wb0 = wb[:, 0].reshape(nl, nheads, HEAD_DIM, d)[:, order].reshape(nl, width, d)
    wb = jnp.concatenate([wb0[:, None], wb[:, 1:]], axis=1)
    wo = w_out.astype(BF16)
    cos, sin = _rope_tables(t_len, c_len)
    qg = jnp.tile(qk_gain[:, 0], (1, 2))[:, None, :]
    kg = jnp.tile(qk_gain[:, 1], (1, 2))[:, None, :]
    ws_b = cmlp_ws.astype(BF16)
    bs_b = jnp.broadcast_to(cmlp_bs[..., None], cmlp_bs.shape + (CMLP_CHUNK,))
    w2s = jnp.stack([_block_diag2(decay_w2[l]) for l in range(nl)]).astype(BF16)
    a2s = jnp.stack([_block_diag2(iclr_a2[l]) for l in range(nl)]).astype(BF16)
    w0 = decay_w0.reshape(nl, 1, 2 * width)
    a0 = iclr_a0.reshape(nl, 1, 2 * width)
    rk = bonus_rk.reshape(nl, 1, width)
    g2 = gate_g2.astype(BF16)
    ffn_gu = ffn_w_gu.astype(BF16)
    ffn_dn = ffn_w_down.astype(BF16)
    moe_gu = moe_w_gu.astype(BF16)
    moe_dn = moe_w_down.astype(BF16)
    router_pad = jnp.pad(moe_router, ((0, 0), (0, 0), (0, LANES - moe_router.shape[2])))
    router_hi = router_pad.astype(BF16)
    router_pad = jnp.concatenate([router_hi, (router_pad - router_hi.astype(F32)).astype(BF16)], axis=2)

    for l in range(depth):
        modblk = modblk_all[l]
        ng = norm_gain[l]
        big = _norm_mod_matmul(xs, ng[0:1], modblk, w_in_p[l], 0, 1)
        qh, kbd, vt = _qk_prep(big, qg[l], kg[l], cos, sin, s_tot)
        oa = _attention(qh, kbd, vt, batch, s_tot, c_len)
        prep = _rwkv_prep(big, rwkv_conv[l], w0[l], w2s[l], a0[l], a2s[l],
                          key_k[l, 0:1], key_k[l, 1:2], rk[l], s_tot, c_len)
        yf, yb = _rwkv_scan(prep, batch, s_tot, c_len)
        ob = _rwkv_readout(yf, yb, prep[1], big, g2[l], lnx_gain[l][None], lnx_bias[l][None])
        oc = _chunk_mlp(big, cmlp_ln_gain[l][None], cmlp_ln_bias[l][None], ws_b[l], bs_b[l])
        xs = _merge(oa, ob, oc, big, xs, wb[l], wo[l], ng[1:2], modblk)
        if l % 2 == 0:
            xs = _dense_ffn(xs, ng[2:3], ng[3:4], modblk, ffn_gu[l // 2], ffn_dn[l // 2])
        else:
            tmoe = _pick_tile(n, (2048, 1024, 512, 256))
            hb, comb, rank, combt, rankt, cnt = _router(xs, ng[2:3], modblk, router_pad[l // 2], tmoe)
            counts = cnt[:, 0, :N_EXPERTS].astype(jnp.int32).reshape(-1)
            y = _moe_ffn(hb, comb, rank, combt, rankt, counts, moe_gu[l // 2], moe_dn[l // 2], tmoe)
            xs = _gated_residual(xs, y, ng[3:4], modblk)
    return xs.reshape(batch, s_tot, d)[:, c_len:, :]
```
